```python
import jax
import jax.numpy as jnp
from jax import lax
import numpy as np

D_MODEL = 1024
BATCH = 2
SEQ = 16384
DEPTH = 2

GRID_W = 64
CTX_LEN = 256

ML_HEADS = 4
ML_DH = 64
ML_W = ML_HEADS * ML_DH
ML_CHUNK = 64
FT_GROUPS = 4
FT_GC = 64
FT_W = FT_GROUPS * FT_GC
GQ_KV = 2
GQ_G = 4
GQ_HEADS = GQ_KV * GQ_G
GQ_DH = 64
GQ_W = GQ_HEADS * GQ_DH
GQ_BLOCK = 128
ROPE_PAIRS = GQ_DH // 4
ROPE_BASE = 10000.0
GL_HEADS = 4
GL_DK = 64
GL_DV = 64
GL_W = GL_HEADS * GL_DV
GL_RANK = 16
GL_TAU = 16.0
GL_CHUNK = 64
FF_DENSE = 2816
N_EXPERTS = 8
TOP_K = 2
FF_EXPERT = 1408

N_BRANCH = 4
BRANCH_WIDTHS = (ML_W, FT_W, GQ_W, GL_W)
MIX_W = ML_W + FT_W + GQ_W + GL_W
N_ADA = 6
LN_EPS = 1e-6
DEEPNORM_ALPHA = (2.0 * DEPTH) ** 0.25
DEEPNORM_BETA = (8.0 * DEPTH) ** -0.25

IN_COLS = (
    ('ml_q', ML_W), ('ml_k', ML_W), ('ml_v', ML_W), ('ml_o', ML_W),
    ('ml_if', ML_HEADS), ('ml_ff', ML_HEADS), ('ml_ib', ML_HEADS), ('ml_fb', ML_HEADS),
    ('ft', FT_W),
    ('gq_q', GQ_W), ('gq_k', GQ_KV * GQ_DH), ('gq_v', GQ_KV * GQ_DH),
    ('gl_q', GL_HEADS * GL_DK), ('gl_k', GL_HEADS * GL_DK), ('gl_v', GL_HEADS * GL_DV), ('gl_r', GL_W),
    ('gl_af', GL_RANK), ('gl_ab', GL_RANK),
)
N_IN = sum(w for _, w in IN_COLS)

kernel_name = 'hybrid_diffusion_mlstm_fnet_gqa_gla_moe'

F32 = jnp.float32


def _layernorm(x):
    xf = x.astype(F32)
    xc = xf - jnp.mean(xf, axis=-1, keepdims=True)
    var = jnp.mean(xc * xc, axis=-1, keepdims=True)
    return (xc * lax.rsqrt(var + LN_EPS)).astype(x.dtype)


def _rms(x):
    xf = x.astype(F32)
    return (xf * lax.rsqrt(jnp.mean(xf * xf, axis=-1, keepdims=True) + LN_EPS)).astype(x.dtype)


def _modulate(x, shift, scale):
    return _layernorm(x) * (1.0 + scale) + shift


def _post_ln(x, g, b):
    return _layernorm(x) * g + b


def _split_cols(p):
    out = {}
    off = 0
    for name, w in IN_COLS:
        out[name] = p[..., off:off + w]
        off += w
    return out


def _heads(a, n):
    b_, t, w = a.shape
    return a.reshape(b_, t, n, w // n).transpose(0, 2, 1, 3)


def _merge_heads(a):
    b_, n, t, d = a.shape
    return a.transpose(0, 2, 1, 3).reshape(b_, t, n * d)


def _to_chunks(a, size):
    nc = a.shape[2] // size
    a = a.reshape(a.shape[:2] + (nc, size) + a.shape[3:])
    return jnp.moveaxis(a, 2, 0)


def _from_chunks(a):
    a = jnp.moveaxis(a, 0, 2)
    return a.reshape(a.shape[:2] + (a.shape[2] * a.shape[3],) + a.shape[4:])


def _flip(a):
    return jnp.flip(a, axis=2)


def _mlstm_scan(q, k, v, li, lf, state):
    size = ML_CHUNK
    causal = jnp.tril(jnp.ones((size, size), dtype=bool))

    def step(carry, inp):
        cmat, nvec, m = carry
        qc, kc, vc, ic, fc = inp
        b = jnp.cumsum(fc, axis=-1)
        d = jnp.where(causal, b[..., :, None] - b[..., None, :] + ic[..., None, :], -jnp.inf)
        inter = b + m[..., None]
        m_t = jnp.maximum(inter, jnp.max(d, axis=-1))
        w_intra = jnp.exp(d - m_t[..., None]) * jnp.einsum('bhtd,bhsd->bhts', qc, kc)
        w_inter = jnp.exp(inter - m_t)
        num = (jnp.einsum('bhts,bhsd->bhtd', w_intra, vc)
               + w_inter[..., None] * jnp.einsum('bhvk,bhtk->bhtv', cmat, qc))
        den = jnp.sum(w_intra, axis=-1) + w_inter * jnp.einsum('bhk,bhtk->bht', nvec, qc)
        h = num / jnp.maximum(jnp.abs(den), jnp.exp(-m_t))[..., None]
        b_last = b[..., -1]
        g = b_last[..., None] - b + ic
        m_new = jnp.maximum(b_last + m, jnp.max(g, axis=-1))
        ws = jnp.exp(g - m_new[..., None])
        wc = jnp.exp(b_last + m - m_new)
        cmat = wc[..., None, None] * cmat + jnp.einsum('bhs,bhsv,bhsk->bhvk', ws, vc, kc)
        nvec = wc[..., None] * nvec + jnp.einsum('bhs,bhsk->bhk', ws, kc)
        return (cmat, nvec, m_new), h

    xs = (_to_chunks(q, size), _to_chunks(k, size), _to_chunks(v, size),
          _to_chunks(li, size), _to_chunks(lf, size))
    state, h = lax.scan(step, state, xs)
    return _from_chunks(h), state


def _mlstm_zero(b_):
    return (jnp.zeros((b_, ML_HEADS, ML_DH, ML_DH), F32), jnp.zeros((b_, ML_HEADS, ML_DH), F32),
            jnp.zeros((b_, ML_HEADS), F32))


def _mlstm_stream(cols, gate_b, init_f, init_b):
    q = _heads(cols['ml_q'], ML_HEADS).astype(F32)
    k = _heads(cols['ml_k'], ML_HEADS).astype(F32) * (ML_DH ** -0.5)
    v = _heads(cols['ml_v'], ML_HEADS).astype(F32)

    def gate(name, j):
        return jnp.swapaxes(cols[name].astype(F32) + gate_b[j].astype(F32), 1, 2)

    li_f, lf_f = gate('ml_if', 0), jax.nn.log_sigmoid(gate('ml_ff', 1))
    li_b, lf_b = gate('ml_ib', 2), jax.nn.log_sigmoid(gate('ml_fb', 3))
    h_f, st_f = _mlstm_scan(q, k, v, li_f, lf_f, init_f)
    h_b, st_b = _mlstm_scan(_flip(q), _flip(k), _flip(v), _flip(li_b), _flip(lf_b), init_b)
    return h_f + _flip(h_b), st_f, st_b


def _mlstm_finish(h, o_pre, norm_g):
    hn = _merge_heads(_layernorm(h)) * norm_g.astype(F32)
    return hn.astype(o_pre.dtype) * jax.nn.sigmoid(o_pre)


def _fourier(u):
    b_, t, _ = u.shape
    ug = u.astype(F32).reshape(b_, t, FT_GROUPS, FT_GC)
    y = jnp.fft.fft2(ug, axes=(1, 3), norm='ortho').real
    return y.reshape(b_, t, FT_W).astype(u.dtype)


def _rope_tables(rows, dtype):
    row = jnp.repeat(jnp.arange(rows, dtype=F32), GRID_W)
    col = jnp.tile(jnp.arange(GRID_W, dtype=F32), rows)
    inv = jnp.power(ROPE_BASE, -jnp.arange(ROPE_PAIRS, dtype=F32) / ROPE_PAIRS)
    ang = jnp.stack([row[:, None] * inv, col[:, None] * inv], axis=1)
    return jnp.cos(ang).astype(dtype), jnp.sin(ang).astype(dtype)


def _rope(x, cos, sin):
    xs = x.reshape(x.shape[:-1] + (2, 2, ROPE_PAIRS))
    x1, x2 = xs[..., 0, :], xs[..., 1, :]
    out = jnp.stack([x1 * cos - x2 * sin, x1 * sin + x2 * cos], axis=-2)
    return out.reshape(x.shape)


def _gqa_q(cols, g):
    b_, t, _ = cols['gq_q'].shape
    q = cols['gq_q'].reshape(b_, t, GQ_KV, GQ_G, GQ_DH).transpose(0, 2, 3, 1, 4)
    return _rms(q) * g


def _gqa_kv(cols, g):
    return _rms(_heads(cols['gq_k'], GQ_KV)) * g, _heads(cols['gq_v'], GQ_KV)


def _gqa_attend(q, k, v):
    b_, kv, g, t, dh = q.shape
    nb = t // GQ_BLOCK
    qb = jnp.moveaxis(q.reshape(b_, kv, g, nb, GQ_BLOCK, dh), 3, 0)

    def attend(qblk):
        s = jnp.einsum('bkgqd,bksd->bkgqs', qblk, k).astype(F32) * (GQ_DH ** -0.5)
        p = jax.nn.softmax(s, axis=-1).astype(v.dtype)
        return jnp.einsum('bkgqs,bksd->bkgqd', p, v)

    o = jnp.moveaxis(lax.map(attend, qb), 0, 3).reshape(b_, kv, g, t, dh)
    return o.transpose(0, 3, 1, 2, 4).reshape(b_, t, kv * g * dh)


def _gla_scan(q, k, v, la, s0):
    size = GL_CHUNK
    causal = jnp.tril(jnp.ones((size, size), dtype=bool))[:, :, None]

    def step(s, inp):
        qc, kc, vc, ac = inp
        g = jnp.cumsum(ac, axis=2)
        diff = jnp.where(causal, g[:, :, :, None, :] - g[:, :, None, :, :], -jnp.inf)
        a = jnp.einsum('bhtk,bhsk,bhtsk->bhts', qc, kc, jnp.exp(diff))
        o = (jnp.einsum('bhts,bhsv->bhtv', a, vc)
             + jnp.einsum('bhtk,bhkv->bhtv', qc * jnp.exp(g), s))
        g_last = g[:, :, -1:, :]
        s = (jnp.exp(g_last[:, :, 0, :])[..., None] * s
             + jnp.einsum('bhsk,bhsv->bhkv', kc * jnp.exp(g_last - g), vc))
        return s, o

    xs = (_to_chunks(q, size), _to_chunks(k, size), _to_chunks(v, size), _to_chunks(la, size))
    s, o = lax.scan(step, s0, xs)
    return _from_chunks(o), s


def _gla_zero(b_):
    return jnp.zeros((b_, GL_HEADS, GL_DK, GL_DV), F32)


def _gla_stream(cols, w2, b2, init_f, init_b):
    q = _heads(cols['gl_q'], GL_HEADS).astype(F32) * (GL_DK ** -0.5)
    k = _heads(cols['gl_k'], GL_HEADS).astype(F32)
    v = _heads(cols['gl_v'], GL_HEADS).astype(F32)

    def log_decay(name, j):
        a = (cols[name] @ w2[j] + b2[j]).astype(F32)
        return _heads(jax.nn.log_sigmoid(a) / GL_TAU, GL_HEADS)

    o_f, s_f = _gla_scan(q, k, v, log_decay('gl_af', 0), init_f)
    o_b, s_b = _gla_scan(_flip(q), _flip(k), _flip(v), _flip(log_decay('gl_ab', 1)), init_b)
    return o_f + _flip(o_b), s_f, s_b


def _gla_finish(o, r_pre, norm_g):
    on = _merge_heads(_rms(o)) * norm_g.astype(F32)
    return on.astype(r_pre.dtype) * jax.nn.silu(r_pre)


def _merge_branches(h, branches, w_branch, w_gate, b_gate, w_out):
    gates = jax.nn.sigmoid(h @ w_gate + b_gate)
    mixed = None
    off = 0
    for j, br in enumerate(branches):
        wd = br.shape[-1]
        u = gates[..., j * D_MODEL:(j + 1) * D_MODEL] * (br @ w_branch[off:off + wd])
        mixed = u if mixed is None else mixed + u
        off += wd
    return mixed @ w_out


def _swiglu(x, wg, wu, wd):
    return (jax.nn.silu(x @ wg) * (x @ wu)) @ wd


def _moe(x, wr, br, wg, wu, wd):
    logits = (x @ wr).astype(F32)
    _, idx = lax.top_k(logits + br.astype(F32), TOP_K)
    w = jax.nn.softmax(jnp.take_along_axis(logits, idx, axis=-1), axis=-1)
    dense = jnp.sum(jax.nn.one_hot(idx, N_EXPERTS, dtype=F32) * w[..., None], axis=-2).astype(x.dtype)
    out = dense[..., 0:1] * _swiglu(x, wg[0], wu[0], wd[0])
    for e in range(1, N_EXPERTS):
        out = out + dense[..., e:e + 1] * _swiglu(x, wg[e], wu[e], wd[e])
    return out


def setup_inputs(seed: int = 0) -> dict:
    key = jax.random.key(seed)
    keys = iter(jax.random.split(key, 48))

    def nrm(shape, scale):
        return jax.random.normal(next(keys), shape, jnp.float32) * scale

    d = D_MODEL
    nl = DEPTH
    n_dense = (DEPTH + 1) // 2
    n_moe = DEPTH // 2
    f_bias = jnp.linspace(3.0, 6.0, ML_HEADS, dtype=jnp.float32)
    z_bias = jnp.zeros((ML_HEADS,), jnp.float32)
    gate_base = jnp.stack([z_bias, f_bias, z_bias, f_bias])
    w_branch = jnp.concatenate([nrm((nl, w, d), w ** -0.5) for w in BRANCH_WIDTHS], axis=1)
    return {
        'x': nrm((BATCH, SEQ, d), 1.0),
        'c': nrm((BATCH, d), 1.0),
        'ctx': nrm((BATCH, CTX_LEN, d), 1.0),
        'c_ctx': nrm((d,), 1.0),
        'w_ada': nrm((nl, d, N_ADA * d), 0.5 * d ** -0.5),
        'b_ada': nrm((nl, N_ADA * d), 0.01),
        'w_in': nrm((nl, d, N_IN), d ** -0.5),
        'ml_gate_b': gate_base + nrm((nl, 4, ML_HEADS), 0.1),
        'ml_norm_g': 1.0 + nrm((nl, ML_W), 0.02),
        'gq_qnorm_g': 1.0 + nrm((nl, GQ_DH), 0.02),
        'gq_knorm_g': 1.0 + nrm((nl, GQ_DH), 0.02),
        'gl_w2': nrm((nl, 2, GL_RANK, GL_HEADS * GL_DK), GL_RANK ** -0.5),
        'gl_b2': nrm((nl, 2, GL_HEADS * GL_DK), 0.1),
        'gl_norm_g': 1.0 + nrm((nl, GL_W), 0.02),
        'w_branch': w_branch,
        'w_gate': nrm((nl, d, N_BRANCH * d), d ** -0.5),
        'b_gate': nrm((nl, N_BRANCH * d), 0.01),
        'w_out': nrm((nl, d, d), DEEPNORM_BETA * d ** -0.5),
        'ln1_g': 1.0 + nrm((nl, d), 0.02),
        'ln1_b': nrm((nl, d), 0.01),
        'ln2_g': 1.0 + nrm((nl, d), 0.02),
        'ln2_b': nrm((nl, d), 0.01),
        'ffd_wg': nrm((n_dense, d, FF_DENSE), d ** -0.5),
        'ffd_wu': nrm((n_dense, d, FF_DENSE), d ** -0.5),
        'ffd_wd': nrm((n_dense, FF_DENSE, d), DEEPNORM_BETA * FF_DENSE ** -0.5),
        'moe_wr': nrm((n_moe, d, N_EXPERTS), d ** -0.5),
        'moe_br': nrm((n_moe, N_EXPERTS), 0.01),
        'moe_wg': nrm((n_moe, N_EXPERTS, d, FF_EXPERT), d ** -0.5),
        'moe_wu': nrm((n_moe, N_EXPERTS, d, FF_EXPERT), d ** -0.5),
        'moe_wd': nrm((n_moe, N_EXPERTS, FF_EXPERT, d), DEEPNORM_BETA * FF_EXPERT ** -0.5),
    }


def reference(x, c, ctx, c_ctx, w_ada, b_ada, w_in, ml_gate_b, ml_norm_g, gq_qnorm_g, gq_knorm_g,
              gl_w2, gl_b2, gl_norm_g, w_branch, w_gate, b_gate, w_out, ln1_g, ln1_b, ln2_g, ln2_b,
              ffd_wg, ffd_wu, ffd_wd, moe_wr, moe_br, moe_wg, moe_wu, moe_wd):
    b_ = x.shape[0]
    rows = x.shape[1] // GRID_W
    cos, sin = _rope_tables(rows, x.dtype)
    alpha = DEEPNORM_ALPHA
    xc = ctx
    for l in range(DEPTH):
        last = l == DEPTH - 1
        ada = (jax.nn.silu(c) @ w_ada[l] + b_ada[l]).reshape(b_, N_ADA, 1, D_MODEL)
        ada_c = (jax.nn.silu(c_ctx) @ w_ada[l] + b_ada[l]).reshape(N_ADA, 1, D_MODEL)

        h = _modulate(x, ada[:, 0], ada[:, 1])
        hc = _modulate(xc, ada_c[0], ada_c[1])
        pl = _split_cols(h @ w_in[l])
        pc = _split_cols(hc @ w_in[l])

        ml_c, ml_sf, ml_sb = _mlstm_stream(pc, ml_gate_b[l], _mlstm_zero(b_), _mlstm_zero(b_))
        gl_c, gl_sf, gl_sb = _gla_stream(pc, gl_w2[l], gl_b2[l], _gla_zero(b_), _gla_zero(b_))
        k_c, v_c = _gqa_kv(pc, gq_knorm_g[l])

        ml_l, _, _ = _mlstm_stream(pl, ml_gate_b[l], ml_sf, ml_sb)
        gl_l, _, _ = _gla_stream(pl, gl_w2[l], gl_b2[l], gl_sf, gl_sb)
        k_l, v_l = _gqa_kv(pl, gq_knorm_g[l])
        q_l = _rope(_gqa_q(pl, gq_qnorm_g[l]), cos, sin)
        att_l = _gqa_attend(q_l, jnp.concatenate([_rope(k_l, cos, sin), k_c], axis=2),
                            jnp.concatenate([v_l, v_c], axis=2))
        br_l = [_mlstm_finish(ml_l, pl['ml_o'], ml_norm_g[l]), _fourier(pl['ft']), att_l,
                _gla_finish(gl_l, pl['gl_r'], gl_norm_g[l])]
        y = _merge_branches(h, br_l, w_branch[l], w_gate[l], b_gate[l], w_out[l])
        x_mid = _post_ln(alpha * x + ada[:, 2] * y, ln1_g[l], ln1_b[l])

        if not last:
            att_c = _gqa_attend(_gqa_q(pc, gq_qnorm_g[l]), k_c, v_c)
            br_c = [_mlstm_finish(ml_c, pc['ml_o'], ml_norm_g[l]), _fourier(pc['ft']), att_c,
                    _gla_finish(gl_c, pc['gl_r'], gl_norm_g[l])]
            yc = _merge_branches(hc, br_c, w_branch[l], w_gate[l], b_gate[l], w_out[l])
            xc = _post_ln(alpha * xc + ada_c[2] * yc, ln1_g[l], ln1_b[l])

        h2 = _modulate(x_mid, ada[:, 3], ada[:, 4])
        if l % 2 == 0:
            j = l // 2
            f = _swiglu(h2, ffd_wg[j], ffd_wu[j], ffd_wd[j])
            if not last:
                h2c = _modulate(xc, ada_c[3], ada_c[4])
                fc = _swiglu(h2c, ffd_wg[j], ffd_wu[j], ffd_wd[j])
                xc = _post_ln(alpha * xc + ada_c[5] * fc, ln2_g[l], ln2_b[l])
        else:
            j = l // 2
            f = _moe(h2, moe_wr[j], moe_br[j], moe_wg[j], moe_wu[j], moe_wd[j])
            if not last:
                h2c = _modulate(xc, ada_c[3], ada_c[4])
                fc = _moe(h2c, moe_wr[j], moe_br[j], moe_wg[j], moe_wu[j], moe_wd[j])
                xc = _post_ln(alpha * xc + ada_c[5] * fc, ln2_g[l], ln2_b[l])
        x = _post_ln(alpha * x_mid + ada[:, 5] * f, ln2_g[l], ln2_b[l])
    return x
```

```python
import functools
import math

import jax
import jax.numpy as jnp
import numpy as np
from jax import lax
from jax.experimental import pallas as pl
from jax.experimental.pallas import tpu as pltpu

F32 = jnp.float32
BF16 = jnp.bfloat16

D_MODEL = 1024
GRID_W = 64
HEAD_DIM = 64
ML_HEADS = 4
ML_W = ML_HEADS * HEAD_DIM
FT_GROUPS = 4
FT_GC = 64
FT_W = FT_GROUPS * FT_GC
GQ_KV = 2
GQ_G = 4
GQ_W = GQ_KV * GQ_G * HEAD_DIM
GQ_KW = GQ_KV * HEAD_DIM
ROPE_PAIRS = HEAD_DIM // 4
ROPE_BASE = 10000.0
GL_HEADS = 4
GL_W = GL_HEADS * HEAD_DIM
GL_RANK = 16
GL_TAU = 16.0
N_EXPERTS = 8
N_ADA = 6
LN_EPS = 1e-6
SMALL_W = 128

ML_CHUNK = 256
GL_CHUNK = 128
GL_SUB = 16
GL_EXP_CLAMP = 80.0
ATT_TQ = 256
ATT_TK = 1024
NEG_BIG = -1e30

VMEM_LIMIT = 56 * 1024 * 1024


def _cparams(*sem):
    return pltpu.CompilerParams(dimension_semantics=sem, vmem_limit_bytes=VMEM_LIMIT)


def _resident(shape):
    nd = len(shape)
    return pl.BlockSpec(shape, lambda *_: (0,) * nd, pipeline_mode=pl.Buffered(1))


def _bdot(a, b):
    return jnp.dot(a.astype(BF16), b.astype(BF16), preferred_element_type=F32)


def _split2(x):
    hi = x.astype(BF16)
    lo = (x - hi.astype(F32)).astype(BF16)
    return hi, lo


def _split3(x):
    a = x.astype(BF16)
    r = x - a.astype(F32)
    b = r.astype(BF16)
    c = (r - b.astype(F32)).astype(BF16)
    return a, b, c


def _dot_exact_rhs(x, m_bf16):
    a, b, c = _split3(x)
    d = functools.partial(jnp.dot, preferred_element_type=F32)
    return d(a, m_bf16) + d(b, m_bf16) + d(c, m_bf16)


def _dot_exact_lhs(m_bf16, x):
    a, b, c = _split3(x)
    d = functools.partial(jnp.dot, preferred_element_type=F32)
    return d(m_bf16, a) + d(m_bf16, b) + d(m_bf16, c)


def _dot3(a_hi, a_lo, b_hi, b_lo):
    d = functools.partial(jnp.dot, preferred_element_type=F32)
    return d(a_hi, b_hi) + d(a_hi, b_lo) + d(a_lo, b_hi)


def _sigmoid(x):
    return 1.0 / (1.0 + jnp.exp(-x))


def _silu(x):
    return x * _sigmoid(x)


def _log_sigmoid(x):
    return jnp.minimum(x, 0.0) - jnp.log(1.0 + jnp.exp(-jnp.abs(x)))


def _layernorm(x):
    mu = jnp.mean(x, axis=-1, keepdims=True)
    xc = x - mu
    var = jnp.mean(xc * xc, axis=-1, keepdims=True)
    return xc * lax.rsqrt(var + LN_EPS)


def _modulate(x, shift, scale):
    return _layernorm(x) * (1.0 + scale) + shift


def _group_ones(width):
    r = lax.broadcasted_iota(jnp.int32, (width, width), 0) >> 6
    c = lax.broadcasted_iota(jnp.int32, (width, width), 1) >> 6
    return jnp.where(r == c, 1.0, 0.0).astype(BF16)


def _group_mean(x, ones):
    return _dot_exact_rhs(x, ones) * (1.0 / HEAD_DIM)


def _tri(n, upper):
    r = lax.broadcasted_iota(jnp.int32, (n, n), 0)
    c = lax.broadcasted_iota(jnp.int32, (n, n), 1)
    keep = (c >= r) if upper else (c <= r)
    return jnp.where(keep, 1.0, 0.0).astype(BF16)


def _ada_kernel(c_ref, w_ref, b_ref, o_ref):
    o_ref[...] = _bdot(_silu(c_ref[...]), w_ref[...]) + b_ref[...]


def _ada_call(cc, w, b):
    rows, d = cc.shape
    n = w.shape[1]
    tn = 1024
    return pl.pallas_call(
        _ada_kernel,
        grid=(n // tn,),
        in_specs=[pl.BlockSpec((rows, d), lambda j: (0, 0)),
                  pl.BlockSpec((d, tn), lambda j: (0, j)),
                  pl.BlockSpec((1, tn), lambda j: (0, j))],
        out_specs=pl.BlockSpec((rows, tn), lambda j: (0, j)),
        out_shape=jax.ShapeDtypeStruct((rows, n), F32),
        compiler_params=_cparams("parallel"),
        name="ada",
    )(cc, w, b)


def _inproj_kernel(x_ref, mod_ref, w_ml, w_ft, w_gq, w_gl, w_sm, o_ml, o_ft, o_gq, o_gl, o_sm):
    h = _modulate(x_ref[0], mod_ref[0, 0:1, :], mod_ref[0, 1:2, :]).astype(BF16)
    for w, o in ((w_ml, o_ml), (w_ft, o_ft), (w_gq, o_gq), (w_gl, o_gl), (w_sm, o_sm)):
        o[0] = jnp.dot(h, w[...], preferred_element_type=F32)


def _inproj_call(x, mod, ws, tm):
    b_, t, d = x.shape
    widths = [w.shape[1] for w in ws]
    return pl.pallas_call(
        _inproj_kernel,
        grid=(b_, t // tm),
        in_specs=[pl.BlockSpec((1, tm, d), lambda b, i: (b, i, 0)),
                  pl.BlockSpec((1, N_ADA, d), lambda b, i: (b, 0, 0))]
                 + [_resident(w.shape) for w in ws],
        out_specs=[pl.BlockSpec((1, tm, n), lambda b, i: (b, i, 0)) for n in widths],
        out_shape=[jax.ShapeDtypeStruct((b_, t, n), F32) for n in widths],
        compiler_params=_cparams("parallel", "parallel"),
        name="inproj",
    )(x, mod, *ws)


def _mlstm_kernel(qkv_f, sm_f, qkv_b, sm_b, bias_ref, s0_ref, m0_ref,
                  hf_ref, hb_ref, st_ref, mt_ref, s_scr, m_scr, *, chunk):
    i = pl.program_id(1)
    n_l = chunk

    @pl.when(i == 0)
    def _():
        s_scr[...] = s0_ref[0]
        m_scr[...] = m0_ref[0]

    row = lax.broadcasted_iota(jnp.int32, (n_l, n_l), 0)
    col = lax.broadcasted_iota(jnp.int32, (n_l, n_l), 1)
    ones_v = jnp.ones((n_l, HEAD_DIM), F32)
    for d, (qkv_ref, sm_ref, h_ref) in enumerate(((qkv_f, sm_f, hf_ref), (qkv_b, sm_b, hb_ref))):
        rev = d == 1
        blk = qkv_ref[0]
        pre = sm_ref[0] + bias_ref[...]
        bcum = _dot_exact_lhs(_tri(n_l, upper=rev), _log_sigmoid(pre))
        pre_t = pre.T
        b_t = bcum.T
        k_t = blk[:, ML_W:2 * ML_W].T
        mask = (col >= row) if rev else (col <= row)
        last = 0 if rev else n_l - 1
        outs = []
        for h in range(ML_HEADS):
            ci = 8 * d + h
            cf = 8 * d + 4 + h
            idx = 4 * d + h
            q = blk[:, h * HEAD_DIM:(h + 1) * HEAD_DIM].astype(BF16)
            kt = k_t[h * HEAD_DIM:(h + 1) * HEAD_DIM, :]
            v = blk[:, 2 * ML_W + h * HEAD_DIM:2 * ML_W + (h + 1) * HEAD_DIM]
            v_ext = jnp.concatenate([v, ones_v], axis=1).astype(BF16)
            bc = bcum[:, cf:cf + 1]
            br = b_t[cf:cf + 1, :]
            ir = pre_t[ci:ci + 1, :]
            state = s_scr[idx]
            m_prev = m_scr[idx][:, 0:1]

            dmat = jnp.where(mask, bc - br + ir, -jnp.inf)
            inter = bc + m_prev
            m_t = jnp.maximum(inter, jnp.max(dmat, axis=1, keepdims=True))
            w_intra = jnp.exp(dmat - m_t) * jnp.dot(q, kt.astype(BF16), preferred_element_type=F32)
            w_inter = jnp.exp(inter - m_t)
            qs = jnp.dot(q, state.astype(BF16), preferred_element_type=F32)
            num = (jnp.dot(w_intra.astype(BF16), v_ext, preferred_element_type=F32)[:, :HEAD_DIM]
                   + w_inter * qs[:, :HEAD_DIM])
            den = jnp.sum(w_intra, axis=1, keepdims=True) + w_inter * qs[:, HEAD_DIM:HEAD_DIM + 1]
            outs.append(num / jnp.maximum(jnp.abs(den), jnp.exp(-m_t)))

            b_last = bc[last:last + 1, :]
            g_row = b_last - br + ir
            m_new = jnp.maximum(b_last + m_prev, jnp.max(g_row, axis=1, keepdims=True))
            ws = jnp.exp(g_row - m_new)
            wc = jnp.exp(b_last + m_prev - m_new)
            s_scr[idx] = wc * state + jnp.dot((kt * ws).astype(BF16), v_ext, preferred_element_type=F32)
            m_scr[idx] = jnp.broadcast_to(m_new, (1, SMALL_W))
        h_ref[0] = jnp.concatenate(outs, axis=1)

    @pl.when(i == pl.num_programs(1) - 1)
    def _():
        st_ref[0] = s_scr[...]
        mt_ref[0] = m_scr[...]


def _mlstm_call(ml, small, bias, s0, m0):
    b_, t, _ = ml.shape
    chunk = min(ML_CHUNK, t)
    n = t // chunk
    fwd = lambda b, i: (b, i, 0)
    bwd = lambda b, i: (b, n - 1 - i, 0)
    state_spec = pl.BlockSpec((1, 2 * ML_HEADS, HEAD_DIM, 2 * HEAD_DIM), lambda b, i: (b, 0, 0, 0))
    m_spec = pl.BlockSpec((1, 2 * ML_HEADS, 1, SMALL_W), lambda b, i: (b, 0, 0, 0))
    return pl.pallas_call(
        functools.partial(_mlstm_kernel, chunk=chunk),
        grid=(b_, n),
        in_specs=[pl.BlockSpec((1, chunk, 3 * ML_W), fwd), pl.BlockSpec((1, chunk, SMALL_W), fwd),
                  pl.BlockSpec((1, chunk, 3 * ML_W), bwd), pl.BlockSpec((1, chunk, SMALL_W), bwd),
                  pl.BlockSpec((1, SMALL_W), lambda b, i: (0, 0)), state_spec, m_spec],
        out_specs=[pl.BlockSpec((1, chunk, ML_W), fwd), pl.BlockSpec((1, chunk, ML_W), bwd),
                   state_spec, m_spec],
        out_shape=[jax.ShapeDtypeStruct((b_, t, ML_W), F32), jax.ShapeDtypeStruct((b_, t, ML_W), F32),
                   jax.ShapeDtypeStruct(s0.shape, F32), jax.ShapeDtypeStruct(m0.shape, F32)],
        scratch_shapes=[pltpu.VMEM((2 * ML_HEADS, HEAD_DIM, 2 * HEAD_DIM), F32),
                        pltpu.VMEM((2 * ML_HEADS, 1, SMALL_W), F32)],
        compiler_params=_cparams("parallel", "arbitrary"),
        name="mlstm",
    )(ml, small, ml, small, bias, s0, m0)


def _gla_kernel(qkv_f, sm_f, qkv_b, sm_b, w2_ref, b2_ref, s0_ref,
                of_ref, ob_ref, st_ref, s_scr, *, chunk):
    i = pl.program_id(1)
    n_l = chunk
    n_sub = n_l // GL_SUB
    width = GL_W

    @pl.when(i == 0)
    def _():
        s_scr[...] = s0_ref[0]

    lane = lax.broadcasted_iota(jnp.int32, (GL_SUB, width), 1) >> 6
    head_masks = [lane == h for h in range(GL_HEADS)]
    bd_mask = ((lax.broadcasted_iota(jnp.int32, (width, width), 0) >> 6)
               == (lax.broadcasted_iota(jnp.int32, (width, width), 1) >> 6))
    for d, (qkv_ref, sm_ref, o_ref) in enumerate(((qkv_f, sm_f, of_ref), (qkv_b, sm_b, ob_ref))):
        rev = d == 1
        blk = qkv_ref[0]
        q = blk[:, 0:width]
        k = blk[:, width:2 * width]
        v = blk[:, 2 * width:3 * width]
        a = _bdot(sm_ref[0], w2_ref[...])[:, d * width:(d + 1) * width] + b2_ref[d:d + 1, :]
        la = _log_sigmoid(a) * (1.0 / GL_TAU)
        g = _dot_exact_lhs(_tri(n_l, upper=rev), la)
        last = 0 if rev else n_l - 1
        g_end = g[last:last + 1, :]
        state = s_scr[d]
        o_inter = lax.dot_general((q * jnp.exp(g)).astype(BF16), state.astype(BF16),
                                  (((1,), (1,)), ((), ())), preferred_element_type=F32)
        v_bf = v.astype(BF16)
        pieces = []
        for s in range(n_sub):
            lo = s * GL_SUB
            hi = lo + GL_SUB
            if rev:
                k_lo, k_hi, ref_row = lo, n_l, hi - 1
            else:
                k_lo, k_hi, ref_row = 0, hi, lo
            r = g[ref_row:ref_row + 1, :]
            qt = q[lo:hi] * jnp.exp(g[lo:hi] - r)
            kt = k[k_lo:k_hi] * jnp.exp(jnp.minimum(r - g[k_lo:k_hi], GL_EXP_CLAMP))
            qstack = jnp.concatenate([jnp.where(hm, qt, 0.0) for hm in head_masks], axis=0)
            amat = lax.dot_general(qstack.astype(BF16), kt.astype(BF16),
                                   (((1,), (1,)), ((), ())), preferred_element_type=F32)
            n_k = k_hi - k_lo
            t_idx = lo + (lax.broadcasted_iota(jnp.int32, (GL_HEADS * GL_SUB, n_k), 0) & (GL_SUB - 1))
            s_idx = k_lo + lax.broadcasted_iota(jnp.int32, (GL_HEADS * GL_SUB, n_k), 1)
            keep = (s_idx >= t_idx) if rev else (s_idx <= t_idx)
            amat = jnp.where(keep, amat, 0.0)
            ov = jnp.dot(amat.astype(BF16), v_bf[k_lo:k_hi], preferred_element_type=F32)
            acc = jnp.where(head_masks[0], ov[0:GL_SUB], 0.0)
            for h in range(1, GL_HEADS):
                acc = acc + jnp.where(head_masks[h], ov[h * GL_SUB:(h + 1) * GL_SUB], 0.0)
            pieces.append(acc)
        o_ref[0] = o_inter + jnp.concatenate(pieces, axis=0)

        kg = k * jnp.exp(g_end - g)
        upd = jnp.dot(v.T.astype(BF16), kg.astype(BF16), preferred_element_type=F32)
        s_scr[d] = jnp.exp(g_end) * state + jnp.where(bd_mask, upd, 0.0)

    @pl.when(i == pl.num_programs(1) - 1)
    def _():
        st_ref[0] = s_scr[...]


def _gla_call(gl, small, w2e, b2, s0):
    b_, t, _ = gl.shape
    chunk = min(GL_CHUNK, t)
    n = t // chunk
    fwd = lambda b, i: (b, i, 0)
    bwd = lambda b, i: (b, n - 1 - i, 0)
    state_spec = pl.BlockSpec((1, 2, GL_W, GL_W), lambda b, i: (b, 0, 0, 0))
    return pl.pallas_call(
        functools.partial(_gla_kernel, chunk=chunk),
        grid=(b_, n),
        in_specs=[pl.BlockSpec((1, chunk, 3 * GL_W), fwd), pl.BlockSpec((1, chunk, SMALL_W), fwd),
                  pl.BlockSpec((1, chunk, 3 * GL_W), bwd), pl.BlockSpec((1, chunk, SMALL_W), bwd),
                  _resident(w2e.shape), _resident(b2.shape), state_spec],
        out_specs=[pl.BlockSpec((1, chunk, GL_W), fwd), pl.BlockSpec((1, chunk, GL_W), bwd), state_spec],
        out_shape=[jax.ShapeDtypeStruct((b_, t, GL_W), F32), jax.ShapeDtypeStruct((b_, t, GL_W), F32),
                   jax.ShapeDtypeStruct(s0.shape, F32)],
        scratch_shapes=[pltpu.VMEM((2, GL_W, GL_W), F32)],
        compiler_params=_cparams("parallel", "arbitrary"),
        name="gla",
    )(gl, small, gl, small, w2e, b2, s0)


def _qkprep_kernel(gq_ref, cos_ref, sin_ref, gq_g, gk_g, q_out, kt_out, v_out):
    x = gq_ref[0]
    tm = x.shape[0]

    def norm_rope(z, g, reps):
        width = z.shape[1]
        msq = _group_mean(z * z, _group_ones(width))
        zn = z * lax.rsqrt(msq + LN_EPS) * jnp.tile(g, (1, reps))
        lane = lax.broadcasted_iota(jnp.int32, zn.shape, 1)
        partner = jnp.where((lane & 31) < ROPE_PAIRS,
                            pltpu.roll(zn, width - ROPE_PAIRS, axis=1),
                            pltpu.roll(zn, ROPE_PAIRS, axis=1))
        return zn * jnp.tile(cos_ref[...], (1, reps)) + partner * jnp.tile(sin_ref[...], (1, reps))

    q = norm_rope(x[:, 0:GQ_W], gq_g[...], GQ_W // 128)
    q_out[0] = (q * (HEAD_DIM ** -0.5)).astype(BF16)
    k = norm_rope(x[:, GQ_W:GQ_W + GQ_KW], gk_g[...], GQ_KW // 128)
    kt_out[0] = k.T.astype(BF16)
    v = x[:, GQ_W + GQ_KW:GQ_W + 2 * GQ_KW]
    ones = jnp.ones((tm, HEAD_DIM), F32)
    for j in range(GQ_KV):
        v_out[0, j] = jnp.concatenate([v[:, j * HEAD_DIM:(j + 1) * HEAD_DIM], ones], axis=1).astype(BF16)


def _qkprep_call(gq, cos, sin, gq_g, gk_g, tm):
    b_, t, w = gq.shape
    return pl.pallas_call(
        _qkprep_kernel,
        grid=(b_, t // tm),
        in_specs=[pl.BlockSpec((1, tm, w), lambda b, i: (b, i, 0)),
                  pl.BlockSpec((tm, 128), lambda b, i: (i, 0)),
                  pl.BlockSpec((tm, 128), lambda b, i: (i, 0)),
                  _resident(gq_g.shape), _resident(gk_g.shape)],
        out_specs=[pl.BlockSpec((1, tm, GQ_W), lambda b, i: (b, i, 0)),
                   pl.BlockSpec((1, GQ_KW, tm), lambda b, i: (b, 0, i)),
                   pl.BlockSpec((1, GQ_KV, tm, 128), lambda b, i: (b, 0, i, 0))],
        out_shape=[jax.ShapeDtypeStruct((b_, t, GQ_W), BF16),
                   jax.ShapeDtypeStruct((b_, GQ_KW, t), BF16),
                   jax.ShapeDtypeStruct((b_, GQ_KV, t, 128), BF16)],
        compiler_params=_cparams("parallel", "parallel"),
        name="qkprep",
    )(gq, cos, sin, gq_g, gk_g)


def _attn_kernel(*refs, n_lat_tiles, tk):
    if n_lat_tiles:
        q_ref, ktc_ref, vc_ref, ktl_ref, vl_ref, o_ref, qs_scr, m_scr, acc_scr = refs
    else:
        q_ref, ktc_ref, vc_ref, o_ref, qs_scr, m_scr, acc_scr = refs
    tq = q_ref.shape[1]
    q = q_ref[0]
    for g in range(GQ_G):
        qs_scr[g * tq:(g + 1) * tq, :] = q[:, g * HEAD_DIM:(g + 1) * HEAD_DIM]
    m_scr[...] = jnp.full(m_scr.shape, NEG_BIG, F32)
    acc_scr[...] = jnp.zeros(acc_scr.shape, F32)

    def update(kt_tile, v_tile):
        s = jnp.dot(qs_scr[...], kt_tile, preferred_element_type=F32)
        m_prev = m_scr[...]
        m_new = jnp.maximum(m_prev, jnp.max(s, axis=1, keepdims=True))
        p = jnp.exp(s - m_new).astype(BF16)
        acc_scr[...] = (jnp.exp(m_prev - m_new) * acc_scr[...]
                        + jnp.dot(p, v_tile, preferred_element_type=F32))
        m_scr[...] = m_new

    update(ktc_ref[0], vc_ref[0, 0])
    if n_lat_tiles:
        def body(j, carry):
            start = pl.multiple_of(j * tk, tk)
            update(ktl_ref[0, :, pl.ds(start, tk)], vl_ref[0, 0, pl.ds(start, tk), :])
            return carry
        lax.fori_loop(0, n_lat_tiles, body, 0)

    acc = acc_scr[...]
    o = acc[:, 0:HEAD_DIM] / acc[:, HEAD_DIM:HEAD_DIM + 1]
    o_ref[0] = jnp.concatenate([o[g * tq:(g + 1) * tq] for g in range(GQ_G)], axis=1)


def _attn_call(q, kt_c, v_c, kt_l=None, v_l=None):
    b_, t, _ = q.shape
    n_ctx = kt_c.shape[2]
    tq = min(ATT_TQ, t)
    gw = GQ_G * HEAD_DIM
    in_specs = [pl.BlockSpec((1, tq, gw), lambda b, kv, i: (b, i, kv)),
                pl.BlockSpec((1, HEAD_DIM, n_ctx), lambda b, kv, i: (b, kv, 0)),
                pl.BlockSpec((1, 1, n_ctx, 128), lambda b, kv, i: (b, kv, 0, 0))]
    args = [q, kt_c, v_c]
    n_lat_tiles, tk = 0, 0
    if kt_l is not None:
        n_lat = kt_l.shape[2]
        tk = min(ATT_TK, n_lat)
        n_lat_tiles = n_lat // tk
        in_specs += [pl.BlockSpec((1, HEAD_DIM, n_lat), lambda b, kv, i: (b, kv, 0)),
                     pl.BlockSpec((1, 1, n_lat, 128), lambda b, kv, i: (b, kv, 0, 0))]
        args += [kt_l, v_l]
    return pl.pallas_call(
        functools.partial(_attn_kernel, n_lat_tiles=n_lat_tiles, tk=tk),
        grid=(b_, GQ_KV, t // tq),
        in_specs=in_specs,
        out_specs=pl.BlockSpec((1, tq, gw), lambda b, kv, i: (b, i, kv)),
        out_shape=jax.ShapeDtypeStruct((b_, t, GQ_W), F32),
        scratch_shapes=[pltpu.VMEM((GQ_G * tq, HEAD_DIM), BF16),
                        pltpu.VMEM((GQ_G * tq, 1), F32),
                        pltpu.VMEM((GQ_G * tq, 128), F32)],
        compiler_params=_cparams("parallel", "parallel", "arbitrary"),
        name="attention",
    )(*args)


def _fourier_factors(t):
    bits = int(round(math.log2(t)))
    assert 1 << bits == t
    n1 = 1 << (bits // 2)
    return n1, t // n1


def _hi_lo(a):
    a = np.asarray(a, np.float64)
    hi = jnp.asarray(a, F32).astype(BF16)
    lo = (jnp.asarray(a, F32) - hi.astype(F32)).astype(BF16)
    return hi, lo


def _fourier_tables(t):
    n1, n2 = _fourier_factors(t)
    c = np.arange(FT_GC)
    ang = 2.0 * np.pi * np.outer(c, c) / FT_GC
    eye = np.eye(FT_GROUPS)
    w0 = np.concatenate([np.kron(eye, np.cos(ang)), -np.kron(eye, np.sin(ang))], axis=1)
    a1 = 2.0 * np.pi * np.outer(np.arange(n1), np.arange(n1)) / n1
    fr, fi = np.cos(a1), -np.sin(a1)
    m1 = np.block([[fr, -fi], [fi, fr]])
    tw = 2.0 * np.pi * np.outer(np.arange(n1), np.arange(n2)) / t
    a2 = 2.0 * np.pi * np.outer(np.arange(n2), np.arange(n2)) / n2
    m2 = np.concatenate([np.cos(a2), np.sin(a2)], axis=1) / math.sqrt(t * FT_GC)
    return dict(n1=n1, n2=n2, w0=_hi_lo(w0), m1=_hi_lo(m1), m2=_hi_lo(m2),
                twr=np.cos(tw).astype(np.float32), twi=(-np.sin(tw)).astype(np.float32))


def _ft_stage1_kernel(u_ref, w0h, w0l, m1h, m1l, twr_ref, twi_ref, yr_ref, yi_ref, *, n_tok):
    u = u_ref[0]
    n1 = u.shape[0]
    zr, zi = [], []
    for j in range(n_tok):
        uh, ul = _split2(u[:, j * FT_W:(j + 1) * FT_W])
        z = _dot3(uh, ul, w0h[...], w0l[...])
        zr.append(z[:, :FT_W])
        zi.append(z[:, FT_W:])
    z = jnp.concatenate([jnp.concatenate(zr, axis=1), jnp.concatenate(zi, axis=1)], axis=0)
    zh, zl = _split2(z)
    y = _dot3(m1h[...], m1l[...], zh, zl)
    y_r, y_i = y[:n1], y[n1:]
    twr, twi = twr_ref[0], twi_ref[0]
    for j in range(n_tok):
        sl = slice(j * FT_W, (j + 1) * FT_W)
        cr, ci = twr[:, j:j + 1], twi[:, j:j + 1]
        yr_ref[0, :, sl] = y_r[:, sl] * cr - y_i[:, sl] * ci
        yi_ref[0, :, sl] = y_r[:, sl] * ci + y_i[:, sl] * cr


def _ft_stage2_kernel(yr_ref, yi_ref, m2h, m2l, o_ref, *, n_k1, n2):
    for j in range(n_k1):
        y = jnp.concatenate([yr_ref[0, j * n2:(j + 1) * n2, :], yi_ref[0, j * n2:(j + 1) * n2, :]], axis=0)
        yh, yl = _split2(y)
        o_ref[0, :, j * FT_W:(j + 1) * FT_W] = _dot3(m2h[...], m2l[...], yh, yl)


def _fourier_call(u, tabs):
    b_, t, w = u.shape
    n1, n2 = tabs["n1"], tabs["n2"]
    n_tok = 8
    twr = jnp.asarray(tabs["twr"]).reshape(n1, n2 // n_tok, n_tok).transpose(1, 0, 2)
    twi = jnp.asarray(tabs["twi"]).reshape(n1, n2 // n_tok, n_tok).transpose(1, 0, 2)
    tc = n_tok * w
    row_view = jax.ShapeDtypeStruct((b_, n1, n2 * w), F32)
    yr, yi = pl.pallas_call(
        functools.partial(_ft_stage1_kernel, n_tok=n_tok),
        grid=(b_, n2 // n_tok),
        in_specs=[pl.BlockSpec((1, n1, tc), lambda b, j: (b, 0, j)),
                  _resident(tabs["w0"][0].shape), _resident(tabs["w0"][1].shape),
                  _resident(tabs["m1"][0].shape), _resident(tabs["m1"][1].shape),
                  pl.BlockSpec((1, n1, n_tok), lambda b, j: (j, 0, 0)),
                  pl.BlockSpec((1, n1, n_tok), lambda b, j: (j, 0, 0))],
        out_specs=[pl.BlockSpec((1, n1, tc), lambda b, j: (b, 0, j))] * 2,
        out_shape=[row_view, row_view],
        compiler_params=_cparams("parallel", "parallel"),
        name="fourier_stage1",
    )(u.reshape(b_, n1, n2 * w), *tabs["w0"], *tabs["m1"], twr, twi)
    n_k1 = 8
    out = pl.pallas_call(
        functools.partial(_ft_stage2_kernel, n_k1=n_k1, n2=n2),
        grid=(b_, n1 // n_k1),
        in_specs=[pl.BlockSpec((1, n_k1 * n2, w), lambda b, j: (b, j, 0)),
                  pl.BlockSpec((1, n_k1 * n2, w), lambda b, j: (b, j, 0)),
                  _resident(tabs["m2"][0].shape), _resident(tabs["m2"][1].shape)],
        out_specs=pl.BlockSpec((1, n2, n_k1 * w), lambda b, j: (b, 0, j)),
        out_shape=jax.ShapeDtypeStruct((b_, n2, n1 * w), F32),
        compiler_params=_cparams("parallel", "parallel"),
        name="fourier_stage2",
    )(yr.reshape(b_, t, w), yi.reshape(b_, t, w), *tabs["m2"])
    return out.reshape(b_, t, w)


def _merge_kernel(x_ref, mod_ref, hf_ref, hb_ref, mlo_ref, ft_ref, att_ref, of_ref, ob_ref, glr_ref,
                  wgate, bgate, wbr, wout, mlg, glg, lng, lnb, o_ref, *, alpha):
    x = x_ref[0]
    h = _modulate(x, mod_ref[0, 0:1, :], mod_ref[0, 1:2, :]).astype(BF16)
    ones = _group_ones(ML_W)
    hs = hf_ref[0] + hb_ref[0]
    hc = hs - _group_mean(hs, ones)
    hn = hc * lax.rsqrt(_group_mean(hc * hc, ones) + LN_EPS) * mlg[...]
    br_ml = hn * _sigmoid(mlo_ref[0])
    os_ = of_ref[0] + ob_ref[0]
    on = os_ * lax.rsqrt(_group_mean(os_ * os_, ones) + LN_EPS) * glg[...]
    br_gl = on * _silu(glr_ref[0])
    branches = (br_ml, ft_ref[0], att_ref[0], br_gl)
    mixed = None
    off = 0
    for j, br in enumerate(branches):
        wd = br.shape[1]
        gate = _sigmoid(jnp.dot(h, wgate[:, j * D_MODEL:(j + 1) * D_MODEL], preferred_element_type=F32)
                        + bgate[:, j * D_MODEL:(j + 1) * D_MODEL])
        u = gate * jnp.dot(br.astype(BF16), wbr[off:off + wd, :], preferred_element_type=F32)
        mixed = u if mixed is None else mixed + u
        off += wd
    y = jnp.dot(mixed.astype(BF16), wout[...], preferred_element_type=F32)
    o_ref[0] = _layernorm(alpha * x + mod_ref[0, 2:3, :] * y) * lng[...] + lnb[...]


def _merge_call(x, mod, hf, hb, ml, ftb, att, of, ob, gl, weights, alpha, tm):
    b_, t, d = x.shape
    tok = lambda w: pl.BlockSpec((1, tm, w), lambda b, i: (b, i, 0))
    last_quarter = pl.BlockSpec((1, tm, ML_W), lambda b, i: (b, i, 3))
    return pl.pallas_call(
        functools.partial(_merge_kernel, alpha=alpha),
        grid=(b_, t // tm),
        in_specs=[tok(d), pl.BlockSpec((1, N_ADA, d), lambda b, i: (b, 0, 0)),
                  tok(ML_W), tok(ML_W), last_quarter, tok(FT_W), tok(GQ_W), tok(GL_W), tok(GL_W),
                  last_quarter] + [_resident(w.shape) for w in weights],
        out_specs=tok(d),
        out_shape=jax.ShapeDtypeStruct((b_, t, d), F32),
        compiler_params=_cparams("parallel", "parallel"),
        name="merge",
    )(x, mod, hf, hb, ml, ftb, att, of, ob, gl, *weights)


def _ffn_kernel(x_ref, mod_ref, wg, wu, wd, lng, lnb, o_ref, *, alpha):
    x = x_ref[0]
    h = _modulate(x, mod_ref[0, 3:4, :], mod_ref[0, 4:5, :]).astype(BF16)
    a = jnp.dot(h, wg[...], preferred_element_type=F32)
    u = jnp.dot(h, wu[...], preferred_element_type=F32)
    f = jnp.dot((_silu(a) * u).astype(BF16), wd[...], preferred_element_type=F32)
    o_ref[0] = _layernorm(alpha * x + mod_ref[0, 5:6, :] * f) * lng[...] + lnb[...]


def _ffn_call(x, mod, weights, alpha, tm):
    b_, t, d = x.shape
    return pl.pallas_call(
        functools.partial(_ffn_kernel, alpha=alpha),
        grid=(b_, t // tm),
        in_specs=[pl.BlockSpec((1, tm, d), lambda b, i: (b, i, 0)),
                  pl.BlockSpec((1, N_ADA, d), lambda b, i: (b, 0, 0))]
                 + [_resident(w.shape) for w in weights],
        out_specs=pl.BlockSpec((1, tm, d), lambda b, i: (b, i, 0)),
        out_shape=jax.ShapeDtypeStruct((b_, t, d), F32),
        compiler_params=_cparams("parallel", "parallel"),
        name="ffn",
    )(x, mod, *weights)


def _moe_kernel(x_ref, mod_ref, wrh, wrl, br_ref, wg, wu, wd, lng, lnb, o_ref, h_scr, dense_scr, acc_scr,
                *, alpha):
    e = pl.program_id(2)

    @pl.when(e == 0)
    def _():
        h = _modulate(x_ref[0], mod_ref[0, 3:4, :], mod_ref[0, 4:5, :])
        h_scr[...] = h.astype(BF16)
        hh, hl = _split2(h)
        logits = _dot3(hh, hl, wrh[...], wrl[...])
        lane = lax.broadcasted_iota(jnp.int32, logits.shape, 1).astype(F32)
        valid = lane < N_EXPERTS
        sel = jnp.where(valid, logits + br_ref[...], -jnp.inf)

        def pick(scores):
            mx = jnp.max(scores, axis=1, keepdims=True)
            idx = jnp.min(jnp.where(scores == mx, lane, 2.0 * SMALL_W), axis=1, keepdims=True)
            return lane == idx

        first = pick(sel)
        second = pick(jnp.where(first, -jnp.inf, sel))
        l1 = jnp.sum(jnp.where(first, logits, 0.0), axis=1, keepdims=True)
        l2 = jnp.sum(jnp.where(second, logits, 0.0), axis=1, keepdims=True)
        mx = jnp.maximum(l1, l2)
        e1, e2 = jnp.exp(l1 - mx), jnp.exp(l2 - mx)
        inv = 1.0 / (e1 + e2)
        dense_scr[...] = jnp.where(first, e1 * inv, 0.0) + jnp.where(second, e2 * inv, 0.0)
        acc_scr[...] = jnp.zeros(acc_scr.shape, F32)

    h = h_scr[...]
    a = jnp.dot(h, wg[0], preferred_element_type=F32)
    u = jnp.dot(h, wu[0], preferred_element_type=F32)
    f = jnp.dot((_silu(a) * u).astype(BF16), wd[0], preferred_element_type=F32)
    dense = dense_scr[...]
    lane = lax.broadcasted_iota(jnp.int32, dense.shape, 1)
    w_e = jnp.sum(jnp.where(lane == e, dense, 0.0), axis=1, keepdims=True)
    acc_scr[...] += w_e * f

    @pl.when(e == pl.num_programs(2) - 1)
    def _():
        o_ref[0] = (_layernorm(alpha * x_ref[0] + mod_ref[0, 5:6, :] * acc_scr[...]) * lng[...]
                    + lnb[...])


def _moe_call(x, mod, wrh, wrl, br, wg, wu, wd, lng, lnb, alpha, tm):
    b_, t, d = x.shape
    n_e, _, ff = wg.shape
    return pl.pallas_call(
        functools.partial(_moe_kernel, alpha=alpha),
        grid=(b_, t // tm, n_e),
        in_specs=[pl.BlockSpec((1, tm, d), lambda b, i, e: (b, i, 0)),
                  pl.BlockSpec((1, N_ADA, d), lambda b, i, e: (b, 0, 0)),
                  _resident(wrh.shape), _resident(wrl.shape), _resident(br.shape),
                  pl.BlockSpec((1, d, ff), lambda b, i, e: (e, 0, 0)),
                  pl.BlockSpec((1, d, ff), lambda b, i, e: (e, 0, 0)),
                  pl.BlockSpec((1, ff, d), lambda b, i, e: (e, 0, 0)),
                  _resident(lng.shape), _resident(lnb.shape)],
        out_specs=pl.BlockSpec((1, tm, d), lambda b, i, e: (b, i, 0)),
        out_shape=jax.ShapeDtypeStruct((b_, t, d), F32),
        scratch_shapes=[pltpu.VMEM((tm, d), BF16), pltpu.VMEM((tm, SMALL_W), F32), pltpu.VMEM((tm, d), F32)],
        compiler_params=_cparams("parallel", "parallel", "arbitrary"),
        name="moe",
    )(x, mod, wrh, wrl, br, wg, wu, wd, lng, lnb)


def _rope_tables(t):
    rows = t // GRID_W
    row = jnp.repeat(jnp.arange(rows, dtype=F32), GRID_W)
    col = jnp.tile(jnp.arange(GRID_W, dtype=F32), rows)
    inv = jnp.power(ROPE_BASE, -jnp.arange(ROPE_PAIRS, dtype=F32) / ROPE_PAIRS)
    ar, ac = row[:, None] * inv, col[:, None] * inv
    cos = jnp.concatenate([jnp.cos(ar), jnp.cos(ar), jnp.cos(ac), jnp.cos(ac)], axis=1)
    sin = jnp.concatenate([-jnp.sin(ar), jnp.sin(ar), -jnp.sin(ac), jnp.sin(ac)], axis=1)
    return jnp.tile(cos, (1, 2)), jnp.tile(sin, (1, 2))


def _token_tile(t, pref):
    return pref if t % pref == 0 else t


def kernel(x, c, ctx, c_ctx, w_ada, b_ada, w_in, ml_gate_b, ml_norm_g, gq_qnorm_g, gq_knorm_g, gl_w2, gl_b2,
           gl_norm_g, w_branch, w_gate, b_gate, w_out, ln1_g, ln1_b, ln2_g, ln2_b, ffd_wg, ffd_wu, ffd_wd,
           moe_wr, moe_br, moe_wg, moe_wu, moe_wd):
    b_, s_len, d = x.shape
    n_ctx = ctx.shape[1]
    depth = w_in.shape[0]
    alpha = (2.0 * depth) ** 0.25
    bf = lambda a: a.astype(BF16)
    row = lambda a: a.reshape(1, -1).astype(F32)

    cos_l, sin_l = _rope_tables(s_len)
    cos_c, sin_c = jnp.ones((n_ctx, 128), F32), jnp.zeros((n_ctx, 128), F32)
    ft_l, ft_c = _fourier_tables(s_len), _fourier_tables(n_ctx)
    tm_l, tm_c = _token_tile(s_len, 512), _token_tile(n_ctx, 512)

    cc = jnp.zeros((8, d), F32).at[0:b_].set(c).at[b_].set(c_ctx)
    xc = ctx
    for l in range(depth):
        last = l == depth - 1
        ada = _ada_call(cc, bf(w_ada[l]), row(b_ada[l])).reshape(8, N_ADA, d)
        mod_l = ada[0:b_]
        mod_c = jnp.broadcast_to(ada[b_][None], (b_, N_ADA, d))

        w = w_in[l]
        o_gate, o_ft, o_gq, o_gl, o_af = 4 * ML_W, 4 * ML_W + 16, 4 * ML_W + 16 + FT_W, \
            4 * ML_W + 16 + FT_W + GQ_W + 2 * GQ_KW, 4 * ML_W + 16 + FT_W + GQ_W + 2 * GQ_KW + 4 * GL_W
        ml_scale = jnp.concatenate([jnp.ones((ML_W,)), jnp.full((ML_W,), HEAD_DIM ** -0.5), jnp.ones((2 * ML_W,))])
        gl_scale = jnp.concatenate([jnp.full((GL_W,), HEAD_DIM ** -0.5), jnp.ones((3 * GL_W,))])
        w_small = jnp.concatenate([w[:, o_gate:o_gate + 16], w[:, o_af:o_af + 2 * GL_RANK],
                                   jnp.zeros((d, SMALL_W - 16 - 2 * GL_RANK), F32)], axis=1)
        ws = [bf(w[:, 0:4 * ML_W] * ml_scale), bf(w[:, o_ft:o_ft + FT_W]), bf(w[:, o_gq:o_gl]),
              bf(w[:, o_gl:o_af] * gl_scale), bf(w_small)]
        gate_bias = jnp.zeros((1, SMALL_W), F32).at[0, 0:16].set(ml_gate_b[l].reshape(-1))
        w2e = jnp.zeros((SMALL_W, 2 * GL_W), F32)
        w2e = w2e.at[16:16 + GL_RANK, 0:GL_W].set(gl_w2[l, 0]).at[16 + GL_RANK:16 + 2 * GL_RANK, GL_W:].set(gl_w2[l, 1])
        merge_w = [bf(w_gate[l]), row(b_gate[l]), bf(w_branch[l]), bf(w_out[l]), row(ml_norm_g[l]),
                   row(gl_norm_g[l]), row(ln1_g[l]), row(ln1_b[l])]
        gq_g = jnp.tile(row(gq_qnorm_g[l]), (1, 2))
        gk_g = jnp.tile(row(gq_knorm_g[l]), (1, 2))

        def mixers(xs, mod, tm, cos, sin, states):
            ml, ftu, gq, gl, small = _inproj_call(xs, mod, ws, tm)
            hf, hb, ml_s, ml_m = _mlstm_call(ml, small, gate_bias, states[0], states[1])
            of, ob, gl_s = _gla_call(gl, small, bf(w2e), gl_b2[l].astype(F32), states[2])
            q, kt, v = _qkprep_call(gq, cos, sin, gq_g, gk_g, tm)
            return dict(ml=ml, ftu=ftu, gl=gl, hf=hf, hb=hb, of=of, ob=ob, q=q, kt=kt, v=v), (ml_s, ml_m, gl_s)

        zero_states = (jnp.zeros((b_, 2 * ML_HEADS, HEAD_DIM, 2 * HEAD_DIM), F32),
                       jnp.zeros((b_, 2 * ML_HEADS, 1, SMALL_W), F32),
                       jnp.zeros((b_, 2, GL_W, GL_W), F32))
        pc, ctx_states = mixers(xc, mod_c, tm_c, cos_c, sin_c, zero_states)
        pl_, _ = mixers(x, mod_l, tm_l, cos_l, sin_l, ctx_states)

        att_l = _attn_call(pl_["q"], pc["kt"], pc["v"], pl_["kt"], pl_["v"])
        x_mid = _merge_call(x, mod_l, pl_["hf"], pl_["hb"], pl_["ml"], _fourier_call(pl_["ftu"], ft_l), att_l,
                            pl_["of"], pl_["ob"], pl_["gl"], merge_w, alpha, tm_l)
        if not last:
            att_c = _attn_call(pc["q"], pc["kt"], pc["v"])
            xc = _merge_call(xc, mod_c, pc["hf"], pc["hb"], pc["ml"], _fourier_call(pc["ftu"], ft_c), att_c,
                             pc["of"], pc["ob"], pc["gl"], merge_w, alpha, tm_c)

        j = l // 2
        if l % 2 == 0:
            ffn_w = [bf(ffd_wg[j]), bf(ffd_wu[j]), bf(ffd_wd[j]), row(ln2_g[l]), row(ln2_b[l])]
            x = _ffn_call(x_mid, mod_l, ffn_w, alpha, tm_l)
            if not last:
                xc = _ffn_call(xc, mod_c, ffn_w, alpha, tm_c)
        else:
            wr = jnp.zeros((d, SMALL_W), F32).at[:, 0:N_EXPERTS].set(moe_wr[j])
            wrh = bf(wr)
            wrl = bf(wr - wrh.astype(F32))
            brp = jnp.zeros((1, SMALL_W), F32).at[0, 0:N_EXPERTS].set(moe_br[j])
            moe_args = (wrh, wrl, brp, bf(moe_wg[j]), bf(moe_wu[j]), bf(moe_wd[j]), row(ln2_g[l]), row(ln2_b[l]))
            x = _moe_call(x_mid, mod_l, *moe_args, alpha, _token_tile(s_len, 1024))
            if not last:
                xc = _moe_call(xc, mod_c, *moe_args, alpha, tm_c)
    return x
```

```python
import functools
import math

import jax
import jax.numpy as jnp
import numpy as np
from jax import lax
from jax.experimental import pallas as pl
from jax.experimental.pallas import tpu as pltpu

F32 = jnp.float32
BF16 = jnp.bfloat16

D_MODEL = 1024
GRID_W = 64
HEAD_DIM = 64
ML_HEADS = 4
ML_W = ML_HEADS * HEAD_DIM
FT_GROUPS = 4
FT_GC = 64
FT_W = FT_GROUPS * FT_GC
GQ_KV = 2
GQ_G = 4
GQ_W = GQ_KV * GQ_G * HEAD_DIM
GQ_KW = GQ_KV * HEAD_DIM
ROPE_PAIRS = HEAD_DIM // 4
ROPE_BASE = 10000.0
GL_HEADS = 4
GL_W = GL_HEADS * HEAD_DIM
GL_RANK = 16
GL_TAU = 16.0
N_EXPERTS = 8
N_ADA = 6
LN_EPS = 1e-6
SMALL_W = 128

ML_CHUNK = 256
GL_CHUNK = 128
GL_SUB = 16
GL_EXP_CLAMP = 80.0
ATT_TQ = 256
ATT_TK = 1280
ATT_ROWS = 256
LOG2_E = 1.4426950408889634
MAX_LOGIT_BOUND = 60.0
NEG_BIG = -1e30

VMEM_LIMIT = 56 * 1024 * 1024


def _cparams(*sem):
    return pltpu.CompilerParams(dimension_semantics=sem, vmem_limit_bytes=VMEM_LIMIT)


def _resident(shape):
    nd = len(shape)
    return pl.BlockSpec(shape, lambda *_: (0,) * nd, pipeline_mode=pl.Buffered(1))


def _bdot(a, b):
    return jnp.dot(a.astype(BF16), b.astype(BF16), preferred_element_type=F32)


def _split2(x):
    hi = x.astype(BF16)
    lo = (x - hi.astype(F32)).astype(BF16)
    return hi, lo


def _split3(x):
    a = x.astype(BF16)
    r = x - a.astype(F32)
    b = r.astype(BF16)
    c = (r - b.astype(F32)).astype(BF16)
    return a, b, c


def _dot_exact_rhs(x, m_bf16):
    a, b, c = _split3(x)
    d = functools.partial(jnp.dot, preferred_element_type=F32)
    return d(a, m_bf16) + d(b, m_bf16) + d(c, m_bf16)


def _dot_exact_lhs(m_bf16, x):
    a, b, c = _split3(x)
    d = functools.partial(jnp.dot, preferred_element_type=F32)
    return d(m_bf16, a) + d(m_bf16, b) + d(m_bf16, c)


def _dot3(a_hi, a_lo, b_hi, b_lo):
    d = functools.partial(jnp.dot, preferred_element_type=F32)
    return d(a_hi, b_hi) + d(a_hi, b_lo) + d(a_lo, b_hi)


def _sigmoid(x):
    return 1.0 / (1.0 + jnp.exp(-x))


def _silu(x):
    return x * _sigmoid(x)


def _log_sigmoid(x):
    return jnp.minimum(x, 0.0) - jnp.log(1.0 + jnp.exp(-jnp.abs(x)))


def _layernorm(x):
    mu = jnp.mean(x, axis=-1, keepdims=True)
    xc = x - mu
    var = jnp.mean(xc * xc, axis=-1, keepdims=True)
    return xc * lax.rsqrt(var + LN_EPS)


def _modulate(x, shift, scale):
    return _layernorm(x) * (1.0 + scale) + shift


def _group_ones(width):
    r = lax.broadcasted_iota(jnp.int32, (width, width), 0) >> 6
    c = lax.broadcasted_iota(jnp.int32, (width, width), 1) >> 6
    return jnp.where(r == c, 1.0, 0.0).astype(BF16)


def _group_mean(x, ones):
    return _dot_exact_rhs(x, ones) * (1.0 / HEAD_DIM)


def _tri(n, upper):
    r = lax.broadcasted_iota(jnp.int32, (n, n), 0)
    c = lax.broadcasted_iota(jnp.int32, (n, n), 1)
    keep = (c >= r) if upper else (c <= r)
    return jnp.where(keep, 1.0, 0.0).astype(BF16)


def _ada_kernel(c_ref, w_ref, b_ref, o_ref):
    o_ref[...] = _bdot(_silu(c_ref[...]), w_ref[...]) + b_ref[...]


def _ada_call(cc, w, b):
    rows, d = cc.shape
    n = w.shape[1]
    tn = 1024
    return pl.pallas_call(
        _ada_kernel,
        grid=(n // tn,),
        in_specs=[pl.BlockSpec((rows, d), lambda j: (0, 0)),
                  pl.BlockSpec((d, tn), lambda j: (0, j)),
                  pl.BlockSpec((1, tn), lambda j: (0, j))],
        out_specs=pl.BlockSpec((rows, tn), lambda j: (0, j)),
        out_shape=jax.ShapeDtypeStruct((rows, n), F32),
        compiler_params=_cparams("parallel"),
        name="ada",
    )(cc, w, b)


def _inproj_kernel(x_ref, mod_ref, w_ml, w_ft, w_gq, w_gl, w_sm, o_ml, o_ft, o_gq, o_gl, o_sm):
    h = _modulate(x_ref[0], mod_ref[0, 0:1, :], mod_ref[0, 1:2, :]).astype(BF16)
    for w, o in ((w_ml, o_ml), (w_ft, o_ft), (w_gq, o_gq), (w_gl, o_gl), (w_sm, o_sm)):
        o[0] = jnp.dot(h, w[...], preferred_element_type=F32)


def _inproj_call(x, mod, ws, tm):
    b_, t, d = x.shape
    widths = [w.shape[1] for w in ws]
    return pl.pallas_call(
        _inproj_kernel,
        grid=(b_, t // tm),
        in_specs=[pl.BlockSpec((1, tm, d), lambda b, i: (b, i, 0)),
                  pl.BlockSpec((1, N_ADA, d), lambda b, i: (b, 0, 0))]
                 + [_resident(w.shape) for w in ws],
        out_specs=[pl.BlockSpec((1, tm, n), lambda b, i: (b, i, 0)) for n in widths],
        out_shape=[jax.ShapeDtypeStruct((b_, t, n), F32) for n in widths],
        compiler_params=_cparams("parallel", "parallel"),
        name="inproj",
    )(x, mod, *ws)


def _mlstm_kernel(qkv_f, sm_f, qkv_b, sm_b, bias_ref, s0_ref, m0_ref,
                  hf_ref, hb_ref, st_ref, mt_ref, s_scr, m_scr, *, chunk):
    i = pl.program_id(1)
    n_l = chunk

    @pl.when(i == 0)
    def _():
        s_scr[...] = s0_ref[0]
        m_scr[...] = m0_ref[0]

    row = lax.broadcasted_iota(jnp.int32, (n_l, n_l), 0)
    col = lax.broadcasted_iota(jnp.int32, (n_l, n_l), 1)
    ones_v = jnp.ones((n_l, HEAD_DIM), F32)
    for d, (qkv_ref, sm_ref, h_ref) in enumerate(((qkv_f, sm_f, hf_ref), (qkv_b, sm_b, hb_ref))):
        rev = d == 1
        blk = qkv_ref[0]
        pre = sm_ref[0] + bias_ref[...]
        bcum = _dot_exact_lhs(_tri(n_l, upper=rev), _log_sigmoid(pre))
        pre_t = pre.T
        b_t = bcum.T
        k_t = blk[:, ML_W:2 * ML_W].T
        mask = (col >= row) if rev else (col <= row)
        last = 0 if rev else n_l - 1
        outs = []
        for h in range(ML_HEADS):
            ci = 8 * d + h
            cf = 8 * d + 4 + h
            idx = 4 * d + h
            q = blk[:, h * HEAD_DIM:(h + 1) * HEAD_DIM].astype(BF16)
            kt = k_t[h * HEAD_DIM:(h + 1) * HEAD_DIM, :]
            v = blk[:, 2 * ML_W + h * HEAD_DIM:2 * ML_W + (h + 1) * HEAD_DIM]
            v_ext = jnp.concatenate([v, ones_v], axis=1).astype(BF16)
            bc = bcum[:, cf:cf + 1]
            br = b_t[cf:cf + 1, :]
            ir = pre_t[ci:ci + 1, :]
            state = s_scr[idx]
            m_prev = m_scr[idx][:, 0:1]

            dmat = jnp.where(mask, bc - br + ir, -jnp.inf)
            inter = bc + m_prev
            m_t = jnp.maximum(inter, jnp.max(dmat, axis=1, keepdims=True))
            w_intra = jnp.exp(dmat - m_t) * jnp.dot(q, kt.astype(BF16), preferred_element_type=F32)
            w_inter = jnp.exp(inter - m_t)
            qs = jnp.dot(q, state.astype(BF16), preferred_element_type=F32)
            num = (jnp.dot(w_intra.astype(BF16), v_ext, preferred_element_type=F32)[:, :HEAD_DIM]
                   + w_inter * qs[:, :HEAD_DIM])
            den = jnp.sum(w_intra, axis=1, keepdims=True) + w_inter * qs[:, HEAD_DIM:HEAD_DIM + 1]
            outs.append(num / jnp.maximum(jnp.abs(den), jnp.exp(-m_t)))

            b_last = bc[last:last + 1, :]
            g_row = b_last - br + ir
            m_new = jnp.maximum(b_last + m_prev, jnp.max(g_row, axis=1, keepdims=True))
            ws = jnp.exp(g_row - m_new)
            wc = jnp.exp(b_last + m_prev - m_new)
            s_scr[idx] = wc * state + jnp.dot((kt * ws).astype(BF16), v_ext, preferred_element_type=F32)
            m_scr[idx] = jnp.broadcast_to(m_new, (1, SMALL_W))
        h_ref[0] = jnp.concatenate(outs, axis=1)

    @pl.when(i == pl.num_programs(1) - 1)
    def _():
        st_ref[0] = s_scr[...]
        mt_ref[0] = m_scr[...]


def _mlstm_call(ml, small, bias, s0, m0):
    b_, t, _ = ml.shape
    chunk = min(ML_CHUNK, t)
    n = t // chunk
    fwd = lambda b, i: (b, i, 0)
    bwd = lambda b, i: (b, n - 1 - i, 0)
    state_spec = pl.BlockSpec((1, 2 * ML_HEADS, HEAD_DIM, 2 * HEAD_DIM), lambda b, i: (b, 0, 0, 0))
    m_spec = pl.BlockSpec((1, 2 * ML_HEADS, 1, SMALL_W), lambda b, i: (b, 0, 0, 0))
    return pl.pallas_call(
        functools.partial(_mlstm_kernel, chunk=chunk),
        grid=(b_, n),
        in_specs=[pl.BlockSpec((1, chunk, 3 * ML_W), fwd), pl.BlockSpec((1, chunk, SMALL_W), fwd),
                  pl.BlockSpec((1, chunk, 3 * ML_W), bwd), pl.BlockSpec((1, chunk, SMALL_W), bwd),
                  pl.BlockSpec((1, SMALL_W), lambda b, i: (0, 0)), state_spec, m_spec],
        out_specs=[pl.BlockSpec((1, chunk, ML_W), fwd), pl.BlockSpec((1, chunk, ML_W), bwd),
                   state_spec, m_spec],
        out_shape=[jax.ShapeDtypeStruct((b_, t, ML_W), F32), jax.ShapeDtypeStruct((b_, t, ML_W), F32),
                   jax.ShapeDtypeStruct(s0.shape, F32), jax.ShapeDtypeStruct(m0.shape, F32)],
        scratch_shapes=[pltpu.VMEM((2 * ML_HEADS, HEAD_DIM, 2 * HEAD_DIM), F32),
                        pltpu.VMEM((2 * ML_HEADS, 1, SMALL_W), F32)],
        compiler_params=_cparams("parallel", "arbitrary"),
        name="mlstm",
    )(ml, small, ml, small, bias, s0, m0)


def _gla_kernel(qkv_f, sm_f, qkv_b, sm_b, w2_ref, b2_ref, s0_ref,
                of_ref, ob_ref, st_ref, s_scr, *, chunk):
    i = pl.program_id(1)
    n_l = chunk
    n_sub = n_l // GL_SUB
    width = GL_W

    @pl.when(i == 0)
    def _():
        s_scr[...] = s0_ref[0]

    lane = lax.broadcasted_iota(jnp.int32, (GL_SUB, width), 1) >> 6
    head_masks = [lane == h for h in range(GL_HEADS)]
    bd_mask = ((lax.broadcasted_iota(jnp.int32, (width, width), 0) >> 6)
               == (lax.broadcasted_iota(jnp.int32, (width, width), 1) >> 6))
    for d, (qkv_ref, sm_ref, o_ref) in enumerate(((qkv_f, sm_f, of_ref), (qkv_b, sm_b, ob_ref))):
        rev = d == 1
        blk = qkv_ref[0]
        q = blk[:, 0:width]
        k = blk[:, width:2 * width]
        v = blk[:, 2 * width:3 * width]
        a = _bdot(sm_ref[0], w2_ref[...])[:, d * width:(d + 1) * width] + b2_ref[d:d + 1, :]
        la = _log_sigmoid(a) * (1.0 / GL_TAU)
        g = _dot_exact_lhs(_tri(n_l, upper=rev), la)
        last = 0 if rev else n_l - 1
        g_end = g[last:last + 1, :]
        state = s_scr[d]
        o_inter = lax.dot_general((q * jnp.exp(g)).astype(BF16), state.astype(BF16),
                                  (((1,), (1,)), ((), ())), preferred_element_type=F32)
        v_bf = v.astype(BF16)
        pieces = []
        for s in range(n_sub):
            lo = s * GL_SUB
            hi = lo + GL_SUB
            if rev:
                k_lo, k_hi, ref_row = lo, n_l, hi - 1
            else:
                k_lo, k_hi, ref_row = 0, hi, lo
            r = g[ref_row:ref_row + 1, :]
            qt = q[lo:hi] * jnp.exp(g[lo:hi] - r)
            kt = k[k_lo:k_hi] * jnp.exp(jnp.minimum(r - g[k_lo:k_hi], GL_EXP_CLAMP))
            qstack = jnp.concatenate([jnp.where(hm, qt, 0.0) for hm in head_masks], axis=0)
            amat = lax.dot_general(qstack.astype(BF16), kt.astype(BF16),
                                   (((1,), (1,)), ((), ())), preferred_element_type=F32)
            n_k = k_hi - k_lo
            t_idx = lo + (lax.broadcasted_iota(jnp.int32, (GL_HEADS * GL_SUB, n_k), 0) & (GL_SUB - 1))
            s_idx = k_lo + lax.broadcasted_iota(jnp.int32, (GL_HEADS * GL_SUB, n_k), 1)
            keep = (s_idx >= t_idx) if rev else (s_idx <= t_idx)
            amat = jnp.where(keep, amat, 0.0)
            ov = jnp.dot(amat.astype(BF16), v_bf[k_lo:k_hi], preferred_element_type=F32)
            acc = jnp.where(head_masks[0], ov[0:GL_SUB], 0.0)
            for h in range(1, GL_HEADS):
                acc = acc + jnp.where(head_masks[h], ov[h * GL_SUB:(h + 1) * GL_SUB], 0.0)
            pieces.append(acc)
        o_ref[0] = o_inter + jnp.concatenate(pieces, axis=0)

        kg = k * jnp.exp(g_end - g)
        upd = jnp.dot(v.T.astype(BF16), kg.astype(BF16), preferred_element_type=F32)
        s_scr[d] = jnp.exp(g_end) * state + jnp.where(bd_mask, upd, 0.0)

    @pl.when(i == pl.num_programs(1) - 1)
    def _():
        st_ref[0] = s_scr[...]


def _gla_call(gl, small, w2e, b2, s0):
    b_, t, _ = gl.shape
    chunk = min(GL_CHUNK, t)
    n = t // chunk
    fwd = lambda b, i: (b, i, 0)
    bwd = lambda b, i: (b, n - 1 - i, 0)
    state_spec = pl.BlockSpec((1, 2, GL_W, GL_W), lambda b, i: (b, 0, 0, 0))
    return pl.pallas_call(
        functools.partial(_gla_kernel, chunk=chunk),
        grid=(b_, n),
        in_specs=[pl.BlockSpec((1, chunk, 3 * GL_W), fwd), pl.BlockSpec((1, chunk, SMALL_W), fwd),
                  pl.BlockSpec((1, chunk, 3 * GL_W), bwd), pl.BlockSpec((1, chunk, SMALL_W), bwd),
                  _resident(w2e.shape), _resident(b2.shape), state_spec],
        out_specs=[pl.BlockSpec((1, chunk, GL_W), fwd), pl.BlockSpec((1, chunk, GL_W), bwd), state_spec],
        out_shape=[jax.ShapeDtypeStruct((b_, t, GL_W), F32), jax.ShapeDtypeStruct((b_, t, GL_W), F32),
                   jax.ShapeDtypeStruct(s0.shape, F32)],
        scratch_shapes=[pltpu.VMEM((2, GL_W, GL_W), F32)],
        compiler_params=_cparams("parallel", "arbitrary"),
        name="gla",
    )(gl, small, gl, small, w2e, b2, s0)


def _qkprep_kernel(gq_ref, cos_ref, sin_ref, gq_g, gk_g, sb_ref, q_out, kt_out, v_out):
    x = gq_ref[0]
    tm = x.shape[0]

    def norm_rope(z, g, reps):
        width = z.shape[1]
        msq = _group_mean(z * z, _group_ones(width))
        zn = z * lax.rsqrt(msq + LN_EPS) * jnp.tile(g, (1, reps))
        lane = lax.broadcasted_iota(jnp.int32, zn.shape, 1)
        partner = jnp.where((lane & 31) < ROPE_PAIRS,
                            pltpu.roll(zn, width - ROPE_PAIRS, axis=1),
                            pltpu.roll(zn, ROPE_PAIRS, axis=1))
        return zn * jnp.tile(cos_ref[...], (1, reps)) + partner * jnp.tile(sin_ref[...], (1, reps))

    q = norm_rope(x[:, 0:GQ_W], gq_g[...], GQ_W // 128)
    q_out[0] = (q * (LOG2_E * HEAD_DIM ** -0.5)).astype(BF16)
    k = norm_rope(x[:, GQ_W:GQ_W + GQ_KW], gk_g[...], GQ_KW // 128)
    bias_cols = jnp.broadcast_to(sb_ref[...], (tm, HEAD_DIM))
    k_ext = jnp.concatenate([k[:, 0:HEAD_DIM], bias_cols, k[:, HEAD_DIM:2 * HEAD_DIM], bias_cols], axis=1)
    kt_out[0] = k_ext.T.astype(BF16)
    v = x[:, GQ_W + GQ_KW:GQ_W + 2 * GQ_KW]
    ones = jnp.ones((tm, HEAD_DIM), F32)
    for j in range(GQ_KV):
        v_out[0, j] = jnp.concatenate([v[:, j * HEAD_DIM:(j + 1) * HEAD_DIM], ones], axis=1).astype(BF16)


def _qkprep_call(gq, cos, sin, gq_g, gk_g, score_bias, tm):
    b_, t, w = gq.shape
    return pl.pallas_call(
        _qkprep_kernel,
        grid=(b_, t // tm),
        in_specs=[pl.BlockSpec((1, tm, w), lambda b, i: (b, i, 0)),
                  pl.BlockSpec((tm, 128), lambda b, i: (i, 0)),
                  pl.BlockSpec((tm, 128), lambda b, i: (i, 0)),
                  _resident(gq_g.shape), _resident(gk_g.shape), _resident(score_bias.shape)],
        out_specs=[pl.BlockSpec((1, tm, GQ_W), lambda b, i: (b, i, 0)),
                   pl.BlockSpec((1, GQ_KV * 128, tm), lambda b, i: (b, 0, i)),
                   pl.BlockSpec((1, GQ_KV, tm, 128), lambda b, i: (b, 0, i, 0))],
        out_shape=[jax.ShapeDtypeStruct((b_, t, GQ_W), BF16),
                   jax.ShapeDtypeStruct((b_, GQ_KV * 128, t), BF16),
                   jax.ShapeDtypeStruct((b_, GQ_KV, t, 128), BF16)],
        compiler_params=_cparams("parallel", "parallel"),
        name="qkprep",
    )(gq, cos, sin, gq_g, gk_g, score_bias)


def _attn_kernel(q_ref, kt_ref, v_ref, o_ref, qs_scr, m_scr, acc_scr, *, n_tiles, tk, bounded):
    tq = q_ref.shape[1]
    rows_all = GQ_G * tq
    q = q_ref[0]
    lane = lax.broadcasted_iota(jnp.int32, (tq, HEAD_DIM), 1)
    ones_lane = jnp.where(lane == 0, 1.0, 0.0).astype(BF16)
    for g in range(GQ_G):
        qs_scr[g * tq:(g + 1) * tq, :] = jnp.concatenate(
            [q[:, g * HEAD_DIM:(g + 1) * HEAD_DIM], ones_lane], axis=1)
    acc_scr[...] = jnp.zeros(acc_scr.shape, F32)
    if not bounded:
        m_scr[...] = jnp.full(m_scr.shape, NEG_BIG, F32)

    def body(j, carry):
        start = pl.multiple_of(j * tk, tk)
        kt_tile = kt_ref[0, :, pl.ds(start, tk)]
        v_tile = v_ref[0, 0, pl.ds(start, tk), :]
        for r in range(rows_all // ATT_ROWS):
            rows = slice(r * ATT_ROWS, (r + 1) * ATT_ROWS)
            s = jnp.dot(qs_scr[rows, :], kt_tile, preferred_element_type=F32)
            if bounded:
                acc_scr[rows, :] += jnp.dot(jnp.exp2(s).astype(BF16), v_tile, preferred_element_type=F32)
            else:
                m_prev = m_scr[rows, :]
                m_new = jnp.maximum(m_prev, jnp.max(s, axis=1, keepdims=True))
                p = jnp.exp2(s - m_new).astype(BF16)
                acc_scr[rows, :] = (jnp.exp2(m_prev - m_new) * acc_scr[rows, :]
                                    + jnp.dot(p, v_tile, preferred_element_type=F32))
                m_scr[rows, :] = m_new
        return carry

    lax.fori_loop(0, n_tiles, body, 0)
    acc = acc_scr[...]
    o = acc[:, 0:HEAD_DIM] / acc[:, HEAD_DIM:HEAD_DIM + 1]
    o_ref[0] = jnp.concatenate([o[g * tq:(g + 1) * tq] for g in range(GQ_G)], axis=1)


def _attn_call(q, kt, v, logit_bound):
    b_, t, _ = q.shape
    n_keys = kt.shape[2]
    tq = min(ATT_TQ, t)
    tk = ATT_TK if n_keys % ATT_TK == 0 else n_keys
    gw = GQ_G * HEAD_DIM

    def call(bounded):
        return pl.pallas_call(
            functools.partial(_attn_kernel, n_tiles=n_keys // tk, tk=tk, bounded=bounded),
            grid=(b_, GQ_KV, t // tq),
            in_specs=[pl.BlockSpec((1, tq, gw), lambda b, kv, i: (b, i, kv)),
                      pl.BlockSpec((1, 128, n_keys), lambda b, kv, i: (b, kv, 0)),
                      pl.BlockSpec((1, 1, n_keys, 128), lambda b, kv, i: (b, kv, 0, 0))],
            out_specs=pl.BlockSpec((1, tq, gw), lambda b, kv, i: (b, i, kv)),
            out_shape=jax.ShapeDtypeStruct((b_, t, GQ_W), F32),
            scratch_shapes=[pltpu.VMEM((GQ_G * tq, 128), BF16),
                            pltpu.VMEM((GQ_G * tq, 1), F32),
                            pltpu.VMEM((GQ_G * tq, 128), F32)],
            compiler_params=_cparams("parallel", "parallel", "arbitrary"),
            name="attention_bounded" if bounded else "attention_online",
        )(q, kt, v)

    return lax.cond(logit_bound <= MAX_LOGIT_BOUND, lambda: call(True), lambda: call(False))


def _fourier_factors(t):
    bits = int(round(math.log2(t)))
    assert 1 << bits == t
    n1 = 1 << (bits // 2)
    return n1, t // n1


def _hi_lo(a):
    a = np.asarray(a, np.float64)
    hi = jnp.asarray(a, F32).astype(BF16)
    lo = (jnp.asarray(a, F32) - hi.astype(F32)).astype(BF16)
    return hi, lo


def _fourier_tables(t):
    n1, n2 = _fourier_factors(t)
    c = np.arange(FT_GC)
    ang = 2.0 * np.pi * np.outer(c, c) / FT_GC
    eye = np.eye(FT_GROUPS)
    w0 = np.concatenate([np.kron(eye, np.cos(ang)), -np.kron(eye, np.sin(ang))], axis=1)
    a1 = 2.0 * np.pi * np.outer(np.arange(n1), np.arange(n1)) / n1
    fr, fi = np.cos(a1), -np.sin(a1)
    m1 = np.block([[fr, -fi], [fi, fr]])
    tw = 2.0 * np.pi * np.outer(np.arange(n1), np.arange(n2)) / t
    a2 = 2.0 * np.pi * np.outer(np.arange(n2), np.arange(n2)) / n2
    m2 = np.concatenate([np.cos(a2), np.sin(a2)], axis=1) / math.sqrt(t * FT_GC)
    return dict(n1=n1, n2=n2, w0=_hi_lo(w0), m1=_hi_lo(m1), m2=_hi_lo(m2),
                twr=np.cos(tw).astype(np.float32), twi=(-np.sin(tw)).astype(np.float32))


def _ft_stage1_kernel(u_ref, w0h, w0l, m1h, m1l, twr_ref, twi_ref, yr_ref, yi_ref, *, n_tok):
    u = u_ref[0]
    n1 = u.shape[0]
    zr, zi = [], []
    for j in range(n_tok):
        uh, ul = _split2(u[:, j * FT_W:(j + 1) * FT_W])
        z = _dot3(uh, ul, w0h[...], w0l[...])
        zr.append(z[:, :FT_W])
        zi.append(z[:, FT_W:])
    z = jnp.concatenate([jnp.concatenate(zr, axis=1), jnp.concatenate(zi, axis=1)], axis=0)
    zh, zl = _split2(z)
    y = _dot3(m1h[...], m1l[...], zh, zl)
    y_r, y_i = y[:n1], y[n1:]
    twr, twi = twr_ref[0], twi_ref[0]
    for j in range(n_tok):
        sl = slice(j * FT_W, (j + 1) * FT_W)
        cr, ci = twr[:, j:j + 1], twi[:, j:j + 1]
        yr_ref[0, :, sl] = y_r[:, sl] * cr - y_i[:, sl] * ci
        yi_ref[0, :, sl] = y_r[:, sl] * ci + y_i[:, sl] * cr


def _ft_stage2_kernel(yr_ref, yi_ref, m2h, m2l, o_ref, *, n_k1, n2):
    for j in range(n_k1):
        y = jnp.concatenate([yr_ref[0, j * n2:(j + 1) * n2, :], yi_ref[0, j * n2:(j + 1) * n2, :]], axis=0)
        yh, yl = _split2(y)
        o_ref[0, :, j * FT_W:(j + 1) * FT_W] = _dot3(m2h[...], m2l[...], yh, yl)


def _fourier_call(u, tabs):
    b_, t, w = u.shape
    n1, n2 = tabs["n1"], tabs["n2"]
    n_tok = 8
    twr = jnp.asarray(tabs["twr"]).reshape(n1, n2 // n_tok, n_tok).transpose(1, 0, 2)
    twi = jnp.asarray(tabs["twi"]).reshape(n1, n2 // n_tok, n_tok).transpose(1, 0, 2)
    tc = n_tok * w
    row_view = jax.ShapeDtypeStruct((b_, n1, n2 * w), F32)
    yr, yi = pl.pallas_call(
        functools.partial(_ft_stage1_kernel, n_tok=n_tok),
        grid=(b_, n2 // n_tok),
        in_specs=[pl.BlockSpec((1, n1, tc), lambda b, j: (b, 0, j)),
                  _resident(tabs["w0"][0].shape), _resident(tabs["w0"][1].shape),
                  _resident(tabs["m1"][0].shape), _resident(tabs["m1"][1].shape),
                  pl.BlockSpec((1, n1, n_tok), lambda b, j: (j, 0, 0)),
                  pl.BlockSpec((1, n1, n_tok), lambda b, j: (j, 0, 0))],
        out_specs=[pl.BlockSpec((1, n1, tc), lambda b, j: (b, 0, j))] * 2,
        out_shape=[row_view, row_view],
        compiler_params=_cparams("parallel", "parallel"),
        name="fourier_stage1",
    )(u.reshape(b_, n1, n2 * w), *tabs["w0"], *tabs["m1"], twr, twi)
    n_k1 = 8
    out = pl.pallas_call(
        functools.partial(_ft_stage2_kernel, n_k1=n_k1, n2=n2),
        grid=(b_, n1 // n_k1),
        in_specs=[pl.BlockSpec((1, n_k1 * n2, w), lambda b, j: (b, j, 0)),
                  pl.BlockSpec((1, n_k1 * n2, w), lambda b, j: (b, j, 0)),
                  _resident(tabs["m2"][0].shape), _resident(tabs["m2"][1].shape)],
        out_specs=pl.BlockSpec((1, n2, n_k1 * w), lambda b, j: (b, 0, j)),
        out_shape=jax.ShapeDtypeStruct((b_, n2, n1 * w), F32),
        compiler_params=_cparams("parallel", "parallel"),
        name="fourier_stage2",
    )(yr.reshape(b_, t, w), yi.reshape(b_, t, w), *tabs["m2"])
    return out.reshape(b_, t, w)


def _merge_kernel(x_ref, mod_ref, hf_ref, hb_ref, mlo_ref, ft_ref, att_ref, of_ref, ob_ref, glr_ref,
                  wgate, bgate, wbr, wout, mlg, glg, lng, lnb, o_ref, *, alpha):
    x = x_ref[0]
    h = _modulate(x, mod_ref[0, 0:1, :], mod_ref[0, 1:2, :]).astype(BF16)
    ones = _group_ones(ML_W)
    hs = hf_ref[0] + hb_ref[0]
    hc = hs - _group_mean(hs, ones)
    hn = hc * lax.rsqrt(_group_mean(hc * hc, ones) + LN_EPS) * mlg[...]
    br_ml = hn * _sigmoid(mlo_ref[0])
    os_ = of_ref[0] + ob_ref[0]
    on = os_ * lax.rsqrt(_group_mean(os_ * os_, ones) + LN_EPS) * glg[...]
    br_gl = on * _silu(glr_ref[0])
    branches = (br_ml, ft_ref[0], att_ref[0], br_gl)
    mixed = None
    off = 0
    for j, br in enumerate(branches):
        wd = br.shape[1]
        gate = _sigmoid(jnp.dot(h, wgate[:, j * D_MODEL:(j + 1) * D_MODEL], preferred_element_type=F32)
                        + bgate[:, j * D_MODEL:(j + 1) * D_MODEL])
        u = gate * jnp.dot(br.astype(BF16), wbr[off:off + wd, :], preferred_element_type=F32)
        mixed = u if mixed is None else mixed + u
        off += wd
    y = jnp.dot(mixed.astype(BF16), wout[...], preferred_element_type=F32)
    o_ref[0] = _layernorm(alpha * x + mod_ref[0, 2:3, :] * y) * lng[...] + lnb[...]


def _merge_call(x, mod, hf, hb, ml, ftb, att, of, ob, gl, weights, alpha, tm):
    b_, t, d = x.shape
    tok = lambda w: pl.BlockSpec((1, tm, w), lambda b, i: (b, i, 0))
    last_quarter = pl.BlockSpec((1, tm, ML_W), lambda b, i: (b, i, 3))
    return pl.pallas_call(
        functools.partial(_merge_kernel, alpha=alpha),
        grid=(b_, t // tm),
        in_specs=[tok(d), pl.BlockSpec((1, N_ADA, d), lambda b, i: (b, 0, 0)),
                  tok(ML_W), tok(ML_W), last_quarter, tok(FT_W), tok(GQ_W), tok(GL_W), tok(GL_W),
                  last_quarter] + [_resident(w.shape) for w in weights],
        out_specs=tok(d),
        out_shape=jax.ShapeDtypeStruct((b_, t, d), F32),
        compiler_params=_cparams("parallel", "parallel"),
        name="merge",
    )(x, mod, hf, hb, ml, ftb, att, of, ob, gl, *weights)


def _ffn_kernel(x_ref, mod_ref, wg, wu, wd, lng, lnb, o_ref, *, alpha):
    x = x_ref[0]
    h = _modulate(x, mod_ref[0, 3:4, :], mod_ref[0, 4:5, :]).astype(BF16)
    a = jnp.dot(h, wg[...], preferred_element_type=F32)
    u = jnp.dot(h, wu[...], preferred_element_type=F32)
    f = jnp.dot((_silu(a) * u).astype(BF16), wd[...], preferred_element_type=F32)
    o_ref[0] = _layernorm(alpha * x + mod_ref[0, 5:6, :] * f) * lng[...] + lnb[...]


def _ffn_call(x, mod, weights, alpha, tm):
    b_, t, d = x.shape
    return pl.pallas_call(
        functools.partial(_ffn_kernel, alpha=alpha),
        grid=(b_, t // tm),
        in_specs=[pl.BlockSpec((1, tm, d), lambda b, i: (b, i, 0)),
                  pl.BlockSpec((1, N_ADA, d), lambda b, i: (b, 0, 0))]
                 + [_resident(w.shape) for w in weights],
        out_specs=pl.BlockSpec((1, tm, d), lambda b, i: (b, i, 0)),
        out_shape=jax.ShapeDtypeStruct((b_, t, d), F32),
        compiler_params=_cparams("parallel", "parallel"),
        name="ffn",
    )(x, mod, *weights)


def _moe_kernel(x_ref, mod_ref, wrh, wrl, br_ref, wg, wu, wd, lng, lnb, o_ref, h_scr, dense_scr, acc_scr,
                *, alpha):
    e = pl.program_id(2)

    @pl.when(e == 0)
    def _():
        h = _modulate(x_ref[0], mod_ref[0, 3:4, :], mod_ref[0, 4:5, :])
        h_scr[...] = h.astype(BF16)
        hh, hl = _split2(h)
        logits = _dot3(hh, hl, wrh[...], wrl[...])
        lane = lax.broadcasted_iota(jnp.int32, logits.shape, 1).astype(F32)
        valid = lane < N_EXPERTS
        sel = jnp.where(valid, logits + br_ref[...], -jnp.inf)

        def pick(scores):
            mx = jnp.max(scores, axis=1, keepdims=True)
            idx = jnp.min(jnp.where(scores == mx, lane, 2.0 * SMALL_W), axis=1, keepdims=True)
            return lane == idx

        first = pick(sel)
        second = pick(jnp.where(first, -jnp.inf, sel))
        l1 = jnp.sum(jnp.where(first, logits, 0.0), axis=1, keepdims=True)
        l2 = jnp.sum(jnp.where(second, logits, 0.0), axis=1, keepdims=True)
        mx = jnp.maximum(l1, l2)
        e1, e2 = jnp.exp(l1 - mx), jnp.exp(l2 - mx)
        inv = 1.0 / (e1 + e2)
        dense_scr[...] = jnp.where(first, e1 * inv, 0.0) + jnp.where(second, e2 * inv, 0.0)
        acc_scr[...] = jnp.zeros(acc_scr.shape, F32)

    h = h_scr[...]
    a = jnp.dot(h, wg[0], preferred_element_type=F32)
    u = jnp.dot(h, wu[0], preferred_element_type=F32)
    f = jnp.dot((_silu(a) * u).astype(BF16), wd[0], preferred_element_type=F32)
    dense = dense_scr[...]
    lane = lax.broadcasted_iota(jnp.int32, dense.shape, 1)
    w_e = jnp.sum(jnp.where(lane == e, dense, 0.0), axis=1, keepdims=True)
    acc_scr[...] += w_e * f

    @pl.when(e == pl.num_programs(2) - 1)
    def _():
        o_ref[0] = (_layernorm(alpha * x_ref[0] + mod_ref[0, 5:6, :] * acc_scr[...]) * lng[...]
                    + lnb[...])


def _moe_call(x, mod, wrh, wrl, br, wg, wu, wd, lng, lnb, alpha, tm):
    b_, t, d = x.shape
    n_e, _, ff = wg.shape
    return pl.pallas_call(
        functools.partial(_moe_kernel, alpha=alpha),
        grid=(b_, t // tm, n_e),
        in_specs=[pl.BlockSpec((1, tm, d), lambda b, i, e: (b, i, 0)),
                  pl.BlockSpec((1, N_ADA, d), lambda b, i, e: (b, 0, 0)),
                  _resident(wrh.shape), _resident(wrl.shape), _resident(br.shape),
                  pl.BlockSpec((1, d, ff), lambda b, i, e: (e, 0, 0)),
                  pl.BlockSpec((1, d, ff), lambda b, i, e: (e, 0, 0)),
                  pl.BlockSpec((1, ff, d), lambda b, i, e: (e, 0, 0)),
                  _resident(lng.shape), _resident(lnb.shape)],
        out_specs=pl.BlockSpec((1, tm, d), lambda b, i, e: (b, i, 0)),
        out_shape=jax.ShapeDtypeStruct((b_, t, d), F32),
        scratch_shapes=[pltpu.VMEM((tm, d), BF16), pltpu.VMEM((tm, SMALL_W), F32), pltpu.VMEM((tm, d), F32)],
        compiler_params=_cparams("parallel", "parallel", "arbitrary"),
        name="moe",
    )(x, mod, wrh, wrl, br, wg, wu, wd, lng, lnb)


def _rope_tables(t):
    rows = t // GRID_W
    row = jnp.repeat(jnp.arange(rows, dtype=F32), GRID_W)
    col = jnp.tile(jnp.arange(GRID_W, dtype=F32), rows)
    inv = jnp.power(ROPE_BASE, -jnp.arange(ROPE_PAIRS, dtype=F32) / ROPE_PAIRS)
    ar, ac = row[:, None] * inv, col[:, None] * inv
    cos = jnp.concatenate([jnp.cos(ar), jnp.cos(ar), jnp.cos(ac), jnp.cos(ac)], axis=1)
    sin = jnp.concatenate([-jnp.sin(ar), jnp.sin(ar), -jnp.sin(ac), jnp.sin(ac)], axis=1)
    return jnp.tile(cos, (1, 2)), jnp.tile(sin, (1, 2))


def _token_tile(t, pref):
    return pref if t % pref == 0 else t


def kernel(x, c, ctx, c_ctx, w_ada, b_ada, w_in, ml_gate_b, ml_norm_g, gq_qnorm_g, gq_knorm_g, gl_w2, gl_b2,
           gl_norm_g, w_branch, w_gate, b_gate, w_out, ln1_g, ln1_b, ln2_g, ln2_b, ffd_wg, ffd_wu, ffd_wd,
           moe_wr, moe_br, moe_wg, moe_wu, moe_wd):
    b_, s_len, d = x.shape
    n_ctx = ctx.shape[1]
    depth = w_in.shape[0]
    alpha = (2.0 * depth) ** 0.25
    bf = lambda a: a.astype(BF16)
    row = lambda a: a.reshape(1, -1).astype(F32)

    cos_l, sin_l = _rope_tables(s_len)
    cos_c, sin_c = jnp.ones((n_ctx, 128), F32), jnp.zeros((n_ctx, 128), F32)
    ft_l, ft_c = _fourier_tables(s_len), _fourier_tables(n_ctx)
    tm_l, tm_c = _token_tile(s_len, 512), _token_tile(n_ctx, 512)

    cc = jnp.zeros((8, d), F32).at[0:b_].set(c).at[b_].set(c_ctx)
    xc = ctx
    for l in range(depth):
        last = l == depth - 1
        ada = _ada_call(cc, bf(w_ada[l]), row(b_ada[l])).reshape(8, N_ADA, d)
        mod_l = ada[0:b_]
        mod_c = jnp.broadcast_to(ada[b_][None], (b_, N_ADA, d))

        w = w_in[l]
        o_gate, o_ft, o_gq, o_gl, o_af = 4 * ML_W, 4 * ML_W + 16, 4 * ML_W + 16 + FT_W, \
            4 * ML_W + 16 + FT_W + GQ_W + 2 * GQ_KW, 4 * ML_W + 16 + FT_W + GQ_W + 2 * GQ_KW + 4 * GL_W
        ml_scale = jnp.concatenate([jnp.ones((ML_W,)), jnp.full((ML_W,), HEAD_DIM ** -0.5), jnp.ones((2 * ML_W,))])
        gl_scale = jnp.concatenate([jnp.full((GL_W,), HEAD_DIM ** -0.5), jnp.ones((3 * GL_W,))])
        w_small = jnp.concatenate([w[:, o_gate:o_gate + 16], w[:, o_af:o_af + 2 * GL_RANK],
                                   jnp.zeros((d, SMALL_W - 16 - 2 * GL_RANK), F32)], axis=1)
        ws = [bf(w[:, 0:4 * ML_W] * ml_scale), bf(w[:, o_ft:o_ft + FT_W]), bf(w[:, o_gq:o_gl]),
              bf(w[:, o_gl:o_af] * gl_scale), bf(w_small)]
        gate_bias = jnp.zeros((1, SMALL_W), F32).at[0, 0:16].set(ml_gate_b[l].reshape(-1))
        w2e = jnp.zeros((SMALL_W, 2 * GL_W), F32)
        w2e = w2e.at[16:16 + GL_RANK, 0:GL_W].set(gl_w2[l, 0]).at[16 + GL_RANK:16 + 2 * GL_RANK, GL_W:].set(gl_w2[l, 1])
        merge_w = [bf(w_gate[l]), row(b_gate[l]), bf(w_branch[l]), bf(w_out[l]), row(ml_norm_g[l]),
                   row(gl_norm_g[l]), row(ln1_g[l]), row(ln1_b[l])]
        gq_g = jnp.tile(row(gq_qnorm_g[l]), (1, 2))
        gk_g = jnp.tile(row(gq_knorm_g[l]), (1, 2))
        logit_bound = (1.02 * LOG2_E * HEAD_DIM ** 0.5) * jnp.max(jnp.abs(gq_qnorm_g[l])) * jnp.max(jnp.abs(gq_knorm_g[l]))
        score_bias = jnp.zeros((1, HEAD_DIM), F32).at[0, 0].set(-logit_bound)

        def mixers(xs, mod, tm, cos, sin, states):
            ml, ftu, gq, gl, small = _inproj_call(xs, mod, ws, tm)
            hf, hb, ml_s, ml_m = _mlstm_call(ml, small, gate_bias, states[0], states[1])
            of, ob, gl_s = _gla_call(gl, small, bf(w2e), gl_b2[l].astype(F32), states[2])
            q, kt, v = _qkprep_call(gq, cos, sin, gq_g, gk_g, score_bias, tm)
            return dict(ml=ml, ftu=ftu, gl=gl, hf=hf, hb=hb, of=of, ob=ob, q=q, kt=kt, v=v), (ml_s, ml_m, gl_s)

        zero_states = (jnp.zeros((b_, 2 * ML_HEADS, HEAD_DIM, 2 * HEAD_DIM), F32),
                       jnp.zeros((b_, 2 * ML_HEADS, 1, SMALL_W), F32),
                       jnp.zeros((b_, 2, GL_W, GL_W), F32))
        pc, ctx_states = mixers(xc, mod_c, tm_c, cos_c, sin_c, zero_states)
        pl_, _ = mixers(x, mod_l, tm_l, cos_l, sin_l, ctx_states)

        att_l = _attn_call(pl_["q"], jnp.concatenate([pc["kt"], pl_["kt"]], axis=2),
                           jnp.concatenate([pc["v"], pl_["v"]], axis=2), logit_bound)
        x_mid = _merge_call(x, mod_l, pl_["hf"], pl_["hb"], pl_["ml"], _fourier_call(pl_["ftu"], ft_l), att_l,
                            pl_["of"], pl_["ob"], pl_["gl"], merge_w, alpha, tm_l)
        if not last:
            att_c = _attn_call(pc["q"], pc["kt"], pc["v"], logit_bound)
            xc = _merge_call(xc, mod_c, pc["hf"], pc["hb"], pc["ml"], _fourier_call(pc["ftu"], ft_c), att_c,
                             pc["of"], pc["ob"], pc["gl"], merge_w, alpha, tm_c)

        j = l // 2
        if l % 2 == 0:
            ffn_w = [bf(ffd_wg[j]), bf(ffd_wu[j]), bf(ffd_wd[j]), row(ln2_g[l]), row(ln2_b[l])]
            x = _ffn_call(x_mid, mod_l, ffn_w, alpha, tm_l)
            if not last:
                xc = _ffn_call(xc, mod_c, ffn_w, alpha, tm_c)
        else:
            wr = jnp.zeros((d, SMALL_W), F32).at[:, 0:N_EXPERTS].set(moe_wr[j])
            wrh = bf(wr)
            wrl = bf(wr - wrh.astype(F32))
            brp = jnp.zeros((1, SMALL_W), F32).at[0, 0:N_EXPERTS].set(moe_br[j])
            moe_args = (wrh, wrl, brp, bf(moe_wg[j]), bf(moe_wu[j]), bf(moe_wd[j]), row(ln2_g[l]), row(ln2_b[l]))
            x = _moe_call(x_mid, mod_l, *moe_args, alpha, _token_tile(s_len, 1024))
            if not last:
                xc = _moe_call(xc, mod_c, *moe_args, alpha, tm_c)
    return x
```

```python
import functools
import math

import jax
import jax.numpy as jnp
import numpy as np
from jax import lax
from jax.experimental import pallas as pl
from jax.experimental.pallas import tpu as pltpu

F32 = jnp.float32
BF16 = jnp.bfloat16

D_MODEL = 1024
GRID_W = 64
HEAD_DIM = 64
ML_HEADS = 4
ML_W = ML_HEADS * HEAD_DIM
FT_GROUPS = 4
FT_GC = 64
FT_W = FT_GROUPS * FT_GC
GQ_KV = 2
GQ_G = 4
GQ_W = GQ_KV * GQ_G * HEAD_DIM
GQ_KW = GQ_KV * HEAD_DIM
ROPE_PAIRS = HEAD_DIM // 4
ROPE_BASE = 10000.0
GL_HEADS = 4
GL_W = GL_HEADS * HEAD_DIM
GL_RANK = 16
GL_TAU = 16.0
N_EXPERTS = 8
N_ADA = 6
LN_EPS = 1e-6
SMALL_W = 128

ML_CHUNK = 256
GL_CHUNK = 128
GL_SUB = 16
GL_EXP_CLAMP = 80.0
ATT_TQ = 512
ATT_TK = 1280
ATT_ROWS = 256
LOG2_E = 1.4426950408889634
MAX_LOGIT_BOUND = 60.0
NEG_BIG = -1e30

VMEM_LIMIT = 56 * 1024 * 1024


def _cparams(*sem):
    return pltpu.CompilerParams(dimension_semantics=sem, vmem_limit_bytes=VMEM_LIMIT)


def _resident(shape):
    nd = len(shape)
    return pl.BlockSpec(shape, lambda *_: (0,) * nd, pipeline_mode=pl.Buffered(1))


def _bdot(a, b):
    return jnp.dot(a.astype(BF16), b.astype(BF16), preferred_element_type=F32)


def _split2(x):
    hi = x.astype(BF16)
    lo = (x - hi.astype(F32)).astype(BF16)
    return hi, lo


def _split3(x):
    a = x.astype(BF16)
    r = x - a.astype(F32)
    b = r.astype(BF16)
    c = (r - b.astype(F32)).astype(BF16)
    return a, b, c


def _dot_exact_rhs(x, m_bf16):
    a, b, c = _split3(x)
    d = functools.partial(jnp.dot, preferred_element_type=F32)
    return d(a, m_bf16) + d(b, m_bf16) + d(c, m_bf16)


def _dot_exact_lhs(m_bf16, x):
    a, b, c = _split3(x)
    d = functools.partial(jnp.dot, preferred_element_type=F32)
    return d(m_bf16, a) + d(m_bf16, b) + d(m_bf16, c)


def _dot3(a_hi, a_lo, b_hi, b_lo):
    d = functools.partial(jnp.dot, preferred_element_type=F32)
    return d(a_hi, b_hi) + d(a_hi, b_lo) + d(a_lo, b_hi)


def _sigmoid(x):
    return 1.0 / (1.0 + jnp.exp(-x))


def _silu(x):
    return x * _sigmoid(x)


def _log_sigmoid(x):
    return jnp.minimum(x, 0.0) - jnp.log(1.0 + jnp.exp(-jnp.abs(x)))


def _layernorm(x):
    mu = jnp.mean(x, axis=-1, keepdims=True)
    xc = x - mu
    var = jnp.mean(xc * xc, axis=-1, keepdims=True)
    return xc * lax.rsqrt(var + LN_EPS)


def _modulate(x, shift, scale):
    return _layernorm(x) * (1.0 + scale) + shift


def _group_ones(width):
    r = lax.broadcasted_iota(jnp.int32, (width, width), 0) >> 6
    c = lax.broadcasted_iota(jnp.int32, (width, width), 1) >> 6
    return jnp.where(r == c, 1.0, 0.0).astype(BF16)


def _group_mean(x, ones):
    return _dot_exact_rhs(x, ones) * (1.0 / HEAD_DIM)


def _tri(n, upper):
    r = lax.broadcasted_iota(jnp.int32, (n, n), 0)
    c = lax.broadcasted_iota(jnp.int32, (n, n), 1)
    keep = (c >= r) if upper else (c <= r)
    return jnp.where(keep, 1.0, 0.0).astype(BF16)


def _ada_kernel(c_ref, w_ref, b_ref, o_ref):
    o_ref[...] = _bdot(_silu(c_ref[...]), w_ref[...]) + b_ref[...]


def _ada_call(cc, w, b):
    rows, d = cc.shape
    n = w.shape[1]
    tn = 1024
    return pl.pallas_call(
        _ada_kernel,
        grid=(n // tn,),
        in_specs=[pl.BlockSpec((rows, d), lambda j: (0, 0)),
                  pl.BlockSpec((d, tn), lambda j: (0, j)),
                  pl.BlockSpec((1, tn), lambda j: (0, j))],
        out_specs=pl.BlockSpec((rows, tn), lambda j: (0, j)),
        out_shape=jax.ShapeDtypeStruct((rows, n), F32),
        compiler_params=_cparams("parallel"),
        name="ada",
    )(cc, w, b)


def _inproj_kernel(x_ref, mod_ref, w_ml, w_ft, w_gq, w_gl, w_sm, o_ml, o_ft, o_gq, o_gl, o_sm):
    h = _modulate(x_ref[0], mod_ref[0, 0:1, :], mod_ref[0, 1:2, :]).astype(BF16)
    for w, o in ((w_ml, o_ml), (w_ft, o_ft), (w_gq, o_gq), (w_gl, o_gl), (w_sm, o_sm)):
        o[0] = jnp.dot(h, w[...], preferred_element_type=F32)


def _inproj_call(x, mod, ws, tm):
    b_, t, d = x.shape
    widths = [w.shape[1] for w in ws]
    return pl.pallas_call(
        _inproj_kernel,
        grid=(b_, t // tm),
        in_specs=[pl.BlockSpec((1, tm, d), lambda b, i: (b, i, 0)),
                  pl.BlockSpec((1, N_ADA, d), lambda b, i: (b, 0, 0))]
                 + [_resident(w.shape) for w in ws],
        out_specs=[pl.BlockSpec((1, tm, n), lambda b, i: (b, i, 0)) for n in widths],
        out_shape=[jax.ShapeDtypeStruct((b_, t, n), F32) for n in widths],
        compiler_params=_cparams("parallel", "parallel"),
        name="inproj",
    )(x, mod, *ws)


def _mlstm_kernel(qkv_f, sm_f, qkv_b, sm_b, bias_ref, s0_ref, m0_ref,
                  hf_ref, hb_ref, st_ref, mt_ref, s_scr, m_scr, *, chunk):
    i = pl.program_id(1)
    n_l = chunk

    @pl.when(i == 0)
    def _():
        s_scr[...] = s0_ref[0]
        m_scr[...] = m0_ref[0]

    row = lax.broadcasted_iota(jnp.int32, (n_l, n_l), 0)
    col = lax.broadcasted_iota(jnp.int32, (n_l, n_l), 1)
    ones_v = jnp.ones((n_l, HEAD_DIM), F32)
    states = [s_scr[idx] for idx in range(2 * ML_HEADS)]
    m_prevs = [m_scr[idx][:, 0:1] for idx in range(2 * ML_HEADS)]
    new_states, new_ms, new_outs = [], [], []
    for d, (qkv_ref, sm_ref, h_ref) in enumerate(((qkv_f, sm_f, hf_ref), (qkv_b, sm_b, hb_ref))):
        rev = d == 1
        blk = qkv_ref[0]
        pre = sm_ref[0] + bias_ref[...]
        bcum = _dot_exact_lhs(_tri(n_l, upper=rev), _log_sigmoid(pre))
        pre_t = pre.T
        b_t = bcum.T
        k_t = blk[:, ML_W:2 * ML_W].T
        mask = (col >= row) if rev else (col <= row)
        last = 0 if rev else n_l - 1
        outs = []
        for h in range(ML_HEADS):
            ci = 8 * d + h
            cf = 8 * d + 4 + h
            idx = 4 * d + h
            q = blk[:, h * HEAD_DIM:(h + 1) * HEAD_DIM].astype(BF16)
            kt = k_t[h * HEAD_DIM:(h + 1) * HEAD_DIM, :]
            v = blk[:, 2 * ML_W + h * HEAD_DIM:2 * ML_W + (h + 1) * HEAD_DIM]
            v_ext = jnp.concatenate([v, ones_v], axis=1).astype(BF16)
            bc = bcum[:, cf:cf + 1]
            br = b_t[cf:cf + 1, :]
            ir = pre_t[ci:ci + 1, :]
            state = states[idx]
            m_prev = m_prevs[idx]

            dmat = jnp.where(mask, bc - br + ir, -jnp.inf)
            inter = bc + m_prev
            m_t = jnp.maximum(inter, jnp.max(dmat, axis=1, keepdims=True))
            w_intra = jnp.exp(dmat - m_t) * jnp.dot(q, kt.astype(BF16), preferred_element_type=F32)
            w_inter = jnp.exp(inter - m_t)
            qs = jnp.dot(q, state.astype(BF16), preferred_element_type=F32)
            num = (jnp.dot(w_intra.astype(BF16), v_ext, preferred_element_type=F32)[:, :HEAD_DIM]
                   + w_inter * qs[:, :HEAD_DIM])
            den = jnp.sum(w_intra, axis=1, keepdims=True) + w_inter * qs[:, HEAD_DIM:HEAD_DIM + 1]
            outs.append(num / jnp.maximum(jnp.abs(den), jnp.exp(-m_t)))

            b_last = bc[last:last + 1, :]
            g_row = b_last - br + ir
            m_new = jnp.maximum(b_last + m_prev, jnp.max(g_row, axis=1, keepdims=True))
            ws = jnp.exp(g_row - m_new)
            wc = jnp.exp(b_last + m_prev - m_new)
            new_states.append(wc * state
                              + jnp.dot((kt * ws).astype(BF16), v_ext, preferred_element_type=F32))
            new_ms.append(jnp.broadcast_to(m_new, (1, SMALL_W)))
        new_outs.append(jnp.concatenate(outs, axis=1))
    hf_ref[0] = new_outs[0]
    hb_ref[0] = new_outs[1]
    for idx in range(2 * ML_HEADS):
        s_scr[idx] = new_states[idx]
        m_scr[idx] = new_ms[idx]

    @pl.when(i == pl.num_programs(1) - 1)
    def _():
        st_ref[0] = s_scr[...]
        mt_ref[0] = m_scr[...]


def _mlstm_call(ml, small, bias, s0, m0):
    b_, t, _ = ml.shape
    chunk = min(ML_CHUNK, t)
    n = t // chunk
    fwd = lambda b, i: (b, i, 0)
    bwd = lambda b, i: (b, n - 1 - i, 0)
    state_spec = pl.BlockSpec((1, 2 * ML_HEADS, HEAD_DIM, 2 * HEAD_DIM), lambda b, i: (b, 0, 0, 0))
    m_spec = pl.BlockSpec((1, 2 * ML_HEADS, 1, SMALL_W), lambda b, i: (b, 0, 0, 0))
    return pl.pallas_call(
        functools.partial(_mlstm_kernel, chunk=chunk),
        grid=(b_, n),
        in_specs=[pl.BlockSpec((1, chunk, 3 * ML_W), fwd), pl.BlockSpec((1, chunk, SMALL_W), fwd),
                  pl.BlockSpec((1, chunk, 3 * ML_W), bwd), pl.BlockSpec((1, chunk, SMALL_W), bwd),
                  pl.BlockSpec((1, SMALL_W), lambda b, i: (0, 0)), state_spec, m_spec],
        out_specs=[pl.BlockSpec((1, chunk, ML_W), fwd), pl.BlockSpec((1, chunk, ML_W), bwd),
                   state_spec, m_spec],
        out_shape=[jax.ShapeDtypeStruct((b_, t, ML_W), F32), jax.ShapeDtypeStruct((b_, t, ML_W), F32),
                   jax.ShapeDtypeStruct(s0.shape, F32), jax.ShapeDtypeStruct(m0.shape, F32)],
        scratch_shapes=[pltpu.VMEM((2 * ML_HEADS, HEAD_DIM, 2 * HEAD_DIM), F32),
                        pltpu.VMEM((2 * ML_HEADS, 1, SMALL_W), F32)],
        compiler_params=_cparams("parallel", "arbitrary"),
        name="mlstm",
    )(ml, small, ml, small, bias, s0, m0)


def _gla_kernel(qkv_f, sm_f, qkv_b, sm_b, w2_ref, b2_ref, s0_ref,
                of_ref, ob_ref, st_ref, s_scr, *, chunk):
    i = pl.program_id(1)
    n_l = chunk
    n_sub = n_l // GL_SUB
    width = GL_W

    @pl.when(i == 0)
    def _():
        s_scr[...] = s0_ref[0]

    lane = lax.broadcasted_iota(jnp.int32, (GL_SUB, width), 1) >> 6
    head_masks = [lane == h for h in range(GL_HEADS)]
    bd_mask = ((lax.broadcasted_iota(jnp.int32, (width, width), 0) >> 6)
               == (lax.broadcasted_iota(jnp.int32, (width, width), 1) >> 6))
    states = [s_scr[0], s_scr[1]]
    new_states, new_outs = [], []
    for d, (qkv_ref, sm_ref, o_ref) in enumerate(((qkv_f, sm_f, of_ref), (qkv_b, sm_b, ob_ref))):
        rev = d == 1
        blk = qkv_ref[0]
        q = blk[:, 0:width]
        k = blk[:, width:2 * width]
        v = blk[:, 2 * width:3 * width]
        a = _bdot(sm_ref[0], w2_ref[...])[:, d * width:(d + 1) * width] + b2_ref[d:d + 1, :]
        la = _log_sigmoid(a) * (1.0 / GL_TAU)
        g = _dot_exact_lhs(_tri(n_l, upper=rev), la)
        last = 0 if rev else n_l - 1
        g_end = g[last:last + 1, :]
        state = states[d]
        o_inter = lax.dot_general((q * jnp.exp(g)).astype(BF16), state.astype(BF16),
                                  (((1,), (1,)), ((), ())), preferred_element_type=F32)
        v_bf = v.astype(BF16)
        pieces = []
        for s in range(n_sub):
            lo = s * GL_SUB
            hi = lo + GL_SUB
            if rev:
                k_lo, k_hi, ref_row = lo, n_l, hi - 1
            else:
                k_lo, k_hi, ref_row = 0, hi, lo
            r = g[ref_row:ref_row + 1, :]
            qt = q[lo:hi] * jnp.exp(g[lo:hi] - r)
            kt = k[k_lo:k_hi] * jnp.exp(jnp.minimum(r - g[k_lo:k_hi], GL_EXP_CLAMP))
            qstack = jnp.concatenate([jnp.where(hm, qt, 0.0) for hm in head_masks], axis=0)
            amat = lax.dot_general(qstack.astype(BF16), kt.astype(BF16),
                                   (((1,), (1,)), ((), ())), preferred_element_type=F32)
            n_k = k_hi - k_lo
            t_idx = lo + (lax.broadcasted_iota(jnp.int32, (GL_HEADS * GL_SUB, n_k), 0) & (GL_SUB - 1))
            s_idx = k_lo + lax.broadcasted_iota(jnp.int32, (GL_HEADS * GL_SUB, n_k), 1)
            keep = (s_idx >= t_idx) if rev else (s_idx <= t_idx)
            amat = jnp.where(keep, amat, 0.0)
            ov = jnp.dot(amat.astype(BF16), v_bf[k_lo:k_hi], preferred_element_type=F32)
            acc = jnp.where(head_masks[0], ov[0:GL_SUB], 0.0)
            for h in range(1, GL_HEADS):
                acc = acc + jnp.where(head_masks[h], ov[h * GL_SUB:(h + 1) * GL_SUB], 0.0)
            pieces.append(acc)
        new_outs.append(o_inter + jnp.concatenate(pieces, axis=0))

        kg = k * jnp.exp(g_end - g)
        upd = jnp.dot(v.T.astype(BF16), kg.astype(BF16), preferred_element_type=F32)
        new_states.append(jnp.exp(g_end) * state + jnp.where(bd_mask, upd, 0.0))
    of_ref[0] = new_outs[0]
    ob_ref[0] = new_outs[1]
    s_scr[0] = new_states[0]
    s_scr[1] = new_states[1]

    @pl.when(i == pl.num_programs(1) - 1)
    def _():
        st_ref[0] = s_scr[...]


def _gla_call(gl, small, w2e, b2, s0):
    b_, t, _ = gl.shape
    chunk = min(GL_CHUNK, t)
    n = t // chunk
    fwd = lambda b, i: (b, i, 0)
    bwd = lambda b, i: (b, n - 1 - i, 0)
    state_spec = pl.BlockSpec((1, 2, GL_W, GL_W), lambda b, i: (b, 0, 0, 0))
    return pl.pallas_call(
        functools.partial(_gla_kernel, chunk=chunk),
        grid=(b_, n),
        in_specs=[pl.BlockSpec((1, chunk, 3 * GL_W), fwd), pl.BlockSpec((1, chunk, SMALL_W), fwd),
                  pl.BlockSpec((1, chunk, 3 * GL_W), bwd), pl.BlockSpec((1, chunk, SMALL_W), bwd),
                  _resident(w2e.shape), _resident(b2.shape), state_spec],
        out_specs=[pl.BlockSpec((1, chunk, GL_W), fwd), pl.BlockSpec((1, chunk, GL_W), bwd), state_spec],
        out_shape=[jax.ShapeDtypeStruct((b_, t, GL_W), F32), jax.ShapeDtypeStruct((b_, t, GL_W), F32),
                   jax.ShapeDtypeStruct(s0.shape, F32)],
        scratch_shapes=[pltpu.VMEM((2, GL_W, GL_W), F32)],
        compiler_params=_cparams("parallel", "arbitrary"),
        name="gla",
    )(gl, small, gl, small, w2e, b2, s0)


def _qkprep_kernel(gq_ref, cos_ref, sin_ref, gq_g, gk_g, sb_ref, qt_out, k_out, vt_out):
    x = gq_ref[0]
    tm = x.shape[0]

    def norm_rope(z, g, reps):
        width = z.shape[1]
        msq = _group_mean(z * z, _group_ones(width))
        zn = z * lax.rsqrt(msq + LN_EPS) * jnp.tile(g, (1, reps))
        lane = lax.broadcasted_iota(jnp.int32, zn.shape, 1)
        partner = jnp.where((lane & 31) < ROPE_PAIRS,
                            pltpu.roll(zn, width - ROPE_PAIRS, axis=1),
                            pltpu.roll(zn, ROPE_PAIRS, axis=1))
        return zn * jnp.tile(cos_ref[...], (1, reps)) + partner * jnp.tile(sin_ref[...], (1, reps))

    lane = lax.broadcasted_iota(jnp.int32, (tm, 128), 1)

    def pad_heads(z, extra):
        out = []
        for p in range(z.shape[1] // 128):
            pair = z[:, p * 128:(p + 1) * 128]
            for base in (pair, pltpu.roll(pair, HEAD_DIM, axis=1)):
                out.append(jnp.where(lane < HEAD_DIM, base, jnp.where(lane == HEAD_DIM, extra, 0.0)))
        return out

    q = norm_rope(x[:, 0:GQ_W], gq_g[...], GQ_W // 128) * (LOG2_E * HEAD_DIM ** -0.5)
    qt_out[0] = jnp.concatenate(pad_heads(q, 1.0), axis=1).T.astype(BF16)
    k = norm_rope(x[:, GQ_W:GQ_W + GQ_KW], gk_g[...], GQ_KW // 128)
    for j, kj in enumerate(pad_heads(k, sb_ref[0:1, 0:1])):
        k_out[0, j] = kj.astype(BF16)
    v = x[:, GQ_W + GQ_KW:GQ_W + 2 * GQ_KW]
    vt_out[0] = jnp.concatenate(pad_heads(v, 1.0), axis=1).T.astype(BF16)


def _qkprep_call(gq, cos, sin, gq_g, gk_g, score_bias, tm):
    b_, t, w = gq.shape
    n_q = GQ_KV * GQ_G
    return pl.pallas_call(
        _qkprep_kernel,
        grid=(b_, t // tm),
        in_specs=[pl.BlockSpec((1, tm, w), lambda b, i: (b, i, 0)),
                  pl.BlockSpec((tm, 128), lambda b, i: (i, 0)),
                  pl.BlockSpec((tm, 128), lambda b, i: (i, 0)),
                  _resident(gq_g.shape), _resident(gk_g.shape), _resident(score_bias.shape)],
        out_specs=[pl.BlockSpec((1, n_q * 128, tm), lambda b, i: (b, 0, i)),
                   pl.BlockSpec((1, GQ_KV, tm, 128), lambda b, i: (b, 0, i, 0)),
                   pl.BlockSpec((1, GQ_KV * 128, tm), lambda b, i: (b, 0, i))],
        out_shape=[jax.ShapeDtypeStruct((b_, n_q * 128, t), BF16),
                   jax.ShapeDtypeStruct((b_, GQ_KV, t, 128), BF16),
                   jax.ShapeDtypeStruct((b_, GQ_KV * 128, t), BF16)],
        compiler_params=_cparams("parallel", "parallel"),
        name="qkprep",
    )(gq, cos, sin, gq_g, gk_g, score_bias)


def _attn_kernel(qt_ref, k_ref, vt_ref, o_ref, m_scr, acc_scr, *, n_tiles, tk, bounded):
    acc_scr[...] = jnp.zeros(acc_scr.shape, F32)
    if not bounded:
        m_scr[...] = jnp.full(m_scr.shape, NEG_BIG, F32)

    def body(j, carry):
        start = pl.multiple_of(j * tk, tk)
        k_tile = k_ref[0, 0, pl.ds(start, tk), :]
        vt_tile = vt_ref[0, :, pl.ds(start, tk)]
        for g in range(GQ_G):
            st = jnp.dot(k_tile, qt_ref[0, g * 128:(g + 1) * 128, :], preferred_element_type=F32)
            if bounded:
                acc_scr[g] += jnp.dot(vt_tile, jnp.exp2(st).astype(BF16), preferred_element_type=F32)
            else:
                m_prev = m_scr[g]
                m_new = jnp.maximum(m_prev, jnp.max(st, axis=0, keepdims=True))
                p = jnp.exp2(st - m_new).astype(BF16)
                acc_scr[g] = (jnp.exp2(m_prev - m_new) * acc_scr[g]
                              + jnp.dot(vt_tile, p, preferred_element_type=F32))
                m_scr[g] = m_new
        return carry

    lax.fori_loop(0, n_tiles, body, 0)
    outs = []
    for g in range(GQ_G):
        acc = acc_scr[g]
        outs.append(acc[0:HEAD_DIM] / acc[HEAD_DIM:HEAD_DIM + 1])
    o_ref[0] = jnp.concatenate(outs, axis=0).T


def _attn_call(qt, k, vt, logit_bound):
    b_, _, t = qt.shape
    n_keys = k.shape[2]
    tq = min(ATT_TQ, t)
    tk = ATT_TK if n_keys % ATT_TK == 0 else n_keys
    gw = GQ_G * HEAD_DIM

    def call(bounded):
        return pl.pallas_call(
            functools.partial(_attn_kernel, n_tiles=n_keys // tk, tk=tk, bounded=bounded),
            grid=(b_, GQ_KV, t // tq),
            in_specs=[pl.BlockSpec((1, GQ_G * 128, tq), lambda b, kv, i: (b, kv, i)),
                      pl.BlockSpec((1, 1, n_keys, 128), lambda b, kv, i: (b, kv, 0, 0)),
                      pl.BlockSpec((1, 128, n_keys), lambda b, kv, i: (b, kv, 0))],
            out_specs=pl.BlockSpec((1, tq, gw), lambda b, kv, i: (b, i, kv)),
            out_shape=jax.ShapeDtypeStruct((b_, t, GQ_W), F32),
            scratch_shapes=[pltpu.VMEM((GQ_G, 1, tq), F32),
                            pltpu.VMEM((GQ_G, 128, tq), F32)],
            compiler_params=_cparams("parallel", "parallel", "arbitrary"),
            name="attention_bounded" if bounded else "attention_online",
        )(qt, k, vt)

    return lax.cond(logit_bound <= MAX_LOGIT_BOUND, lambda: call(True), lambda: call(False))


def _fourier_factors(t):
    bits = int(round(math.log2(t)))
    assert 1 << bits == t
    n1 = 1 << (bits // 2)
    return n1, t // n1


def _hi_lo(a):
    a = np.asarray(a, np.float64)
    hi = jnp.asarray(a, F32).astype(BF16)
    lo = (jnp.asarray(a, F32) - hi.astype(F32)).astype(BF16)
    return hi, lo


def _fourier_tables(t):
    n1, n2 = _fourier_factors(t)
    c = np.arange(FT_GC)
    ang = 2.0 * np.pi * np.outer(c, c) / FT_GC
    eye = np.eye(FT_GROUPS)
    w0 = np.concatenate([np.kron(eye, np.cos(ang)), -np.kron(eye, np.sin(ang))], axis=1)
    a1 = 2.0 * np.pi * np.outer(np.arange(n1), np.arange(n1)) / n1
    fr, fi = np.cos(a1), -np.sin(a1)
    m1 = np.block([[fr, -fi], [fi, fr]])
    tw = 2.0 * np.pi * np.outer(np.arange(n1), np.arange(n2)) / t
    a2 = 2.0 * np.pi * np.outer(np.arange(n2), np.arange(n2)) / n2
    m2 = np.concatenate([np.cos(a2), np.sin(a2)], axis=1) / math.sqrt(t * FT_GC)
    return dict(n1=n1, n2=n2, w0=_hi_lo(w0), m1=_hi_lo(m1), m2=_hi_lo(m2),
                twr=np.cos(tw).astype(np.float32), twi=(-np.sin(tw)).astype(np.float32))


def _ft_stage1_kernel(u_ref, w0h, w0l, m1h, m1l, twr_ref, twi_ref, yr_ref, yi_ref, *, n_tok):
    u = u_ref[0]
    n1 = u.shape[0]
    zr, zi = [], []
    for j in range(n_tok):
        uh, ul = _split2(u[:, j * FT_W:(j + 1) * FT_W])
        z = _dot3(uh, ul, w0h[...], w0l[...])
        zr.append(z[:, :FT_W])
        zi.append(z[:, FT_W:])
    z = jnp.concatenate([jnp.concatenate(zr, axis=1), jnp.concatenate(zi, axis=1)], axis=0)
    zh, zl = _split2(z)
    y = _dot3(m1h[...], m1l[...], zh, zl)
    y_r, y_i = y[:n1], y[n1:]
    twr, twi = twr_ref[0], twi_ref[0]
    for j in range(n_tok):
        sl = slice(j * FT_W, (j + 1) * FT_W)
        cr, ci = twr[:, j:j + 1], twi[:, j:j + 1]
        yr_ref[0, :, sl] = y_r[:, sl] * cr - y_i[:, sl] * ci
        yi_ref[0, :, sl] = y_r[:, sl] * ci + y_i[:, sl] * cr


def _ft_stage2_kernel(yr_ref, yi_ref, m2h, m2l, o_ref, *, n_k1, n2):
    for j in range(n_k1):
        y = jnp.concatenate([yr_ref[0, j * n2:(j + 1) * n2, :], yi_ref[0, j * n2:(j + 1) * n2, :]], axis=0)
        yh, yl = _split2(y)
        o_ref[0, :, j * FT_W:(j + 1) * FT_W] = _dot3(m2h[...], m2l[...], yh, yl)


def _fourier_call(u, tabs):
    b_, t, w = u.shape
    n1, n2 = tabs["n1"], tabs["n2"]
    n_tok = 8
    twr = jnp.asarray(tabs["twr"]).reshape(n1, n2 // n_tok, n_tok).transpose(1, 0, 2)
    twi = jnp.asarray(tabs["twi"]).reshape(n1, n2 // n_tok, n_tok).transpose(1, 0, 2)
    tc = n_tok * w
    row_view = jax.ShapeDtypeStruct((b_, n1, n2 * w), F32)
    yr, yi = pl.pallas_call(
        functools.partial(_ft_stage1_kernel, n_tok=n_tok),
        grid=(b_, n2 // n_tok),
        in_specs=[pl.BlockSpec((1, n1, tc), lambda b, j: (b, 0, j)),
                  _resident(tabs["w0"][0].shape), _resident(tabs["w0"][1].shape),
                  _resident(tabs["m1"][0].shape), _resident(tabs["m1"][1].shape),
                  pl.BlockSpec((1, n1, n_tok), lambda b, j: (j, 0, 0)),
                  pl.BlockSpec((1, n1, n_tok), lambda b, j: (j, 0, 0))],
        out_specs=[pl.BlockSpec((1, n1, tc), lambda b, j: (b, 0, j))] * 2,
        out_shape=[row_view, row_view],
        compiler_params=_cparams("parallel", "parallel"),
        name="fourier_stage1",
    )(u.reshape(b_, n1, n2 * w), *tabs["w0"], *tabs["m1"], twr, twi)
    n_k1 = 8
    out = pl.pallas_call(
        functools.partial(_ft_stage2_kernel, n_k1=n_k1, n2=n2),
        grid=(b_, n1 // n_k1),
        in_specs=[pl.BlockSpec((1, n_k1 * n2, w), lambda b, j: (b, j, 0)),
                  pl.BlockSpec((1, n_k1 * n2, w), lambda b, j: (b, j, 0)),
                  _resident(tabs["m2"][0].shape), _resident(tabs["m2"][1].shape)],
        out_specs=pl.BlockSpec((1, n2, n_k1 * w), lambda b, j: (b, 0, j)),
        out_shape=jax.ShapeDtypeStruct((b_, n2, n1 * w), F32),
        compiler_params=_cparams("parallel", "parallel"),
        name="fourier_stage2",
    )(yr.reshape(b_, t, w), yi.reshape(b_, t, w), *tabs["m2"])
    return out.reshape(b_, t, w)


def _merge_kernel(x_ref, mod_ref, hf_ref, hb_ref, mlo_ref, ft_ref, att_ref, of_ref, ob_ref, glr_ref,
                  wgate, bgate, wbr, wout, mlg, glg, lng, lnb, o_ref, *, alpha):
    x = x_ref[0]
    h = _modulate(x, mod_ref[0, 0:1, :], mod_ref[0, 1:2, :]).astype(BF16)
    ones = _group_ones(ML_W)
    hs = hf_ref[0] + hb_ref[0]
    hc = hs - _group_mean(hs, ones)
    hn = hc * lax.rsqrt(_group_mean(hc * hc, ones) + LN_EPS) * mlg[...]
    br_ml = hn * _sigmoid(mlo_ref[0])
    os_ = of_ref[0] + ob_ref[0]
    on = os_ * lax.rsqrt(_group_mean(os_ * os_, ones) + LN_EPS) * glg[...]
    br_gl = on * _silu(glr_ref[0])
    branches = (br_ml, ft_ref[0], att_ref[0], br_gl)
    mixed = None
    off = 0
    for j, br in enumerate(branches):
        wd = br.shape[1]
        gate = _sigmoid(jnp.dot(h, wgate[:, j * D_MODEL:(j + 1) * D_MODEL], preferred_element_type=F32)
                        + bgate[:, j * D_MODEL:(j + 1) * D_MODEL])
        u = gate * jnp.dot(br.astype(BF16), wbr[off:off + wd, :], preferred_element_type=F32)
        mixed = u if mixed is None else mixed + u
        off += wd
    y = jnp.dot(mixed.astype(BF16), wout[...], preferred_element_type=F32)
    o_ref[0] = _layernorm(alpha * x + mod_ref[0, 2:3, :] * y) * lng[...] + lnb[...]


def _merge_call(x, mod, hf, hb, ml, ftb, att, of, ob, gl, weights, alpha, tm):
    b_, t, d = x.shape
    tok = lambda w: pl.BlockSpec((1, tm, w), lambda b, i: (b, i, 0))
    last_quarter = pl.BlockSpec((1, tm, ML_W), lambda b, i: (b, i, 3))
    return pl.pallas_call(
        functools.partial(_merge_kernel, alpha=alpha),
        grid=(b_, t // tm),
        in_specs=[tok(d), pl.BlockSpec((1, N_ADA, d), lambda b, i: (b, 0, 0)),
                  tok(ML_W), tok(ML_W), last_quarter, tok(FT_W), tok(GQ_W), tok(GL_W), tok(GL_W),
                  last_quarter] + [_resident(w.shape) for w in weights],
        out_specs=tok(d),
        out_shape=jax.ShapeDtypeStruct((b_, t, d), F32),
        compiler_params=_cparams("parallel", "parallel"),
        name="merge",
    )(x, mod, hf, hb, ml, ftb, att, of, ob, gl, *weights)


def _ffn_kernel(x_ref, mod_ref, wg, wu, wd, lng, lnb, o_ref, *, alpha):
    x = x_ref[0]
    h = _modulate(x, mod_ref[0, 3:4, :], mod_ref[0, 4:5, :]).astype(BF16)
    a = jnp.dot(h, wg[...], preferred_element_type=F32)
    u = jnp.dot(h, wu[...], preferred_element_type=F32)
    f = jnp.dot((_silu(a) * u).astype(BF16), wd[...], preferred_element_type=F32)
    o_ref[0] = _layernorm(alpha * x + mod_ref[0, 5:6, :] * f) * lng[...] + lnb[...]


def _ffn_call(x, mod, weights, alpha, tm):
    b_, t, d = x.shape
    return pl.pallas_call(
        functools.partial(_ffn_kernel, alpha=alpha),
        grid=(b_, t // tm),
        in_specs=[pl.BlockSpec((1, tm, d), lambda b, i: (b, i, 0)),
                  pl.BlockSpec((1, N_ADA, d), lambda b, i: (b, 0, 0))]
                 + [_resident(w.shape) for w in weights],
        out_specs=pl.BlockSpec((1, tm, d), lambda b, i: (b, i, 0)),
        out_shape=jax.ShapeDtypeStruct((b_, t, d), F32),
        compiler_params=_cparams("parallel", "parallel"),
        name="ffn",
    )(x, mod, *weights)


def _moe_kernel(x_ref, mod_ref, wrh, wrl, br_ref, wg, wu, wd, lng, lnb, o_ref, h_scr, dense_scr, acc_scr,
                *, alpha):
    e = pl.program_id(2)

    @pl.when(e == 0)
    def _():
        h = _modulate(x_ref[0], mod_ref[0, 3:4, :], mod_ref[0, 4:5, :])
        h_scr[...] = h.astype(BF16)
        hh, hl = _split2(h)
        logits = _dot3(hh, hl, wrh[...], wrl[...])
        lane = lax.broadcasted_iota(jnp.int32, logits.shape, 1).astype(F32)
        valid = lane < N_EXPERTS
        sel = jnp.where(valid, logits + br_ref[...], -jnp.inf)

        def pick(scores):
            mx = jnp.max(scores, axis=1, keepdims=True)
            idx = jnp.min(jnp.where(scores == mx, lane, 2.0 * SMALL_W), axis=1, keepdims=True)
            return lane == idx

        first = pick(sel)
        second = pick(jnp.where(first, -jnp.inf, sel))
        l1 = jnp.sum(jnp.where(first, logits, 0.0), axis=1, keepdims=True)
        l2 = jnp.sum(jnp.where(second, logits, 0.0), axis=1, keepdims=True)
        mx = jnp.maximum(l1, l2)
        e1, e2 = jnp.exp(l1 - mx), jnp.exp(l2 - mx)
        inv = 1.0 / (e1 + e2)
        dense_scr[...] = jnp.where(first, e1 * inv, 0.0) + jnp.where(second, e2 * inv, 0.0)
        acc_scr[...] = jnp.zeros(acc_scr.shape, F32)

    h = h_scr[...]
    a = jnp.dot(h, wg[0], preferred_element_type=F32)
    u = jnp.dot(h, wu[0], preferred_element_type=F32)
    f = jnp.dot((_silu(a) * u).astype(BF16), wd[0], preferred_element_type=F32)
    dense = dense_scr[...]
    lane = lax.broadcasted_iota(jnp.int32, dense.shape, 1)
    w_e = jnp.sum(jnp.where(lane == e, dense, 0.0), axis=1, keepdims=True)
    acc_scr[...] += w_e * f

    @pl.when(e == pl.num_programs(2) - 1)
    def _():
        o_ref[0] = (_layernorm(alpha * x_ref[0] + mod_ref[0, 5:6, :] * acc_scr[...]) * lng[...]
                    + lnb[...])


def _moe_call(x, mod, wrh, wrl, br, wg, wu, wd, lng, lnb, alpha, tm):
    b_, t, d = x.shape
    n_e, _, ff = wg.shape
    return pl.pallas_call(
        functools.partial(_moe_kernel, alpha=alpha),
        grid=(b_, t // tm, n_e),
        in_specs=[pl.BlockSpec((1, tm, d), lambda b, i, e: (b, i, 0)),
                  pl.BlockSpec((1, N_ADA, d), lambda b, i, e: (b, 0, 0)),
                  _resident(wrh.shape), _resident(wrl.shape), _resident(br.shape),
                  pl.BlockSpec((1, d, ff), lambda b, i, e: (e, 0, 0)),
                  pl.BlockSpec((1, d, ff), lambda b, i, e: (e, 0, 0)),
                  pl.BlockSpec((1, ff, d), lambda b, i, e: (e, 0, 0)),
                  _resident(lng.shape), _resident(lnb.shape)],
        out_specs=pl.BlockSpec((1, tm, d), lambda b, i, e: (b, i, 0)),
        out_shape=jax.ShapeDtypeStruct((b_, t, d), F32),
        scratch_shapes=[pltpu.VMEM((tm, d), BF16), pltpu.VMEM((tm, SMALL_W), F32), pltpu.VMEM((tm, d), F32)],
        compiler_params=_cparams("parallel", "parallel", "arbitrary"),
        name="moe",
    )(x, mod, wrh, wrl, br, wg, wu, wd, lng, lnb)


def _rope_tables(t):
    rows = t // GRID_W
    row = jnp.repeat(jnp.arange(rows, dtype=F32), GRID_W)
    col = jnp.tile(jnp.arange(GRID_W, dtype=F32), rows)
    inv = jnp.power(ROPE_BASE, -jnp.arange(ROPE_PAIRS, dtype=F32) / ROPE_PAIRS)
    ar, ac = row[:, None] * inv, col[:, None] * inv
    cos = jnp.concatenate([jnp.cos(ar), jnp.cos(ar), jnp.cos(ac), jnp.cos(ac)], axis=1)
    sin = jnp.concatenate([-jnp.sin(ar), jnp.sin(ar), -jnp.sin(ac), jnp.sin(ac)], axis=1)
    return jnp.tile(cos, (1, 2)), jnp.tile(sin, (1, 2))


def _token_tile(t, pref):
    return pref if t % pref == 0 else t


def kernel(x, c, ctx, c_ctx, w_ada, b_ada, w_in, ml_gate_b, ml_norm_g, gq_qnorm_g, gq_knorm_g, gl_w2, gl_b2,
           gl_norm_g, w_branch, w_gate, b_gate, w_out, ln1_g, ln1_b, ln2_g, ln2_b, ffd_wg, ffd_wu, ffd_wd,
           moe_wr, moe_br, moe_wg, moe_wu, moe_wd):
    b_, s_len, d = x.shape
    n_ctx = ctx.shape[1]
    depth = w_in.shape[0]
    alpha = (2.0 * depth) ** 0.25
    bf = lambda a: a.astype(BF16)
    row = lambda a: a.reshape(1, -1).astype(F32)

    cos_l, sin_l = _rope_tables(s_len)
    cos_c, sin_c = jnp.ones((n_ctx, 128), F32), jnp.zeros((n_ctx, 128), F32)
    ft_l, ft_c = _fourier_tables(s_len), _fourier_tables(n_ctx)
    tm_l, tm_c = _token_tile(s_len, 512), _token_tile(n_ctx, 512)

    cc = jnp.zeros((8, d), F32).at[0:b_].set(c).at[b_].set(c_ctx)
    xc = ctx
    for l in range(depth):
        last = l == depth - 1
        ada = _ada_call(cc, bf(w_ada[l]), row(b_ada[l])).reshape(8, N_ADA, d)
        mod_l = ada[0:b_]
        mod_c = jnp.broadcast_to(ada[b_][None], (b_, N_ADA, d))

        w = w_in[l]
        o_gate, o_ft, o_gq, o_gl, o_af = 4 * ML_W, 4 * ML_W + 16, 4 * ML_W + 16 + FT_W, \
            4 * ML_W + 16 + FT_W + GQ_W + 2 * GQ_KW, 4 * ML_W + 16 + FT_W + GQ_W + 2 * GQ_KW + 4 * GL_W
        ml_scale = jnp.concatenate([jnp.ones((ML_W,)), jnp.full((ML_W,), HEAD_DIM ** -0.5), jnp.ones((2 * ML_W,))])
        gl_scale = jnp.concatenate([jnp.full((GL_W,), HEAD_DIM ** -0.5), jnp.ones((3 * GL_W,))])
        w_small = jnp.concatenate([w[:, o_gate:o_gate + 16], w[:, o_af:o_af + 2 * GL_RANK],
                                   jnp.zeros((d, SMALL_W - 16 - 2 * GL_RANK), F32)], axis=1)
        ws = [bf(w[:, 0:4 * ML_W] * ml_scale), bf(w[:, o_ft:o_ft + FT_W]), bf(w[:, o_gq:o_gl]),
              bf(w[:, o_gl:o_af] * gl_scale), bf(w_small)]
        gate_bias = jnp.zeros((1, SMALL_W), F32).at[0, 0:16].set(ml_gate_b[l].reshape(-1))
        w2e = jnp.zeros((SMALL_W, 2 * GL_W), F32)
        w2e = w2e.at[16:16 + GL_RANK, 0:GL_W].set(gl_w2[l, 0]).at[16 + GL_RANK:16 + 2 * GL_RANK, GL_W:].set(gl_w2[l, 1])
        merge_w = [bf(w_gate[l]), row(b_gate[l]), bf(w_branch[l]), bf(w_out[l]), row(ml_norm_g[l]),
                   row(gl_norm_g[l]), row(ln1_g[l]), row(ln1_b[l])]
        gq_g = jnp.tile(row(gq_qnorm_g[l]), (1, 2))
        gk_g = jnp.tile(row(gq_knorm_g[l]), (1, 2))
        logit_bound = (1.02 * LOG2_E * HEAD_DIM ** 0.5) * jnp.max(jnp.abs(gq_qnorm_g[l])) * jnp.max(jnp.abs(gq_knorm_g[l]))
        score_bias = jnp.zeros((1, HEAD_DIM), F32).at[0, 0].set(-logit_bound)

        def mixers(xs, mod, tm, cos, sin, states):
            ml, ftu, gq, gl, small = _inproj_call(xs, mod, ws, tm)
            hf, hb, ml_s, ml_m = _mlstm_call(ml, small, gate_bias, states[0], states[1])
            of, ob, gl_s = _gla_call(gl, small, bf(w2e), gl_b2[l].astype(F32), states[2])
            qt, k, vt = _qkprep_call(gq, cos, sin, gq_g, gk_g, score_bias, tm)
            return dict(ml=ml, ftu=ftu, gl=gl, hf=hf, hb=hb, of=of, ob=ob, qt=qt, k=k, vt=vt), (ml_s, ml_m, gl_s)

        zero_states = (jnp.zeros((b_, 2 * ML_HEADS, HEAD_DIM, 2 * HEAD_DIM), F32),
                       jnp.zeros((b_, 2 * ML_HEADS, 1, SMALL_W), F32),
                       jnp.zeros((b_, 2, GL_W, GL_W), F32))
        pc, ctx_states = mixers(xc, mod_c, tm_c, cos_c, sin_c, zero_states)
        pl_, _ = mixers(x, mod_l, tm_l, cos_l, sin_l, ctx_states)

        att_l = _attn_call(pl_["qt"], jnp.concatenate([pc["k"], pl_["k"]], axis=2),
                           jnp.concatenate([pc["vt"], pl_["vt"]], axis=2), logit_bound)
        x_mid = _merge_call(x, mod_l, pl_["hf"], pl_["hb"], pl_["ml"], _fourier_call(pl_["ftu"], ft_l), att_l,
                            pl_["of"], pl_["ob"], pl_["gl"], merge_w, alpha, tm_l)
        if not last:
            att_c = _attn_call(pc["qt"], pc["k"], pc["vt"], logit_bound)
            xc = _merge_call(xc, mod_c, pc["hf"], pc["hb"], pc["ml"], _fourier_call(pc["ftu"], ft_c), att_c,
                             pc["of"], pc["ob"], pc["gl"], merge_w, alpha, tm_c)

        j = l // 2
        if l % 2 == 0:
            ffn_w = [bf(ffd_wg[j]), bf(ffd_wu[j]), bf(ffd_wd[j]), row(ln2_g[l]), row(ln2_b[l])]
            x = _ffn_call(x_mid, mod_l, ffn_w, alpha, tm_l)
            if not last:
                xc = _ffn_call(xc, mod_c, ffn_w, alpha, tm_c)
        else:
            wr = jnp.zeros((d, SMALL_W), F32).at[:, 0:N_EXPERTS].set(moe_wr[j])
            wrh = bf(wr)
            wrl = bf(wr - wrh.astype(F32))
            brp = jnp.zeros((1, SMALL_W), F32).at[0, 0:N_EXPERTS].set(moe_br[j])
            moe_args = (wrh, wrl, brp, bf(moe_wg[j]), bf(moe_wu[j]), bf(moe_wd[j]), row(ln2_g[l]), row(ln2_b[l]))
            x = _moe_call(x_mid, mod_l, *moe_args, alpha, _token_tile(s_len, 1024))
            if not last:
                xc = _moe_call(xc, mod_c, *moe_args, alpha, tm_c)
    return x
```

```python
import functools
import math

import jax
import jax.numpy as jnp
import numpy as np
from jax import lax
from jax.experimental import pallas as pl
from jax.experimental.pallas import tpu as pltpu

F32 = jnp.float32
BF16 = jnp.bfloat16

D_MODEL = 1024
GRID_W = 64
HEAD_DIM = 64
ML_HEADS = 4
ML_W = ML_HEADS * HEAD_DIM
FT_GROUPS = 4
FT_GC = 64
FT_W = FT_GROUPS * FT_GC
GQ_KV = 2
GQ_G = 4
GQ_W = GQ_KV * GQ_G * HEAD_DIM
GQ_KW = GQ_KV * HEAD_DIM
ROPE_PAIRS = HEAD_DIM // 4
ROPE_BASE = 10000.0
GL_HEADS = 4
GL_W = GL_HEADS * HEAD_DIM
GL_RANK = 16
GL_TAU = 16.0
N_EXPERTS = 8
N_ADA = 6
LN_EPS = 1e-6
SMALL_W = 128

ML_CHUNK = 256
GL_CHUNK = 128
GL_SUB = 16
GL_EXP_CLAMP = 80.0
ATT_TQ = 512
ATT_TK = 1280
ATT_ROWS = 256
LOG2_E = 1.4426950408889634
MAX_LOGIT_BOUND = 60.0
NEG_BIG = -1e30

VMEM_LIMIT = 56 * 1024 * 1024


def _cparams(*sem):
    return pltpu.CompilerParams(dimension_semantics=sem, vmem_limit_bytes=VMEM_LIMIT)


def _resident(shape):
    nd = len(shape)
    return pl.BlockSpec(shape, lambda *_: (0,) * nd, pipeline_mode=pl.Buffered(1))


def _bdot(a, b):
    return jnp.dot(a.astype(BF16), b.astype(BF16), preferred_element_type=F32)


def _split2(x):
    hi = x.astype(BF16)
    lo = (x - hi.astype(F32)).astype(BF16)
    return hi, lo


def _split3(x):
    a = x.astype(BF16)
    r = x - a.astype(F32)
    b = r.astype(BF16)
    c = (r - b.astype(F32)).astype(BF16)
    return a, b, c


def _dot_exact_rhs(x, m_bf16):
    a, b, c = _split3(x)
    d = functools.partial(jnp.dot, preferred_element_type=F32)
    return d(a, m_bf16) + d(b, m_bf16) + d(c, m_bf16)


def _dot_exact_lhs(m_bf16, x):
    a, b, c = _split3(x)
    d = functools.partial(jnp.dot, preferred_element_type=F32)
    return d(m_bf16, a) + d(m_bf16, b) + d(m_bf16, c)


def _dot3(a_hi, a_lo, b_hi, b_lo):
    d = functools.partial(jnp.dot, preferred_element_type=F32)
    return d(a_hi, b_hi) + d(a_hi, b_lo) + d(a_lo, b_hi)


def _sigmoid(x):
    return 1.0 / (1.0 + jnp.exp(-x))


def _silu(x):
    return x * _sigmoid(x)


def _log_sigmoid(x):
    return jnp.minimum(x, 0.0) - jnp.log(1.0 + jnp.exp(-jnp.abs(x)))


def _layernorm(x):
    mu = jnp.mean(x, axis=-1, keepdims=True)
    xc = x - mu
    var = jnp.mean(xc * xc, axis=-1, keepdims=True)
    return xc * lax.rsqrt(var + LN_EPS)


def _modulate(x, shift, scale):
    return _layernorm(x) * (1.0 + scale) + shift


def _group_ones(width):
    r = lax.broadcasted_iota(jnp.int32, (width, width), 0) >> 6
    c = lax.broadcasted_iota(jnp.int32, (width, width), 1) >> 6
    return jnp.where(r == c, 1.0, 0.0).astype(BF16)


def _group_mean(x, ones):
    return _dot_exact_rhs(x, ones) * (1.0 / HEAD_DIM)


def _tri(n, upper):
    r = lax.broadcasted_iota(jnp.int32, (n, n), 0)
    c = lax.broadcasted_iota(jnp.int32, (n, n), 1)
    keep = (c >= r) if upper else (c <= r)
    return jnp.where(keep, 1.0, 0.0).astype(BF16)


def _ada_kernel(c_ref, w_ref, b_ref, o_ref):
    o_ref[...] = _bdot(_silu(c_ref[...]), w_ref[...]) + b_ref[...]


def _ada_call(cc, w, b):
    rows, d = cc.shape
    n = w.shape[1]
    tn = 1024
    return pl.pallas_call(
        _ada_kernel,
        grid=(n // tn,),
        in_specs=[pl.BlockSpec((rows, d), lambda j: (0, 0)),
                  pl.BlockSpec((d, tn), lambda j: (0, j)),
                  pl.BlockSpec((1, tn), lambda j: (0, j))],
        out_specs=pl.BlockSpec((rows, tn), lambda j: (0, j)),
        out_shape=jax.ShapeDtypeStruct((rows, n), F32),
        compiler_params=_cparams("parallel"),
        name="ada",
    )(cc, w, b)


def _inproj_kernel(x_ref, mod_ref, w_ml, w_ft, w_gq, w_gl, w_sm, o_ml, o_ft, o_gq, o_gl, o_sm):
    h = _modulate(x_ref[0], mod_ref[0, 0:1, :], mod_ref[0, 1:2, :]).astype(BF16)
    for w, o in ((w_ml, o_ml), (w_ft, o_ft), (w_gq, o_gq), (w_gl, o_gl), (w_sm, o_sm)):
        o[0] = jnp.dot(h, w[...], preferred_element_type=F32)


def _inproj_call(x, mod, ws, tm):
    b_, t, d = x.shape
    widths = [w.shape[1] for w in ws]
    return pl.pallas_call(
        _inproj_kernel,
        grid=(b_, t // tm),
        in_specs=[pl.BlockSpec((1, tm, d), lambda b, i: (b, i, 0)),
                  pl.BlockSpec((1, N_ADA, d), lambda b, i: (b, 0, 0))]
                 + [_resident(w.shape) for w in ws],
        out_specs=[pl.BlockSpec((1, tm, n), lambda b, i: (b, i, 0)) for n in widths],
        out_shape=[jax.ShapeDtypeStruct((b_, t, n), F32) for n in widths],
        compiler_params=_cparams("parallel", "parallel"),
        name="inproj",
    )(x, mod, *ws)


def _mlstm_kernel(qkv_f, sm_f, qkv_b, sm_b, bias_ref, s0_ref, m0_ref,
                  hf_ref, hb_ref, st_ref, mt_ref, s_scr, m_scr, *, chunk):
    i = pl.program_id(1)
    n_l = chunk

    @pl.when(i == 0)
    def _():
        s_scr[...] = s0_ref[0]
        m_scr[...] = m0_ref[0]

    row = lax.broadcasted_iota(jnp.int32, (n_l, n_l), 0)
    col = lax.broadcasted_iota(jnp.int32, (n_l, n_l), 1)
    lane = lax.broadcasted_iota(jnp.int32, (n_l, 128), 1)
    sub = lax.broadcasted_iota(jnp.int32, (HEAD_DIM, n_l), 0)
    ones_row = jnp.where(sub == 0, 1.0, 0.0)
    for d, (qkv_ref, sm_ref, h_ref) in enumerate(((qkv_f, sm_f, hf_ref), (qkv_b, sm_b, hb_ref))):
        rev = d == 1
        blk = qkv_ref[0]
        pre = sm_ref[0] + bias_ref[...]
        bcum = _dot_exact_lhs(_tri(n_l, upper=rev), _log_sigmoid(pre))
        pre_t = pre.T
        b_t = bcum.T
        q_t = blk[:, 0:ML_W].T.astype(BF16)
        v_t = blk[:, 2 * ML_W:3 * ML_W].T
        mask = (row >= col) if rev else (row <= col)
        last = 0 if rev else n_l - 1
        outs = []
        for h in range(ML_HEADS):
            ci = 8 * d + h
            cf = 8 * d + 4 + h
            idx = 4 * d + h
            pair = h // 2
            own = (lane >= HEAD_DIM) if h % 2 else (lane < HEAD_DIM)
            k_own = jnp.where(own, blk[:, ML_W + pair * 128:ML_W + (pair + 1) * 128], 0.0).astype(BF16)
            q_pair = q_t[pair * 128:(pair + 1) * 128, :]
            v_h = v_t[h * HEAD_DIM:(h + 1) * HEAD_DIM, :]
            c_col = pre[:, ci:ci + 1] - bcum[:, cf:cf + 1]
            b_row = b_t[cf:cf + 1, :]
            i_row = pre_t[ci:ci + 1, :]
            state = s_scr[idx]
            m_prev = m_scr[idx][:, 0:1]

            dmat = jnp.where(mask, b_row + c_col, -jnp.inf)
            inter = b_row + m_prev
            m_t = jnp.maximum(inter, jnp.max(dmat, axis=0, keepdims=True))
            w_intra = jnp.exp(dmat - m_t) * jnp.dot(k_own, q_pair, preferred_element_type=F32)
            w_inter = jnp.exp(inter - m_t)
            sq = jnp.dot(state.astype(BF16), q_pair, preferred_element_type=F32)
            num = (jnp.dot(v_h.astype(BF16), w_intra.astype(BF16), preferred_element_type=F32)
                   + w_inter * sq[0:HEAD_DIM])
            den = jnp.sum(w_intra, axis=0, keepdims=True) + w_inter * sq[HEAD_DIM:HEAD_DIM + 1]
            outs.append(num / jnp.maximum(jnp.abs(den), jnp.exp(-m_t)))

            b_last = b_row[:, last:last + 1]
            g_row = b_last - b_row + i_row
            m_new = jnp.maximum(b_last + m_prev, jnp.max(g_row, axis=1, keepdims=True))
            ws = jnp.exp(g_row - m_new)
            wc = jnp.exp(b_last + m_prev - m_new)
            v_ext = jnp.concatenate([v_h, ones_row], axis=0)
            s_scr[idx] = wc * state + jnp.dot((v_ext * ws).astype(BF16), k_own, preferred_element_type=F32)
            m_scr[idx] = jnp.broadcast_to(m_new, (1, SMALL_W))
        h_ref[0] = jnp.concatenate(outs, axis=0).T

    @pl.when(i == pl.num_programs(1) - 1)
    def _():
        st_ref[0] = s_scr[...]
        mt_ref[0] = m_scr[...]


def _mlstm_call(ml, small, bias, s0, m0):
    b_, t, _ = ml.shape
    chunk = min(ML_CHUNK, t)
    n = t // chunk
    fwd = lambda b, i: (b, i, 0)
    bwd = lambda b, i: (b, n - 1 - i, 0)
    state_spec = pl.BlockSpec((1, 2 * ML_HEADS, 2 * HEAD_DIM, 2 * HEAD_DIM), lambda b, i: (b, 0, 0, 0))
    m_spec = pl.BlockSpec((1, 2 * ML_HEADS, 1, SMALL_W), lambda b, i: (b, 0, 0, 0))
    return pl.pallas_call(
        functools.partial(_mlstm_kernel, chunk=chunk),
        grid=(b_, n),
        in_specs=[pl.BlockSpec((1, chunk, 3 * ML_W), fwd), pl.BlockSpec((1, chunk, SMALL_W), fwd),
                  pl.BlockSpec((1, chunk, 3 * ML_W), bwd), pl.BlockSpec((1, chunk, SMALL_W), bwd),
                  pl.BlockSpec((1, SMALL_W), lambda b, i: (0, 0)), state_spec, m_spec],
        out_specs=[pl.BlockSpec((1, chunk, ML_W), fwd), pl.BlockSpec((1, chunk, ML_W), bwd),
                   state_spec, m_spec],
        out_shape=[jax.ShapeDtypeStruct((b_, t, ML_W), F32), jax.ShapeDtypeStruct((b_, t, ML_W), F32),
                   jax.ShapeDtypeStruct(s0.shape, F32), jax.ShapeDtypeStruct(m0.shape, F32)],
        scratch_shapes=[pltpu.VMEM((2 * ML_HEADS, 2 * HEAD_DIM, 2 * HEAD_DIM), F32),
                        pltpu.VMEM((2 * ML_HEADS, 1, SMALL_W), F32)],
        compiler_params=_cparams("parallel", "arbitrary"),
        name="mlstm",
    )(ml, small, ml, small, bias, s0, m0)


def _gla_kernel(qkv_f, sm_f, qkv_b, sm_b, w2_ref, b2_ref, s0_ref,
                of_ref, ob_ref, st_ref, s_scr, *, chunk):
    i = pl.program_id(1)
    n_l = chunk
    n_sub = n_l // GL_SUB
    width = GL_W

    @pl.when(i == 0)
    def _():
        s_scr[...] = s0_ref[0]

    lane = lax.broadcasted_iota(jnp.int32, (GL_SUB, width), 1) >> 6
    head_masks = [lane == h for h in range(GL_HEADS)]
    chunk_lane = lax.broadcasted_iota(jnp.int32, (n_l, width), 1) >> 6
    chunk_masks = [chunk_lane == h for h in range(GL_HEADS)]
    bd_mask = ((lax.broadcasted_iota(jnp.int32, (width, width), 0) >> 6)
               == (lax.broadcasted_iota(jnp.int32, (width, width), 1) >> 6))
    states = [s_scr[0], s_scr[1]]
    new_states, new_outs = [], []
    for d, (qkv_ref, sm_ref, o_ref) in enumerate(((qkv_f, sm_f, of_ref), (qkv_b, sm_b, ob_ref))):
        rev = d == 1
        blk = qkv_ref[0]
        q = blk[:, 0:width]
        k = blk[:, width:2 * width]
        v = blk[:, 2 * width:3 * width]
        a = _bdot(sm_ref[0], w2_ref[...])[:, d * width:(d + 1) * width] + b2_ref[d:d + 1, :]
        la = _log_sigmoid(a) * (1.0 / GL_TAU)
        g = _dot_exact_lhs(_tri(n_l, upper=rev), la)
        last = 0 if rev else n_l - 1
        g_end = g[last:last + 1, :]
        state = states[d]
        o_inter = lax.dot_general((q * jnp.exp(g)).astype(BF16), state.astype(BF16),
                                  (((1,), (1,)), ((), ())), preferred_element_type=F32)
        v_bf = v.astype(BF16)
        a_blocks = []
        t_idx = lax.broadcasted_iota(jnp.int32, (GL_HEADS * GL_SUB, n_l), 0) & (GL_SUB - 1)
        s_idx = lax.broadcasted_iota(jnp.int32, (GL_HEADS * GL_SUB, n_l), 1)
        for s in range(n_sub):
            lo = s * GL_SUB
            hi = lo + GL_SUB
            r = g[hi - 1:hi, :] if rev else g[lo:lo + 1, :]
            qt = q[lo:hi] * jnp.exp(g[lo:hi] - r)
            kt = k * jnp.exp(jnp.minimum(r - g, GL_EXP_CLAMP))
            qstack = jnp.concatenate([jnp.where(hm, qt, 0.0) for hm in head_masks], axis=0)
            amat = lax.dot_general(qstack.astype(BF16), kt.astype(BF16),
                                   (((1,), (1,)), ((), ())), preferred_element_type=F32)
            keep = (s_idx >= lo + t_idx) if rev else (s_idx <= lo + t_idx)
            a_blocks.append(jnp.where(keep, amat, 0.0).astype(BF16))
        a_all = jnp.concatenate([a_blocks[s][h * GL_SUB:(h + 1) * GL_SUB]
                                 for h in range(GL_HEADS) for s in range(n_sub)], axis=0)
        ov = jnp.dot(a_all, v_bf, preferred_element_type=F32)
        o_intra = jnp.where(chunk_masks[0], ov[0:n_l], 0.0)
        for h in range(1, GL_HEADS):
            o_intra = o_intra + jnp.where(chunk_masks[h], ov[h * n_l:(h + 1) * n_l], 0.0)
        new_outs.append(o_inter + o_intra)

        kg = k * jnp.exp(g_end - g)
        upd = jnp.dot(v.T.astype(BF16), kg.astype(BF16), preferred_element_type=F32)
        new_states.append(jnp.exp(g_end) * state + jnp.where(bd_mask, upd, 0.0))
    of_ref[0] = new_outs[0]
    ob_ref[0] = new_outs[1]
    s_scr[0] = new_states[0]
    s_scr[1] = new_states[1]

    @pl.when(i == pl.num_programs(1) - 1)
    def _():
        st_ref[0] = s_scr[...]


def _gla_call(gl, small, w2e, b2, s0):
    b_, t, _ = gl.shape
    chunk = min(GL_CHUNK, t)
    n = t // chunk
    fwd = lambda b, i: (b, i, 0)
    bwd = lambda b, i: (b, n - 1 - i, 0)
    state_spec = pl.BlockSpec((1, 2, GL_W, GL_W), lambda b, i: (b, 0, 0, 0))
    return pl.pallas_call(
        functools.partial(_gla_kernel, chunk=chunk),
        grid=(b_, n),
        in_specs=[pl.BlockSpec((1, chunk, 3 * GL_W), fwd), pl.BlockSpec((1, chunk, SMALL_W), fwd),
                  pl.BlockSpec((1, chunk, 3 * GL_W), bwd), pl.BlockSpec((1, chunk, SMALL_W), bwd),
                  _resident(w2e.shape), _resident(b2.shape), state_spec],
        out_specs=[pl.BlockSpec((1, chunk, GL_W), fwd), pl.BlockSpec((1, chunk, GL_W), bwd), state_spec],
        out_shape=[jax.ShapeDtypeStruct((b_, t, GL_W), F32), jax.ShapeDtypeStruct((b_, t, GL_W), F32),
                   jax.ShapeDtypeStruct(s0.shape, F32)],
        scratch_shapes=[pltpu.VMEM((2, GL_W, GL_W), F32)],
        compiler_params=_cparams("parallel", "arbitrary"),
        name="gla",
    )(gl, small, gl, small, w2e, b2, s0)


def _qkprep_kernel(gq_ref, cos_ref, sin_ref, gq_g, gk_g, sb_ref, qt_out, k_out, vt_out):
    x = gq_ref[0]
    tm = x.shape[0]

    def norm_rope(z, g, reps):
        width = z.shape[1]
        msq = _group_mean(z * z, _group_ones(width))
        zn = z * lax.rsqrt(msq + LN_EPS) * jnp.tile(g, (1, reps))
        lane = lax.broadcasted_iota(jnp.int32, zn.shape, 1)
        partner = jnp.where((lane & 31) < ROPE_PAIRS,
                            pltpu.roll(zn, width - ROPE_PAIRS, axis=1),
                            pltpu.roll(zn, ROPE_PAIRS, axis=1))
        return zn * jnp.tile(cos_ref[...], (1, reps)) + partner * jnp.tile(sin_ref[...], (1, reps))

    lane = lax.broadcasted_iota(jnp.int32, (tm, 128), 1)

    def pad_heads(z, extra):
        out = []
        for p in range(z.shape[1] // 128):
            pair = z[:, p * 128:(p + 1) * 128]
            for base in (pair, pltpu.roll(pair, HEAD_DIM, axis=1)):
                out.append(jnp.where(lane < HEAD_DIM, base, jnp.where(lane == HEAD_DIM, extra, 0.0)))
        return out

    q = norm_rope(x[:, 0:GQ_W], gq_g[...], GQ_W // 128) * (LOG2_E * HEAD_DIM ** -0.5)
    qt_out[0] = jnp.concatenate(pad_heads(q, 1.0), axis=1).T.astype(BF16)
    k = norm_rope(x[:, GQ_W:GQ_W + GQ_KW], gk_g[...], GQ_KW // 128)
    for j, kj in enumerate(pad_heads(k, sb_ref[0:1, 0:1])):
        k_out[0, j] = kj.astype(BF16)
    v = x[:, GQ_W + GQ_KW:GQ_W + 2 * GQ_KW]
    vt_out[0] = jnp.concatenate(pad_heads(v, 1.0), axis=1).T.astype(BF16)


def _qkprep_call(gq, cos, sin, gq_g, gk_g, score_bias, tm):
    b_, t, w = gq.shape
    n_q = GQ_KV * GQ_G
    return pl.pallas_call(
        _qkprep_kernel,
        grid=(b_, t // tm),
        in_specs=[pl.BlockSpec((1, tm, w), lambda b, i: (b, i, 0)),
                  pl.BlockSpec((tm, 128), lambda b, i: (i, 0)),
                  pl.BlockSpec((tm, 128), lambda b, i: (i, 0)),
                  _resident(gq_g.shape), _resident(gk_g.shape), _resident(score_bias.shape)],
        out_specs=[pl.BlockSpec((1, n_q * 128, tm), lambda b, i: (b, 0, i)),
                   pl.BlockSpec((1, GQ_KV, tm, 128), lambda b, i: (b, 0, i, 0)),
                   pl.BlockSpec((1, GQ_KV * 128, tm), lambda b, i: (b, 0, i))],
        out_shape=[jax.ShapeDtypeStruct((b_, n_q * 128, t), BF16),
                   jax.ShapeDtypeStruct((b_, GQ_KV, t, 128), BF16),
                   jax.ShapeDtypeStruct((b_, GQ_KV * 128, t), BF16)],
        compiler_params=_cparams("parallel", "parallel"),
        name="qkprep",
    )(gq, cos, sin, gq_g, gk_g, score_bias)


def _attn_kernel(qt_ref, k_ref, vt_ref, o_ref, m_scr, acc_scr, *, n_tiles, tk, bounded):
    acc_scr[...] = jnp.zeros(acc_scr.shape, F32)
    if not bounded:
        m_scr[...] = jnp.full(m_scr.shape, NEG_BIG, F32)

    def body(j, carry):
        start = pl.multiple_of(j * tk, tk)
        k_tile = k_ref[0, 0, pl.ds(start, tk), :]
        vt_tile = vt_ref[0, :, pl.ds(start, tk)]
        for g in range(GQ_G):
            st = jnp.dot(k_tile, qt_ref[0, g * 128:(g + 1) * 128, :], preferred_element_type=F32)
            if bounded:
                acc_scr[g] += jnp.dot(vt_tile, jnp.exp2(st).astype(BF16), preferred_element_type=F32)
            else:
                m_prev = m_scr[g]
                m_new = jnp.maximum(m_prev, jnp.max(st, axis=0, keepdims=True))
                p = jnp.exp2(st - m_new).astype(BF16)
                acc_scr[g] = (jnp.exp2(m_prev - m_new) * acc_scr[g]
                              + jnp.dot(vt_tile, p, preferred_element_type=F32))
                m_scr[g] = m_new
        return carry

    lax.fori_loop(0, n_tiles, body, 0)
    outs = []
    for g in range(GQ_G):
        acc = acc_scr[g]
        outs.append(acc[0:HEAD_DIM] / acc[HEAD_DIM:HEAD_DIM + 1])
    o_ref[0] = jnp.concatenate(outs, axis=0).T


def _attn_call(qt, k, vt, logit_bound):
    b_, _, t = qt.shape
    n_keys = k.shape[2]
    tq = min(ATT_TQ, t)
    tk = ATT_TK if n_keys % ATT_TK == 0 else n_keys
    gw = GQ_G * HEAD_DIM

    def call(bounded):
        return pl.pallas_call(
            functools.partial(_attn_kernel, n_tiles=n_keys // tk, tk=tk, bounded=bounded),
            grid=(b_, GQ_KV, t // tq),
            in_specs=[pl.BlockSpec((1, GQ_G * 128, tq), lambda b, kv, i: (b, kv, i)),
                      pl.BlockSpec((1, 1, n_keys, 128), lambda b, kv, i: (b, kv, 0, 0)),
                      pl.BlockSpec((1, 128, n_keys), lambda b, kv, i: (b, kv, 0))],
            out_specs=pl.BlockSpec((1, tq, gw), lambda b, kv, i: (b, i, kv)),
            out_shape=jax.ShapeDtypeStruct((b_, t, GQ_W), F32),
            scratch_shapes=[pltpu.VMEM((GQ_G, 1, tq), F32),
                            pltpu.VMEM((GQ_G, 128, tq), F32)],
            compiler_params=_cparams("parallel", "parallel", "arbitrary"),
            name="attention_bounded" if bounded else "attention_online",
        )(qt, k, vt)

    return lax.cond(logit_bound <= MAX_LOGIT_BOUND, lambda: call(True), lambda: call(False))


def _fourier_factors(t):
    bits = int(round(math.log2(t)))
    assert 1 << bits == t
    n1 = 1 << (bits // 2)
    return n1, t // n1


def _hi_lo(a):
    a = np.asarray(a, np.float64)
    hi = jnp.asarray(a, F32).astype(BF16)
    lo = (jnp.asarray(a, F32) - hi.astype(F32)).astype(BF16)
    return hi, lo


def _fourier_tables(t):
    n1, n2 = _fourier_factors(t)
    c = np.arange(FT_GC)
    ang = 2.0 * np.pi * np.outer(c, c) / FT_GC
    eye = np.eye(FT_GROUPS)
    w0 = np.concatenate([np.kron(eye, np.cos(ang)), -np.kron(eye, np.sin(ang))], axis=1)
    a1 = 2.0 * np.pi * np.outer(np.arange(n1), np.arange(n1)) / n1
    fr, fi = np.cos(a1), -np.sin(a1)
    m1 = np.block([[fr, -fi], [fi, fr]])
    tw = 2.0 * np.pi * np.outer(np.arange(n1), np.arange(n2)) / t
    a2 = 2.0 * np.pi * np.outer(np.arange(n2), np.arange(n2)) / n2
    m2 = np.concatenate([np.cos(a2), np.sin(a2)], axis=1) / math.sqrt(t * FT_GC)
    return dict(n1=n1, n2=n2, w0=_hi_lo(w0), m1=_hi_lo(m1), m2=_hi_lo(m2),
                twr=np.cos(tw).astype(np.float32), twi=(-np.sin(tw)).astype(np.float32))


def _ft_stage1_kernel(u_ref, w0h, w0l, m1h, m1l, twr_ref, twi_ref, yr_ref, yi_ref, *, n_tok):
    n1 = u_ref.shape[1]
    zr, zi = [], []
    for j in range(n_tok):
        uh, ul = _split2(u_ref[0, :, j, :])
        z = _dot3(uh, ul, w0h[...], w0l[...])
        zr.append(z[:, :FT_W])
        zi.append(z[:, FT_W:])
    z = jnp.concatenate([jnp.concatenate(zr, axis=1), jnp.concatenate(zi, axis=1)], axis=0)
    zh, zl = _split2(z)
    y = _dot3(m1h[...], m1l[...], zh, zl)
    y_r, y_i = y[:n1], y[n1:]
    twr, twi = twr_ref[0], twi_ref[0]
    for j in range(n_tok):
        sl = slice(j * FT_W, (j + 1) * FT_W)
        cr, ci = twr[:, j:j + 1], twi[:, j:j + 1]
        yr_ref[0, :, j, :] = y_r[:, sl] * cr - y_i[:, sl] * ci
        yi_ref[0, :, j, :] = y_r[:, sl] * ci + y_i[:, sl] * cr


def _ft_stage2_kernel(yr_ref, yi_ref, m2h, m2l, o_ref, *, n_k1, n2):
    for j in range(n_k1):
        y = jnp.concatenate([yr_ref[0, j * n2:(j + 1) * n2, :], yi_ref[0, j * n2:(j + 1) * n2, :]], axis=0)
        yh, yl = _split2(y)
        o_ref[0, :, j, :] = _dot3(m2h[...], m2l[...], yh, yl)


def _fourier_call(u, tabs):
    b_, t, w = u.shape
    n1, n2 = tabs["n1"], tabs["n2"]
    n_tok = 8
    twr = jnp.asarray(tabs["twr"]).reshape(n1, n2 // n_tok, n_tok).transpose(1, 0, 2)
    twi = jnp.asarray(tabs["twi"]).reshape(n1, n2 // n_tok, n_tok).transpose(1, 0, 2)
    grid_view = jax.ShapeDtypeStruct((b_, n1, n2, w), F32)
    tok_block = pl.BlockSpec((1, n1, n_tok, w), lambda b, j: (b, 0, j, 0))
    yr, yi = pl.pallas_call(
        functools.partial(_ft_stage1_kernel, n_tok=n_tok),
        grid=(b_, n2 // n_tok),
        in_specs=[tok_block,
                  _resident(tabs["w0"][0].shape), _resident(tabs["w0"][1].shape),
                  _resident(tabs["m1"][0].shape), _resident(tabs["m1"][1].shape),
                  pl.BlockSpec((1, n1, n_tok), lambda b, j: (j, 0, 0)),
                  pl.BlockSpec((1, n1, n_tok), lambda b, j: (j, 0, 0))],
        out_specs=[tok_block] * 2,
        out_shape=[grid_view, grid_view],
        compiler_params=_cparams("parallel", "parallel"),
        name="fourier_stage1",
    )(u.reshape(b_, n1, n2, w), *tabs["w0"], *tabs["m1"], twr, twi)
    n_k1 = 8
    out = pl.pallas_call(
        functools.partial(_ft_stage2_kernel, n_k1=n_k1, n2=n2),
        grid=(b_, n1 // n_k1),
        in_specs=[pl.BlockSpec((1, n_k1 * n2, w), lambda b, j: (b, j, 0)),
                  pl.BlockSpec((1, n_k1 * n2, w), lambda b, j: (b, j, 0)),
                  _resident(tabs["m2"][0].shape), _resident(tabs["m2"][1].shape)],
        out_specs=pl.BlockSpec((1, n2, n_k1, w), lambda b, j: (b, 0, j, 0)),
        out_shape=jax.ShapeDtypeStruct((b_, n2, n1, w), F32),
        compiler_params=_cparams("parallel", "parallel"),
        name="fourier_stage2",
    )(yr.reshape(b_, t, w), yi.reshape(b_, t, w), *tabs["m2"])
    return out.reshape(b_, t, w)


def _merge_kernel(x_ref, mod_ref, hf_ref, hb_ref, mlo_ref, ft_ref, att_ref, of_ref, ob_ref, glr_ref,
                  wgate, bgate, wbr, wout, mlg, glg, lng, lnb, o_ref, *, alpha):
    x = x_ref[0]
    h = _modulate(x, mod_ref[0, 0:1, :], mod_ref[0, 1:2, :]).astype(BF16)
    ones = _group_ones(ML_W)
    hs = hf_ref[0] + hb_ref[0]
    hc = hs - _group_mean(hs, ones)
    hn = hc * lax.rsqrt(_group_mean(hc * hc, ones) + LN_EPS) * mlg[...]
    br_ml = hn * _sigmoid(mlo_ref[0])
    os_ = of_ref[0] + ob_ref[0]
    on = os_ * lax.rsqrt(_group_mean(os_ * os_, ones) + LN_EPS) * glg[...]
    br_gl = on * _silu(glr_ref[0])
    branches = (br_ml, ft_ref[0], att_ref[0], br_gl)
    mixed = None
    off = 0
    for j, br in enumerate(branches):
        wd = br.shape[1]
        gate = _sigmoid(jnp.dot(h, wgate[:, j * D_MODEL:(j + 1) * D_MODEL], preferred_element_type=F32)
                        + bgate[:, j * D_MODEL:(j + 1) * D_MODEL])
        u = gate * jnp.dot(br.astype(BF16), wbr[off:off + wd, :], preferred_element_type=F32)
        mixed = u if mixed is None else mixed + u
        off += wd
    y = jnp.dot(mixed.astype(BF16), wout[...], preferred_element_type=F32)
    o_ref[0] = _layernorm(alpha * x + mod_ref[0, 2:3, :] * y) * lng[...] + lnb[...]


def _merge_call(x, mod, hf, hb, ml, ftb, att, of, ob, gl, weights, alpha, tm):
    b_, t, d = x.shape
    tok = lambda w: pl.BlockSpec((1, tm, w), lambda b, i: (b, i, 0))
    last_quarter = pl.BlockSpec((1, tm, ML_W), lambda b, i: (b, i, 3))
    return pl.pallas_call(
        functools.partial(_merge_kernel, alpha=alpha),
        grid=(b_, t // tm),
        in_specs=[tok(d), pl.BlockSpec((1, N_ADA, d), lambda b, i: (b, 0, 0)),
                  tok(ML_W), tok(ML_W), last_quarter, tok(FT_W), tok(GQ_W), tok(GL_W), tok(GL_W),
                  last_quarter] + [_resident(w.shape) for w in weights],
        out_specs=tok(d),
        out_shape=jax.ShapeDtypeStruct((b_, t, d), F32),
        compiler_params=_cparams("parallel", "parallel"),
        name="merge",
    )(x, mod, hf, hb, ml, ftb, att, of, ob, gl, *weights)


def _ffn_kernel(x_ref, mod_ref, wg, wu, wd, lng, lnb, o_ref, *, alpha):
    x = x_ref[0]
    h = _modulate(x, mod_ref[0, 3:4, :], mod_ref[0, 4:5, :]).astype(BF16)
    a = jnp.dot(h, wg[...], preferred_element_type=F32)
    u = jnp.dot(h, wu[...], preferred_element_type=F32)
    f = jnp.dot((_silu(a) * u).astype(BF16), wd[...], preferred_element_type=F32)
    o_ref[0] = _layernorm(alpha * x + mod_ref[0, 5:6, :] * f) * lng[...] + lnb[...]


def _ffn_call(x, mod, weights, alpha, tm):
    b_, t, d = x.shape
    return pl.pallas_call(
        functools.partial(_ffn_kernel, alpha=alpha),
        grid=(b_, t // tm),
        in_specs=[pl.BlockSpec((1, tm, d), lambda b, i: (b, i, 0)),
                  pl.BlockSpec((1, N_ADA, d), lambda b, i: (b, 0, 0))]
                 + [_resident(w.shape) for w in weights],
        out_specs=pl.BlockSpec((1, tm, d), lambda b, i: (b, i, 0)),
        out_shape=jax.ShapeDtypeStruct((b_, t, d), F32),
        compiler_params=_cparams("parallel", "parallel"),
        name="ffn",
    )(x, mod, *weights)


def _moe_kernel(x_ref, mod_ref, wrh, wrl, br_ref, wg, wu, wd, lng, lnb, o_ref, h_scr, dense_scr, acc_scr,
                *, alpha):
    e = pl.program_id(2)

    @pl.when(e == 0)
    def _():
        h = _modulate(x_ref[0], mod_ref[0, 3:4, :], mod_ref[0, 4:5, :])
        h_scr[...] = h.astype(BF16)
        hh, hl = _split2(h)
        logits = _dot3(hh, hl, wrh[...], wrl[...])
        lane = lax.broadcasted_iota(jnp.int32, logits.shape, 1).astype(F32)
        valid = lane < N_EXPERTS
        sel = jnp.where(valid, logits + br_ref[...], -jnp.inf)

        def pick(scores):
            mx = jnp.max(scores, axis=1, keepdims=True)
            idx = jnp.min(jnp.where(scores == mx, lane, 2.0 * SMALL_W), axis=1, keepdims=True)
            return lane == idx

        first = pick(sel)
        second = pick(jnp.where(first, -jnp.inf, sel))
        l1 = jnp.sum(jnp.where(first, logits, 0.0), axis=1, keepdims=True)
        l2 = jnp.sum(jnp.where(second, logits, 0.0), axis=1, keepdims=True)
        mx = jnp.maximum(l1, l2)
        e1, e2 = jnp.exp(l1 - mx), jnp.exp(l2 - mx)
        inv = 1.0 / (e1 + e2)
        dense_scr[...] = jnp.where(first, e1 * inv, 0.0) + jnp.where(second, e2 * inv, 0.0)
        acc_scr[...] = jnp.zeros(acc_scr.shape, F32)

    h = h_scr[...]
    a = jnp.dot(h, wg[0], preferred_element_type=F32)
    u = jnp.dot(h, wu[0], preferred_element_type=F32)
    f = jnp.dot((_silu(a) * u).astype(BF16), wd[0], preferred_element_type=F32)
    dense = dense_scr[...]
    lane = lax.broadcasted_iota(jnp.int32, dense.shape, 1)
    w_e = jnp.sum(jnp.where(lane == e, dense, 0.0), axis=1, keepdims=True)
    acc_scr[...] += w_e * f

    @pl.when(e == pl.num_programs(2) - 1)
    def _():
        o_ref[0] = (_layernorm(alpha * x_ref[0] + mod_ref[0, 5:6, :] * acc_scr[...]) * lng[...]
                    + lnb[...])


def _moe_call(x, mod, wrh, wrl, br, wg, wu, wd, lng, lnb, alpha, tm):
    b_, t, d = x.shape
    n_e, _, ff = wg.shape
    return pl.pallas_call(
        functools.partial(_moe_kernel, alpha=alpha),
        grid=(b_, t // tm, n_e),
        in_specs=[pl.BlockSpec((1, tm, d), lambda b, i, e: (b, i, 0)),
                  pl.BlockSpec((1, N_ADA, d), lambda b, i, e: (b, 0, 0)),
                  _resident(wrh.shape), _resident(wrl.shape), _resident(br.shape),
                  pl.BlockSpec((1, d, ff), lambda b, i, e: (e, 0, 0)),
                  pl.BlockSpec((1, d, ff), lambda b, i, e: (e, 0, 0)),
                  pl.BlockSpec((1, ff, d), lambda b, i, e: (e, 0, 0)),
                  _resident(lng.shape), _resident(lnb.shape)],
        out_specs=pl.BlockSpec((1, tm, d), lambda b, i, e: (b, i, 0)),
        out_shape=jax.ShapeDtypeStruct((b_, t, d), F32),
        scratch_shapes=[pltpu.VMEM((tm, d), BF16), pltpu.VMEM((tm, SMALL_W), F32), pltpu.VMEM((tm, d), F32)],
        compiler_params=_cparams("parallel", "parallel", "arbitrary"),
        name="moe",
    )(x, mod, wrh, wrl, br, wg, wu, wd, lng, lnb)


def _rope_tables(t):
    rows = t // GRID_W
    row = jnp.repeat(jnp.arange(rows, dtype=F32), GRID_W)
    col = jnp.tile(jnp.arange(GRID_W, dtype=F32), rows)
    inv = jnp.power(ROPE_BASE, -jnp.arange(ROPE_PAIRS, dtype=F32) / ROPE_PAIRS)
    ar, ac = row[:, None] * inv, col[:, None] * inv
    cos = jnp.concatenate([jnp.cos(ar), jnp.cos(ar), jnp.cos(ac), jnp.cos(ac)], axis=1)
    sin = jnp.concatenate([-jnp.sin(ar), jnp.sin(ar), -jnp.sin(ac), jnp.sin(ac)], axis=1)
    return jnp.tile(cos, (1, 2)), jnp.tile(sin, (1, 2))


def _token_tile(t, pref):
    return pref if t % pref == 0 else t


def kernel(x, c, ctx, c_ctx, w_ada, b_ada, w_in, ml_gate_b, ml_norm_g, gq_qnorm_g, gq_knorm_g, gl_w2, gl_b2,
           gl_norm_g, w_branch, w_gate, b_gate, w_out, ln1_g, ln1_b, ln2_g, ln2_b, ffd_wg, ffd_wu, ffd_wd,
           moe_wr, moe_br, moe_wg, moe_wu, moe_wd):
    b_, s_len, d = x.shape
    n_ctx = ctx.shape[1]
    depth = w_in.shape[0]
    alpha = (2.0 * depth) ** 0.25
    bf = lambda a: a.astype(BF16)
    row = lambda a: a.reshape(1, -1).astype(F32)

    cos_l, sin_l = _rope_tables(s_len)
    cos_c, sin_c = jnp.ones((n_ctx, 128), F32), jnp.zeros((n_ctx, 128), F32)
    ft_l, ft_c = _fourier_tables(s_len), _fourier_tables(n_ctx)
    tm_l, tm_c = _token_tile(s_len, 512), _token_tile(n_ctx, 512)

    cc = jnp.zeros((8, d), F32).at[0:b_].set(c).at[b_].set(c_ctx)
    xc = ctx
    for l in range(depth):
        last = l == depth - 1
        ada = _ada_call(cc, bf(w_ada[l]), row(b_ada[l])).reshape(8, N_ADA, d)
        mod_l = ada[0:b_]
        mod_c = jnp.broadcast_to(ada[b_][None], (b_, N_ADA, d))

        w = w_in[l]
        o_gate, o_ft, o_gq, o_gl, o_af = 4 * ML_W, 4 * ML_W + 16, 4 * ML_W + 16 + FT_W, \
            4 * ML_W + 16 + FT_W + GQ_W + 2 * GQ_KW, 4 * ML_W + 16 + FT_W + GQ_W + 2 * GQ_KW + 4 * GL_W
        ml_scale = jnp.concatenate([jnp.ones((ML_W,)), jnp.full((ML_W,), HEAD_DIM ** -0.5), jnp.ones((2 * ML_W,))])
        gl_scale = jnp.concatenate([jnp.full((GL_W,), HEAD_DIM ** -0.5), jnp.ones((3 * GL_W,))])
        w_small = jnp.concatenate([w[:, o_gate:o_gate + 16], w[:, o_af:o_af + 2 * GL_RANK],
                                   jnp.zeros((d, SMALL_W - 16 - 2 * GL_RANK), F32)], axis=1)
        ws = [bf(w[:, 0:4 * ML_W] * ml_scale), bf(w[:, o_ft:o_ft + FT_W]), bf(w[:, o_gq:o_gl]),
              bf(w[:, o_gl:o_af] * gl_scale), bf(w_small)]
        gate_bias = jnp.zeros((1, SMALL_W), F32).at[0, 0:16].set(ml_gate_b[l].reshape(-1))
        w2e = jnp.zeros((SMALL_W, 2 * GL_W), F32)
        w2e = w2e.at[16:16 + GL_RANK, 0:GL_W].set(gl_w2[l, 0]).at[16 + GL_RANK:16 + 2 * GL_RANK, GL_W:].set(gl_w2[l, 1])
        merge_w = [bf(w_gate[l]), row(b_gate[l]), bf(w_branch[l]), bf(w_out[l]), row(ml_norm_g[l]),
                   row(gl_norm_g[l]), row(ln1_g[l]), row(ln1_b[l])]
        gq_g = jnp.tile(row(gq_qnorm_g[l]), (1, 2))
        gk_g = jnp.tile(row(gq_knorm_g[l]), (1, 2))
        logit_bound = (1.02 * LOG2_E * HEAD_DIM ** 0.5) * jnp.max(jnp.abs(gq_qnorm_g[l])) * jnp.max(jnp.abs(gq_knorm_g[l]))
        score_bias = jnp.zeros((1, HEAD_DIM), F32).at[0, 0].set(-logit_bound)

        def mixers(xs, mod, tm, cos, sin, states):
            ml, ftu, gq, gl, small = _inproj_call(xs, mod, ws, tm)
            hf, hb, ml_s, ml_m = _mlstm_call(ml, small, gate_bias, states[0], states[1])
            of, ob, gl_s = _gla_call(gl, small, bf(w2e), gl_b2[l].astype(F32), states[2])
            qt, k, vt = _qkprep_call(gq, cos, sin, gq_g, gk_g, score_bias, tm)
            return dict(ml=ml, ftu=ftu, gl=gl, hf=hf, hb=hb, of=of, ob=ob, qt=qt, k=k, vt=vt), (ml_s, ml_m, gl_s)

        zero_states = (jnp.zeros((b_, 2 * ML_HEADS, 2 * HEAD_DIM, 2 * HEAD_DIM), F32),
                       jnp.zeros((b_, 2 * ML_HEADS, 1, SMALL_W), F32),
                       jnp.zeros((b_, 2, GL_W, GL_W), F32))
        pc, ctx_states = mixers(xc, mod_c, tm_c, cos_c, sin_c, zero_states)
        pl_, _ = mixers(x, mod_l, tm_l, cos_l, sin_l, ctx_states)

        att_l = _attn_call(pl_["qt"], jnp.concatenate([pc["k"], pl_["k"]], axis=2),
                           jnp.concatenate([pc["vt"], pl_["vt"]], axis=2), logit_bound)
        x_mid = _merge_call(x, mod_l, pl_["hf"], pl_["hb"], pl_["ml"], _fourier_call(pl_["ftu"], ft_l), att_l,
                            pl_["of"], pl_["ob"], pl_["gl"], merge_w, alpha, tm_l)
        if not last:
            att_c = _attn_call(pc["qt"], pc["k"], pc["vt"], logit_bound)
            xc = _merge_call(xc, mod_c, pc["hf"], pc["hb"], pc["ml"], _fourier_call(pc["ftu"], ft_c), att_c,
                             pc["of"], pc["ob"], pc["gl"], merge_w, alpha, tm_c)

        j = l // 2
        if l % 2 == 0:
            ffn_w = [bf(ffd_wg[j]), bf(ffd_wu[j]), bf(ffd_wd[j]), row(ln2_g[l]), row(ln2_b[l])]
            x = _ffn_call(x_mid, mod_l, ffn_w, alpha, tm_l)
            if not last:
                xc = _ffn_call(xc, mod_c, ffn_w, alpha, tm_c)
        else:
            wr = jnp.zeros((d, SMALL_W), F32).at[:, 0:N_EXPERTS].set(moe_wr[j])
            wrh = bf(wr)
            wrl = bf(wr - wrh.astype(F32))
            brp = jnp.zeros((1, SMALL_W), F32).at[0, 0:N_EXPERTS].set(moe_br[j])
            moe_args = (wrh, wrl, brp, bf(moe_wg[j]), bf(moe_wu[j]), bf(moe_wd[j]), row(ln2_g[l]), row(ln2_b[l]))
            x = _moe_call(x_mid, mod_l, *moe_args, alpha, _token_tile(s_len, 1024))
            if not last:
                xc = _moe_call(xc, mod_c, *moe_args, alpha, tm_c)
    return x
```

```python
import functools
import math

import jax
import jax.numpy as jnp
import numpy as np
from jax import lax
from jax.experimental import pallas as pl
from jax.experimental.pallas import tpu as pltpu

F32 = jnp.float32
BF16 = jnp.bfloat16

D_MODEL = 1024
GRID_W = 64
HEAD_DIM = 64
ML_HEADS = 4
ML_W = ML_HEADS * HEAD_DIM
FT_GROUPS = 4
FT_GC = 64
FT_W = FT_GROUPS * FT_GC
GQ_KV = 2
GQ_G = 4
GQ_W = GQ_KV * GQ_G * HEAD_DIM
GQ_KW = GQ_KV * HEAD_DIM
ROPE_PAIRS = HEAD_DIM // 4
ROPE_BASE = 10000.0
GL_HEADS = 4
GL_W = GL_HEADS * HEAD_DIM
GL_RANK = 16
GL_TAU = 16.0
N_EXPERTS = 8
N_ADA = 6
LN_EPS = 1e-6
SMALL_W = 128

ML_CHUNK = 256
GL_CHUNK = 128
GL_SUB = 16
GL_EXP_CLAMP = 80.0
ATT_TQ = 512
ATT_TK = 1280
ATT_ROWS = 256
LOG2_E = 1.4426950408889634
MAX_LOGIT_BOUND = 60.0
NEG_BIG = -1e30
FT_TOKENS_PER_STEP = 16

VMEM_LIMIT = 56 * 1024 * 1024


def _cparams(*sem):
    return pltpu.CompilerParams(dimension_semantics=sem, vmem_limit_bytes=VMEM_LIMIT)


def _resident(shape):
    nd = len(shape)
    return pl.BlockSpec(shape, lambda *_: (0,) * nd, pipeline_mode=pl.Buffered(1))


def _bdot(a, b):
    return jnp.dot(a.astype(BF16), b.astype(BF16), preferred_element_type=F32)


def _split2(x):
    hi = x.astype(BF16)
    lo = (x - hi.astype(F32)).astype(BF16)
    return hi, lo


def _split3(x):
    a = x.astype(BF16)
    r = x - a.astype(F32)
    b = r.astype(BF16)
    c = (r - b.astype(F32)).astype(BF16)
    return a, b, c


def _dot_exact_rhs(x, m_bf16):
    a, b, c = _split3(x)
    d = functools.partial(jnp.dot, preferred_element_type=F32)
    return d(a, m_bf16) + d(b, m_bf16) + d(c, m_bf16)


def _dot_exact_lhs(m_bf16, x):
    a, b, c = _split3(x)
    d = functools.partial(jnp.dot, preferred_element_type=F32)
    return d(m_bf16, a) + d(m_bf16, b) + d(m_bf16, c)


def _dot3(a_hi, a_lo, b_hi, b_lo):
    d = functools.partial(jnp.dot, preferred_element_type=F32)
    return d(a_hi, b_hi) + d(a_hi, b_lo) + d(a_lo, b_hi)


def _sigmoid(x):
    return 1.0 / (1.0 + jnp.exp(-x))


def _silu(x):
    return x * _sigmoid(x)


def _log_sigmoid(x):
    return jnp.minimum(x, 0.0) - jnp.log(1.0 + jnp.exp(-jnp.abs(x)))


def _layernorm(x):
    mu = jnp.mean(x, axis=-1, keepdims=True)
    xc = x - mu
    var = jnp.mean(xc * xc, axis=-1, keepdims=True)
    return xc * lax.rsqrt(var + LN_EPS)


def _modulate(x, shift, scale):
    return _layernorm(x) * (1.0 + scale) + shift


def _group_ones(width):
    r = lax.broadcasted_iota(jnp.int32, (width, width), 0) >> 6
    c = lax.broadcasted_iota(jnp.int32, (width, width), 1) >> 6
    return jnp.where(r == c, 1.0, 0.0).astype(BF16)


def _group_mean(x, ones):
    return _dot_exact_rhs(x, ones) * (1.0 / HEAD_DIM)


def _tri(n, upper):
    r = lax.broadcasted_iota(jnp.int32, (n, n), 0)
    c = lax.broadcasted_iota(jnp.int32, (n, n), 1)
    keep = (c >= r) if upper else (c <= r)
    return jnp.where(keep, 1.0, 0.0).astype(BF16)


def _ada_kernel(c_ref, w_ref, b_ref, o_ref):
    o_ref[...] = _bdot(_silu(c_ref[...]), w_ref[...]) + b_ref[...]


def _ada_call(cc, w, b):
    rows, d = cc.shape
    n = w.shape[1]
    tn = 1024
    return pl.pallas_call(
        _ada_kernel,
        grid=(n // tn,),
        in_specs=[pl.BlockSpec((rows, d), lambda j: (0, 0)),
                  pl.BlockSpec((d, tn), lambda j: (0, j)),
                  pl.BlockSpec((1, tn), lambda j: (0, j))],
        out_specs=pl.BlockSpec((rows, tn), lambda j: (0, j)),
        out_shape=jax.ShapeDtypeStruct((rows, n), F32),
        compiler_params=_cparams("parallel"),
        name="ada",
    )(cc, w, b)


def _inproj_kernel(x_ref, mod_ref, w_ml, w_ft, w_gq, w_gl, w_sm, o_ml, o_ft, o_gq, o_gl, o_sm):
    h = _modulate(x_ref[0], mod_ref[0, 0:1, :], mod_ref[0, 1:2, :]).astype(BF16)
    for w, o in ((w_ml, o_ml), (w_ft, o_ft), (w_gq, o_gq), (w_gl, o_gl), (w_sm, o_sm)):
        o[0] = jnp.dot(h, w[...], preferred_element_type=F32)


def _inproj_call(x, mod, ws, tm):
    b_, t, d = x.shape
    widths = [w.shape[1] for w in ws]
    return pl.pallas_call(
        _inproj_kernel,
        grid=(b_, t // tm),
        in_specs=[pl.BlockSpec((1, tm, d), lambda b, i: (b, i, 0)),
                  pl.BlockSpec((1, N_ADA, d), lambda b, i: (b, 0, 0))]
                 + [_resident(w.shape) for w in ws],
        out_specs=[pl.BlockSpec((1, tm, n), lambda b, i: (b, i, 0)) for n in widths],
        out_shape=[jax.ShapeDtypeStruct((b_, t, n), F32) for n in widths],
        compiler_params=_cparams("parallel", "parallel"),
        name="inproj",
    )(x, mod, *ws)


def _mlstm_kernel(qkv_f, sm_f, qkv_b, sm_b, bias_ref, s0_ref, m0_ref,
                  hf_ref, hb_ref, st_ref, mt_ref, s_scr, m_scr, *, chunk):
    i = pl.program_id(1)
    n_l = chunk

    @pl.when(i == 0)
    def _():
        s_scr[...] = s0_ref[0]
        m_scr[...] = m0_ref[0]

    row = lax.broadcasted_iota(jnp.int32, (n_l, n_l), 0)
    col = lax.broadcasted_iota(jnp.int32, (n_l, n_l), 1)
    lane = lax.broadcasted_iota(jnp.int32, (n_l, 128), 1)
    sub = lax.broadcasted_iota(jnp.int32, (HEAD_DIM, n_l), 0)
    ones_row = jnp.where(sub == 0, 1.0, 0.0)
    for d, (qkv_ref, sm_ref, h_ref) in enumerate(((qkv_f, sm_f, hf_ref), (qkv_b, sm_b, hb_ref))):
        rev = d == 1
        blk = qkv_ref[0]
        pre = sm_ref[0] + bias_ref[...]
        bcum = _dot_exact_lhs(_tri(n_l, upper=rev), _log_sigmoid(pre))
        pre_t = pre.T
        b_t = bcum.T
        q_t = blk[:, 0:ML_W].T.astype(BF16)
        v_t = blk[:, 2 * ML_W:3 * ML_W].T
        mask = (row >= col) if rev else (row <= col)
        last = 0 if rev else n_l - 1
        outs = []
        for h in range(ML_HEADS):
            ci = 8 * d + h
            cf = 8 * d + 4 + h
            idx = 4 * d + h
            pair = h // 2
            own = (lane >= HEAD_DIM) if h % 2 else (lane < HEAD_DIM)
            k_own = jnp.where(own, blk[:, ML_W + pair * 128:ML_W + (pair + 1) * 128], 0.0).astype(BF16)
            q_pair = q_t[pair * 128:(pair + 1) * 128, :]
            v_h = v_t[h * HEAD_DIM:(h + 1) * HEAD_DIM, :]
            c_col = pre[:, ci:ci + 1] - bcum[:, cf:cf + 1]
            b_row = b_t[cf:cf + 1, :]
            i_row = pre_t[ci:ci + 1, :]
            state = s_scr[idx]
            m_prev = m_scr[idx][:, 0:1]

            dmat = jnp.where(mask, b_row + c_col, -jnp.inf)
            inter = b_row + m_prev
            m_t = jnp.maximum(inter, jnp.max(dmat, axis=0, keepdims=True))
            w_intra = jnp.exp(dmat - m_t) * jnp.dot(k_own, q_pair, preferred_element_type=F32)
            w_inter = jnp.exp(inter - m_t)
            sq = jnp.dot(state.astype(BF16), q_pair, preferred_element_type=F32)
            num = (jnp.dot(v_h.astype(BF16), w_intra.astype(BF16), preferred_element_type=F32)
                   + w_inter * sq[0:HEAD_DIM])
            den = jnp.sum(w_intra, axis=0, keepdims=True) + w_inter * sq[HEAD_DIM:HEAD_DIM + 1]
            outs.append(num / jnp.maximum(jnp.abs(den), jnp.exp(-m_t)))

            b_last = b_row[:, last:last + 1]
            g_row = b_last - b_row + i_row
            m_new = jnp.maximum(b_last + m_prev, jnp.max(g_row, axis=1, keepdims=True))
            ws = jnp.exp(g_row - m_new)
            wc = jnp.exp(b_last + m_prev - m_new)
            v_ext = jnp.concatenate([v_h, ones_row], axis=0)
            s_scr[idx] = wc * state + jnp.dot((v_ext * ws).astype(BF16), k_own, preferred_element_type=F32)
            m_scr[idx] = jnp.broadcast_to(m_new, (1, SMALL_W))
        h_ref[0] = jnp.concatenate(outs, axis=0).T

    @pl.when(i == pl.num_programs(1) - 1)
    def _():
        st_ref[0] = s_scr[...]
        mt_ref[0] = m_scr[...]


def _mlstm_call(ml, small, bias, s0, m0):
    b_, t, _ = ml.shape
    chunk = min(ML_CHUNK, t)
    n = t // chunk
    fwd = lambda b, i: (b, i, 0)
    bwd = lambda b, i: (b, n - 1 - i, 0)
    state_spec = pl.BlockSpec((1, 2 * ML_HEADS, 2 * HEAD_DIM, 2 * HEAD_DIM), lambda b, i: (b, 0, 0, 0))
    m_spec = pl.BlockSpec((1, 2 * ML_HEADS, 1, SMALL_W), lambda b, i: (b, 0, 0, 0))
    return pl.pallas_call(
        functools.partial(_mlstm_kernel, chunk=chunk),
        grid=(b_, n),
        in_specs=[pl.BlockSpec((1, chunk, 3 * ML_W), fwd), pl.BlockSpec((1, chunk, SMALL_W), fwd),
                  pl.BlockSpec((1, chunk, 3 * ML_W), bwd), pl.BlockSpec((1, chunk, SMALL_W), bwd),
                  pl.BlockSpec((1, SMALL_W), lambda b, i: (0, 0)), state_spec, m_spec],
        out_specs=[pl.BlockSpec((1, chunk, ML_W), fwd), pl.BlockSpec((1, chunk, ML_W), bwd),
                   state_spec, m_spec],
        out_shape=[jax.ShapeDtypeStruct((b_, t, ML_W), F32), jax.ShapeDtypeStruct((b_, t, ML_W), F32),
                   jax.ShapeDtypeStruct(s0.shape, F32), jax.ShapeDtypeStruct(m0.shape, F32)],
        scratch_shapes=[pltpu.VMEM((2 * ML_HEADS, 2 * HEAD_DIM, 2 * HEAD_DIM), F32),
                        pltpu.VMEM((2 * ML_HEADS, 1, SMALL_W), F32)],
        compiler_params=_cparams("parallel", "arbitrary"),
        name="mlstm",
    )(ml, small, ml, small, bias, s0, m0)


def _gla_kernel(qkv_f, sm_f, qkv_b, sm_b, w2_ref, b2_ref, s0_ref,
                of_ref, ob_ref, st_ref, s_scr, *, chunk):
    i = pl.program_id(1)
    n_l = chunk
    n_sub = n_l // GL_SUB
    width = GL_W

    @pl.when(i == 0)
    def _():
        s_scr[...] = s0_ref[0]

    lane = lax.broadcasted_iota(jnp.int32, (GL_SUB, width), 1) >> 6
    head_masks = [lane == h for h in range(GL_HEADS)]
    chunk_lane = lax.broadcasted_iota(jnp.int32, (n_l, width), 1) >> 6
    chunk_masks = [chunk_lane == h for h in range(GL_HEADS)]
    bd_mask = ((lax.broadcasted_iota(jnp.int32, (width, width), 0) >> 6)
               == (lax.broadcasted_iota(jnp.int32, (width, width), 1) >> 6))
    states = [s_scr[0], s_scr[1]]
    new_states, new_outs = [], []
    for d, (qkv_ref, sm_ref, o_ref) in enumerate(((qkv_f, sm_f, of_ref), (qkv_b, sm_b, ob_ref))):
        rev = d == 1
        blk = qkv_ref[0]
        q = blk[:, 0:width]
        k = blk[:, width:2 * width]
        v = blk[:, 2 * width:3 * width]
        a = _bdot(sm_ref[0], w2_ref[...])[:, d * width:(d + 1) * width] + b2_ref[d:d + 1, :]
        la = _log_sigmoid(a) * (1.0 / GL_TAU)
        g = _dot_exact_lhs(_tri(n_l, upper=rev), la)
        last = 0 if rev else n_l - 1
        g_end = g[last:last + 1, :]
        state = states[d]
        o_inter = lax.dot_general((q * jnp.exp(g)).astype(BF16), state.astype(BF16),
                                  (((1,), (1,)), ((), ())), preferred_element_type=F32)
        v_bf = v.astype(BF16)
        a_blocks = []
        t_idx = lax.broadcasted_iota(jnp.int32, (GL_HEADS * GL_SUB, n_l), 0) & (GL_SUB - 1)
        s_idx = lax.broadcasted_iota(jnp.int32, (GL_HEADS * GL_SUB, n_l), 1)
        for s in range(n_sub):
            lo = s * GL_SUB
            hi = lo + GL_SUB
            r = g[hi - 1:hi, :] if rev else g[lo:lo + 1, :]
            qt = q[lo:hi] * jnp.exp(g[lo:hi] - r)
            kt = k * jnp.exp(jnp.minimum(r - g, GL_EXP_CLAMP))
            qstack = jnp.concatenate([jnp.where(hm, qt, 0.0) for hm in head_masks], axis=0)
            amat = lax.dot_general(qstack.astype(BF16), kt.astype(BF16),
                                   (((1,), (1,)), ((), ())), preferred_element_type=F32)
            keep = (s_idx >= lo + t_idx) if rev else (s_idx <= lo + t_idx)
            a_blocks.append(jnp.where(keep, amat, 0.0).astype(BF16))
        a_all = jnp.concatenate([a_blocks[s][h * GL_SUB:(h + 1) * GL_SUB]
                                 for h in range(GL_HEADS) for s in range(n_sub)], axis=0)
        ov = jnp.dot(a_all, v_bf, preferred_element_type=F32)
        o_intra = jnp.where(chunk_masks[0], ov[0:n_l], 0.0)
        for h in range(1, GL_HEADS):
            o_intra = o_intra + jnp.where(chunk_masks[h], ov[h * n_l:(h + 1) * n_l], 0.0)
        new_outs.append(o_inter + o_intra)

        kg = k * jnp.exp(g_end - g)
        upd = jnp.dot(v.T.astype(BF16), kg.astype(BF16), preferred_element_type=F32)
        new_states.append(jnp.exp(g_end) * state + jnp.where(bd_mask, upd, 0.0))
    of_ref[0] = new_outs[0]
    ob_ref[0] = new_outs[1]
    s_scr[0] = new_states[0]
    s_scr[1] = new_states[1]

    @pl.when(i == pl.num_programs(1) - 1)
    def _():
        st_ref[0] = s_scr[...]


def _gla_call(gl, small, w2e, b2, s0):
    b_, t, _ = gl.shape
    chunk = min(GL_CHUNK, t)
    n = t // chunk
    fwd = lambda b, i: (b, i, 0)
    bwd = lambda b, i: (b, n - 1 - i, 0)
    state_spec = pl.BlockSpec((1, 2, GL_W, GL_W), lambda b, i: (b, 0, 0, 0))
    return pl.pallas_call(
        functools.partial(_gla_kernel, chunk=chunk),
        grid=(b_, n),
        in_specs=[pl.BlockSpec((1, chunk, 3 * GL_W), fwd), pl.BlockSpec((1, chunk, SMALL_W), fwd),
                  pl.BlockSpec((1, chunk, 3 * GL_W), bwd), pl.BlockSpec((1, chunk, SMALL_W), bwd),
                  _resident(w2e.shape), _resident(b2.shape), state_spec],
        out_specs=[pl.BlockSpec((1, chunk, GL_W), fwd), pl.BlockSpec((1, chunk, GL_W), bwd), state_spec],
        out_shape=[jax.ShapeDtypeStruct((b_, t, GL_W), F32), jax.ShapeDtypeStruct((b_, t, GL_W), F32),
                   jax.ShapeDtypeStruct(s0.shape, F32)],
        scratch_shapes=[pltpu.VMEM((2, GL_W, GL_W), F32)],
        compiler_params=_cparams("parallel", "arbitrary"),
        name="gla",
    )(gl, small, gl, small, w2e, b2, s0)


def _qkprep_kernel(gq_ref, cos_ref, sin_ref, gq_g, gk_g, sb_ref, qt_out, k_out, vt_out):
    x = gq_ref[0]
    tm = x.shape[0]

    def norm_rope(z, g, reps):
        width = z.shape[1]
        msq = _group_mean(z * z, _group_ones(width))
        zn = z * lax.rsqrt(msq + LN_EPS) * jnp.tile(g, (1, reps))
        lane = lax.broadcasted_iota(jnp.int32, zn.shape, 1)
        partner = jnp.where((lane & 31) < ROPE_PAIRS,
                            pltpu.roll(zn, width - ROPE_PAIRS, axis=1),
                            pltpu.roll(zn, ROPE_PAIRS, axis=1))
        return zn * jnp.tile(cos_ref[...], (1, reps)) + partner * jnp.tile(sin_ref[...], (1, reps))

    lane = lax.broadcasted_iota(jnp.int32, (tm, 128), 1)

    def pad_heads(z, extra):
        out = []
        for p in range(z.shape[1] // 128):
            pair = z[:, p * 128:(p + 1) * 128]
            for base in (pair, pltpu.roll(pair, HEAD_DIM, axis=1)):
                out.append(jnp.where(lane < HEAD_DIM, base, jnp.where(lane == HEAD_DIM, extra, 0.0)))
        return out

    q = norm_rope(x[:, 0:GQ_W], gq_g[...], GQ_W // 128) * (LOG2_E * HEAD_DIM ** -0.5)
    qt_out[0] = jnp.concatenate(pad_heads(q, 1.0), axis=1).T.astype(BF16)
    k = norm_rope(x[:, GQ_W:GQ_W + GQ_KW], gk_g[...], GQ_KW // 128)
    for j, kj in enumerate(pad_heads(k, sb_ref[0:1, 0:1])):
        k_out[0, j] = kj.astype(BF16)
    v = x[:, GQ_W + GQ_KW:GQ_W + 2 * GQ_KW]
    vt_out[0] = v.T.astype(BF16)


def _qkprep_call(gq, cos, sin, gq_g, gk_g, score_bias, tm):
    b_, t, w = gq.shape
    n_q = GQ_KV * GQ_G
    return pl.pallas_call(
        _qkprep_kernel,
        grid=(b_, t // tm),
        in_specs=[pl.BlockSpec((1, tm, w), lambda b, i: (b, i, 0)),
                  pl.BlockSpec((tm, 128), lambda b, i: (i, 0)),
                  pl.BlockSpec((tm, 128), lambda b, i: (i, 0)),
                  _resident(gq_g.shape), _resident(gk_g.shape), _resident(score_bias.shape)],
        out_specs=[pl.BlockSpec((1, n_q * 128, tm), lambda b, i: (b, 0, i)),
                   pl.BlockSpec((1, GQ_KV, tm, 128), lambda b, i: (b, 0, i, 0)),
                   pl.BlockSpec((1, GQ_KW, tm), lambda b, i: (b, 0, i))],
        out_shape=[jax.ShapeDtypeStruct((b_, n_q * 128, t), BF16),
                   jax.ShapeDtypeStruct((b_, GQ_KV, t, 128), BF16),
                   jax.ShapeDtypeStruct((b_, GQ_KW, t), BF16)],
        compiler_params=_cparams("parallel", "parallel"),
        name="qkprep",
    )(gq, cos, sin, gq_g, gk_g, score_bias)


def _attn_kernel(qt_ref, k_ref, vt_ref, o_ref, m_scr, sum_scr, acc_scr, *, n_tiles, tk, bounded):
    tq = qt_ref.shape[2]
    acc_scr[...] = jnp.zeros(acc_scr.shape, F32)
    sum_scr[...] = jnp.zeros(sum_scr.shape, F32)
    if not bounded:
        m_scr[...] = jnp.full(m_scr.shape, NEG_BIG, F32)

    def body(j, carry):
        start = pl.multiple_of(j * tk, tk)
        k_tile = k_ref[0, 0, pl.ds(start, tk), :]
        vt_tile = vt_ref[0, :, pl.ds(start, tk)]
        for g in range(GQ_G):
            st = jnp.dot(k_tile, qt_ref[0, g * 128:(g + 1) * 128, :], preferred_element_type=F32)
            if bounded:
                p = jnp.exp2(st)
                sum_scr[g] += jnp.sum(p.reshape(tk // 8, 8, tq), axis=0)
                acc_scr[g] += jnp.dot(vt_tile, p.astype(BF16), preferred_element_type=F32)
            else:
                m_prev = m_scr[g]
                m_new = jnp.maximum(m_prev, jnp.max(st, axis=0, keepdims=True))
                p = jnp.exp2(st - m_new)
                scale = jnp.exp2(m_prev - m_new)
                sum_scr[g] = scale * sum_scr[g] + jnp.sum(p.reshape(tk // 8, 8, tq), axis=0)
                acc_scr[g] = scale * acc_scr[g] + jnp.dot(vt_tile, p.astype(BF16), preferred_element_type=F32)
                m_scr[g] = m_new
        return carry

    lax.fori_loop(0, n_tiles, body, 0)
    outs = [acc_scr[g] / jnp.sum(sum_scr[g], axis=0, keepdims=True) for g in range(GQ_G)]
    o_ref[0] = jnp.concatenate(outs, axis=0).T


def _attn_call(qt, k, vt, logit_bound):
    b_, _, t = qt.shape
    n_keys = k.shape[2]
    tq = min(ATT_TQ, t)
    tk = ATT_TK if n_keys % ATT_TK == 0 else n_keys
    gw = GQ_G * HEAD_DIM

    def call(bounded):
        return pl.pallas_call(
            functools.partial(_attn_kernel, n_tiles=n_keys // tk, tk=tk, bounded=bounded),
            grid=(b_, GQ_KV, t // tq),
            in_specs=[pl.BlockSpec((1, GQ_G * 128, tq), lambda b, kv, i: (b, kv, i)),
                      pl.BlockSpec((1, 1, n_keys, 128), lambda b, kv, i: (b, kv, 0, 0)),
                      pl.BlockSpec((1, HEAD_DIM, n_keys), lambda b, kv, i: (b, kv, 0))],
            out_specs=pl.BlockSpec((1, tq, gw), lambda b, kv, i: (b, i, kv)),
            out_shape=jax.ShapeDtypeStruct((b_, t, GQ_W), F32),
            scratch_shapes=[pltpu.VMEM((GQ_G, 1, tq), F32),
                            pltpu.VMEM((GQ_G, 8, tq), F32),
                            pltpu.VMEM((GQ_G, HEAD_DIM, tq), F32)],
            compiler_params=_cparams("parallel", "parallel", "arbitrary"),
            name="attention_bounded" if bounded else "attention_online",
        )(qt, k, vt)

    return lax.cond(logit_bound <= MAX_LOGIT_BOUND, lambda: call(True), lambda: call(False))


def _fourier_factors(t):
    bits = int(round(math.log2(t)))
    assert 1 << bits == t
    n1 = 1 << (bits // 2)
    return n1, t // n1


def _hi_lo(a):
    a = np.asarray(a, np.float64)
    hi = jnp.asarray(a, F32).astype(BF16)
    lo = (jnp.asarray(a, F32) - hi.astype(F32)).astype(BF16)
    return hi, lo


def _fourier_tables(t):
    n1, n2 = _fourier_factors(t)
    c = np.arange(FT_GC)
    ang = 2.0 * np.pi * np.outer(c, c) / FT_GC
    eye = np.eye(FT_GROUPS)
    w0 = np.concatenate([np.kron(eye, np.cos(ang)), -np.kron(eye, np.sin(ang))], axis=1)
    a1 = 2.0 * np.pi * np.outer(np.arange(n1), np.arange(n1)) / n1
    fr, fi = np.cos(a1), -np.sin(a1)
    m1 = np.block([[fr, -fi], [fi, fr]])
    tw = 2.0 * np.pi * np.outer(np.arange(n1), np.arange(n2)) / t
    a2 = 2.0 * np.pi * np.outer(np.arange(n2), np.arange(n2)) / n2
    m2 = np.concatenate([np.cos(a2), np.sin(a2)], axis=1) / math.sqrt(t * FT_GC)
    return dict(n1=n1, n2=n2, w0=_hi_lo(w0), m1=_hi_lo(m1), m2=_hi_lo(m2),
                twr=np.cos(tw).astype(np.float32), twi=(-np.sin(tw)).astype(np.float32))


def _ft_stage1_kernel(u_ref, w0h, w0l, m1h, m1l, twr_ref, twi_ref, yr_ref, yi_ref, *, n_tok):
    n1 = u_ref.shape[1]
    zr, zi = [], []
    for j in range(n_tok):
        uh, ul = _split2(u_ref[0, :, j, :])
        z = _dot3(uh, ul, w0h[...], w0l[...])
        zr.append(z[:, :FT_W])
        zi.append(z[:, FT_W:])
    z = jnp.concatenate([jnp.concatenate(zr, axis=1), jnp.concatenate(zi, axis=1)], axis=0)
    zh, zl = _split2(z)
    y = _dot3(m1h[...], m1l[...], zh, zl)
    y_r, y_i = y[:n1], y[n1:]
    twr, twi = twr_ref[0], twi_ref[0]
    for j in range(n_tok):
        sl = slice(j * FT_W, (j + 1) * FT_W)
        cr, ci = twr[:, j:j + 1], twi[:, j:j + 1]
        yr_ref[0, :, j, :] = y_r[:, sl] * cr - y_i[:, sl] * ci
        yi_ref[0, :, j, :] = y_r[:, sl] * ci + y_i[:, sl] * cr


def _ft_stage2_kernel(yr_ref, yi_ref, m2h, m2l, o_ref, *, n_k1, n2):
    for j in range(n_k1):
        y = jnp.concatenate([yr_ref[0, j * n2:(j + 1) * n2, :], yi_ref[0, j * n2:(j + 1) * n2, :]], axis=0)
        yh, yl = _split2(y)
        o_ref[0, :, j, :] = _dot3(m2h[...], m2l[...], yh, yl)


def _fourier_call(u, tabs):
    b_, t, w = u.shape
    n1, n2 = tabs["n1"], tabs["n2"]
    n_tok = min(FT_TOKENS_PER_STEP, n2)
    twr = jnp.asarray(tabs["twr"]).reshape(n1, n2 // n_tok, n_tok).transpose(1, 0, 2)
    twi = jnp.asarray(tabs["twi"]).reshape(n1, n2 // n_tok, n_tok).transpose(1, 0, 2)
    grid_view = jax.ShapeDtypeStruct((b_, n1, n2, w), F32)
    tok_block = pl.BlockSpec((1, n1, n_tok, w), lambda b, j: (b, 0, j, 0))
    yr, yi = pl.pallas_call(
        functools.partial(_ft_stage1_kernel, n_tok=n_tok),
        grid=(b_, n2 // n_tok),
        in_specs=[tok_block,
                  _resident(tabs["w0"][0].shape), _resident(tabs["w0"][1].shape),
                  _resident(tabs["m1"][0].shape), _resident(tabs["m1"][1].shape),
                  pl.BlockSpec((1, n1, n_tok), lambda b, j: (j, 0, 0)),
                  pl.BlockSpec((1, n1, n_tok), lambda b, j: (j, 0, 0))],
        out_specs=[tok_block] * 2,
        out_shape=[grid_view, grid_view],
        compiler_params=_cparams("parallel", "parallel"),
        name="fourier_stage1",
    )(u.reshape(b_, n1, n2, w), *tabs["w0"], *tabs["m1"], twr, twi)
    n_k1 = min(FT_TOKENS_PER_STEP, n1)
    out = pl.pallas_call(
        functools.partial(_ft_stage2_kernel, n_k1=n_k1, n2=n2),
        grid=(b_, n1 // n_k1),
        in_specs=[pl.BlockSpec((1, n_k1 * n2, w), lambda b, j: (b, j, 0)),
                  pl.BlockSpec((1, n_k1 * n2, w), lambda b, j: (b, j, 0)),
                  _resident(tabs["m2"][0].shape), _resident(tabs["m2"][1].shape)],
        out_specs=pl.BlockSpec((1, n2, n_k1, w), lambda b, j: (b, 0, j, 0)),
        out_shape=jax.ShapeDtypeStruct((b_, n2, n1, w), F32),
        compiler_params=_cparams("parallel", "parallel"),
        name="fourier_stage2",
    )(yr.reshape(b_, t, w), yi.reshape(b_, t, w), *tabs["m2"])
    return out.reshape(b_, t, w)


def _merge_kernel(x_ref, mod_ref, hf_ref, hb_ref, mlo_ref, ft_ref, att_ref, of_ref, ob_ref, glr_ref,
                  wgate, bgate, wbr, wout, mlg, glg, lng, lnb, o_ref, *, alpha):
    x = x_ref[0]
    h = _modulate(x, mod_ref[0, 0:1, :], mod_ref[0, 1:2, :]).astype(BF16)
    ones = _group_ones(ML_W)
    hs = hf_ref[0] + hb_ref[0]
    hc = hs - _group_mean(hs, ones)
    hn = hc * lax.rsqrt(_group_mean(hc * hc, ones) + LN_EPS) * mlg[...]
    br_ml = hn * _sigmoid(mlo_ref[0])
    os_ = of_ref[0] + ob_ref[0]
    on = os_ * lax.rsqrt(_group_mean(os_ * os_, ones) + LN_EPS) * glg[...]
    br_gl = on * _silu(glr_ref[0])
    branches = (br_ml, ft_ref[0], att_ref[0], br_gl)
    mixed = None
    off = 0
    for j, br in enumerate(branches):
        wd = br.shape[1]
        gate = _sigmoid(jnp.dot(h, wgate[:, j * D_MODEL:(j + 1) * D_MODEL], preferred_element_type=F32)
                        + bgate[:, j * D_MODEL:(j + 1) * D_MODEL])
        u = gate * jnp.dot(br.astype(BF16), wbr[off:off + wd, :], preferred_element_type=F32)
        mixed = u if mixed is None else mixed + u
        off += wd
    y = jnp.dot(mixed.astype(BF16), wout[...], preferred_element_type=F32)
    o_ref[0] = _layernorm(alpha * x + mod_ref[0, 2:3, :] * y) * lng[...] + lnb[...]


def _merge_call(x, mod, hf, hb, ml, ftb, att, of, ob, gl, weights, alpha, tm):
    b_, t, d = x.shape
    tok = lambda w: pl.BlockSpec((1, tm, w), lambda b, i: (b, i, 0))
    last_quarter = pl.BlockSpec((1, tm, ML_W), lambda b, i: (b, i, 3))
    return pl.pallas_call(
        functools.partial(_merge_kernel, alpha=alpha),
        grid=(b_, t // tm),
        in_specs=[tok(d), pl.BlockSpec((1, N_ADA, d), lambda b, i: (b, 0, 0)),
                  tok(ML_W), tok(ML_W), last_quarter, tok(FT_W), tok(GQ_W), tok(GL_W), tok(GL_W),
                  last_quarter] + [_resident(w.shape) for w in weights],
        out_specs=tok(d),
        out_shape=jax.ShapeDtypeStruct((b_, t, d), F32),
        compiler_params=_cparams("parallel", "parallel"),
        name="merge",
    )(x, mod, hf, hb, ml, ftb, att, of, ob, gl, *weights)


def _ffn_kernel(x_ref, mod_ref, wg, wu, wd, lng, lnb, o_ref, *, alpha):
    x = x_ref[0]
    h = _modulate(x, mod_ref[0, 3:4, :], mod_ref[0, 4:5, :]).astype(BF16)
    a = jnp.dot(h, wg[...], preferred_element_type=F32)
    u = jnp.dot(h, wu[...], preferred_element_type=F32)
    f = jnp.dot((_silu(a) * u).astype(BF16), wd[...], preferred_element_type=F32)
    o_ref[0] = _layernorm(alpha * x + mod_ref[0, 5:6, :] * f) * lng[...] + lnb[...]


def _ffn_call(x, mod, weights, alpha, tm):
    b_, t, d = x.shape
    return pl.pallas_call(
        functools.partial(_ffn_kernel, alpha=alpha),
        grid=(b_, t // tm),
        in_specs=[pl.BlockSpec((1, tm, d), lambda b, i: (b, i, 0)),
                  pl.BlockSpec((1, N_ADA, d), lambda b, i: (b, 0, 0))]
                 + [_resident(w.shape) for w in weights],
        out_specs=pl.BlockSpec((1, tm, d), lambda b, i: (b, i, 0)),
        out_shape=jax.ShapeDtypeStruct((b_, t, d), F32),
        compiler_params=_cparams("parallel", "parallel"),
        name="ffn",
    )(x, mod, *weights)


def _moe_kernel(x_ref, mod_ref, wrh, wrl, br_ref, wg, wu, wd, lng, lnb, o_ref, h_scr, dense_scr, acc_scr,
                *, alpha):
    e = pl.program_id(2)

    @pl.when(e == 0)
    def _():
        h = _modulate(x_ref[0], mod_ref[0, 3:4, :], mod_ref[0, 4:5, :])
        h_scr[...] = h.astype(BF16)
        hh, hl = _split2(h)
        logits = _dot3(hh, hl, wrh[...], wrl[...])
        lane = lax.broadcasted_iota(jnp.int32, logits.shape, 1).astype(F32)
        valid = lane < N_EXPERTS
        sel = jnp.where(valid, logits + br_ref[...], -jnp.inf)

        def pick(scores):
            mx = jnp.max(scores, axis=1, keepdims=True)
            idx = jnp.min(jnp.where(scores == mx, lane, 2.0 * SMALL_W), axis=1, keepdims=True)
            return lane == idx

        first = pick(sel)
        second = pick(jnp.where(first, -jnp.inf, sel))
        l1 = jnp.sum(jnp.where(first, logits, 0.0), axis=1, keepdims=True)
        l2 = jnp.sum(jnp.where(second, logits, 0.0), axis=1, keepdims=True)
        mx = jnp.maximum(l1, l2)
        e1, e2 = jnp.exp(l1 - mx), jnp.exp(l2 - mx)
        inv = 1.0 / (e1 + e2)
        dense_scr[...] = jnp.where(first, e1 * inv, 0.0) + jnp.where(second, e2 * inv, 0.0)
        acc_scr[...] = jnp.zeros(acc_scr.shape, F32)

    h = h_scr[...]
    a = jnp.dot(h, wg[0], preferred_element_type=F32)
    u = jnp.dot(h, wu[0], preferred_element_type=F32)
    f = jnp.dot((_silu(a) * u).astype(BF16), wd[0], preferred_element_type=F32)
    dense = dense_scr[...]
    lane = lax.broadcasted_iota(jnp.int32, dense.shape, 1)
    w_e = jnp.sum(jnp.where(lane == e, dense, 0.0), axis=1, keepdims=True)
    acc_scr[...] += w_e * f

    @pl.when(e == pl.num_programs(2) - 1)
    def _():
        o_ref[0] = (_layernorm(alpha * x_ref[0] + mod_ref[0, 5:6, :] * acc_scr[...]) * lng[...]
                    + lnb[...])


def _moe_call(x, mod, wrh, wrl, br, wg, wu, wd, lng, lnb, alpha, tm):
    b_, t, d = x.shape
    n_e, _, ff = wg.shape
    return pl.pallas_call(
        functools.partial(_moe_kernel, alpha=alpha),
        grid=(b_, t // tm, n_e),
        in_specs=[pl.BlockSpec((1, tm, d), lambda b, i, e: (b, i, 0)),
                  pl.BlockSpec((1, N_ADA, d), lambda b, i, e: (b, 0, 0)),
                  _resident(wrh.shape), _resident(wrl.shape), _resident(br.shape),
                  pl.BlockSpec((1, d, ff), lambda b, i, e: (e, 0, 0)),
                  pl.BlockSpec((1, d, ff), lambda b, i, e: (e, 0, 0)),
                  pl.BlockSpec((1, ff, d), lambda b, i, e: (e, 0, 0)),
                  _resident(lng.shape), _resident(lnb.shape)],
        out_specs=pl.BlockSpec((1, tm, d), lambda b, i, e: (b, i, 0)),
        out_shape=jax.ShapeDtypeStruct((b_, t, d), F32),
        scratch_shapes=[pltpu.VMEM((tm, d), BF16), pltpu.VMEM((tm, SMALL_W), F32), pltpu.VMEM((tm, d), F32)],
        compiler_params=_cparams("parallel", "parallel", "arbitrary"),
        name="moe",
    )(x, mod, wrh, wrl, br, wg, wu, wd, lng, lnb)


def _rope_tables(t):
    rows = t // GRID_W
    row = jnp.repeat(jnp.arange(rows, dtype=F32), GRID_W)
    col = jnp.tile(jnp.arange(GRID_W, dtype=F32), rows)
    inv = jnp.power(ROPE_BASE, -jnp.arange(ROPE_PAIRS, dtype=F32) / ROPE_PAIRS)
    ar, ac = row[:, None] * inv, col[:, None] * inv
    cos = jnp.concatenate([jnp.cos(ar), jnp.cos(ar), jnp.cos(ac), jnp.cos(ac)], axis=1)
    sin = jnp.concatenate([-jnp.sin(ar), jnp.sin(ar), -jnp.sin(ac), jnp.sin(ac)], axis=1)
    return jnp.tile(cos, (1, 2)), jnp.tile(sin, (1, 2))


def _token_tile(t, pref):
    return pref if t % pref == 0 else t


def kernel(x, c, ctx, c_ctx, w_ada, b_ada, w_in, ml_gate_b, ml_norm_g, gq_qnorm_g, gq_knorm_g, gl_w2, gl_b2,
           gl_norm_g, w_branch, w_gate, b_gate, w_out, ln1_g, ln1_b, ln2_g, ln2_b, ffd_wg, ffd_wu, ffd_wd,
           moe_wr, moe_br, moe_wg, moe_wu, moe_wd):
    b_, s_len, d = x.shape
    n_ctx = ctx.shape[1]
    depth = w_in.shape[0]
    alpha = (2.0 * depth) ** 0.25
    bf = lambda a: a.astype(BF16)
    row = lambda a: a.reshape(1, -1).astype(F32)

    cos_l, sin_l = _rope_tables(s_len)
    cos_c, sin_c = jnp.ones((n_ctx, 128), F32), jnp.zeros((n_ctx, 128), F32)
    ft_l, ft_c = _fourier_tables(s_len), _fourier_tables(n_ctx)
    tm_l, tm_c = _token_tile(s_len, 512), _token_tile(n_ctx, 512)

    cc = jnp.zeros((8, d), F32).at[0:b_].set(c).at[b_].set(c_ctx)
    xc = ctx
    for l in range(depth):
        last = l == depth - 1
        ada = _ada_call(cc, bf(w_ada[l]), row(b_ada[l])).reshape(8, N_ADA, d)
        mod_l = ada[0:b_]
        mod_c = jnp.broadcast_to(ada[b_][None], (b_, N_ADA, d))

        w = w_in[l]
        o_gate, o_ft, o_gq, o_gl, o_af = 4 * ML_W, 4 * ML_W + 16, 4 * ML_W + 16 + FT_W, \
            4 * ML_W + 16 + FT_W + GQ_W + 2 * GQ_KW, 4 * ML_W + 16 + FT_W + GQ_W + 2 * GQ_KW + 4 * GL_W
        ml_scale = jnp.concatenate([jnp.ones((ML_W,)), jnp.full((ML_W,), HEAD_DIM ** -0.5), jnp.ones((2 * ML_W,))])
        gl_scale = jnp.concatenate([jnp.full((GL_W,), HEAD_DIM ** -0.5), jnp.ones((3 * GL_W,))])
        w_small = jnp.concatenate([w[:, o_gate:o_gate + 16], w[:, o_af:o_af + 2 * GL_RANK],
                                   jnp.zeros((d, SMALL_W - 16 - 2 * GL_RANK), F32)], axis=1)
        ws = [bf(w[:, 0:4 * ML_W] * ml_scale), bf(w[:, o_ft:o_ft + FT_W]), bf(w[:, o_gq:o_gl]),
              bf(w[:, o_gl:o_af] * gl_scale), bf(w_small)]
        gate_bias = jnp.zeros((1, SMALL_W), F32).at[0, 0:16].set(ml_gate_b[l].reshape(-1))
        w2e = jnp.zeros((SMALL_W, 2 * GL_W), F32)
        w2e = w2e.at[16:16 + GL_RANK, 0:GL_W].set(gl_w2[l, 0]).at[16 + GL_RANK:16 + 2 * GL_RANK, GL_W:].set(gl_w2[l, 1])
        merge_w = [bf(w_gate[l]), row(b_gate[l]), bf(w_branch[l]), bf(w_out[l]), row(ml_norm_g[l]),
                   row(gl_norm_g[l]), row(ln1_g[l]), row(ln1_b[l])]
        gq_g = jnp.tile(row(gq_qnorm_g[l]), (1, 2))
        gk_g = jnp.tile(row(gq_knorm_g[l]), (1, 2))
        logit_bound = (1.02 * LOG2_E * HEAD_DIM ** 0.5) * jnp.max(jnp.abs(gq_qnorm_g[l])) * jnp.max(jnp.abs(gq_knorm_g[l]))
        score_bias = jnp.zeros((1, HEAD_DIM), F32).at[0, 0].set(-logit_bound)

        def mixers(xs, mod, tm, cos, sin, states):
            ml, ftu, gq, gl, small = _inproj_call(xs, mod, ws, tm)
            hf, hb, ml_s, ml_m = _mlstm_call(ml, small, gate_bias, states[0], states[1])
            of, ob, gl_s = _gla_call(gl, small, bf(w2e), gl_b2[l].astype(F32), states[2])
            qt, k, vt = _qkprep_call(gq, cos, sin, gq_g, gk_g, score_bias, tm)
            return dict(ml=ml, ftu=ftu, gl=gl, hf=hf, hb=hb, of=of, ob=ob, qt=qt, k=k, vt=vt), (ml_s, ml_m, gl_s)

        zero_states = (jnp.zeros((b_, 2 * ML_HEADS, 2 * HEAD_DIM, 2 * HEAD_DIM), F32),
                       jnp.zeros((b_, 2 * ML_HEADS, 1, SMALL_W), F32),
                       jnp.zeros((b_, 2, GL_W, GL_W), F32))
        pc, ctx_states = mixers(xc, mod_c, tm_c, cos_c, sin_c, zero_states)
        pl_, _ = mixers(x, mod_l, tm_l, cos_l, sin_l, ctx_states)

        att_l = _attn_call(pl_["qt"], jnp.concatenate([pc["k"], pl_["k"]], axis=2),
                           jnp.concatenate([pc["vt"], pl_["vt"]], axis=2), logit_bound)
        x_mid = _merge_call(x, mod_l, pl_["hf"], pl_["hb"], pl_["ml"], _fourier_call(pl_["ftu"], ft_l), att_l,
                            pl_["of"], pl_["ob"], pl_["gl"], merge_w, alpha, tm_l)
        if not last:
            att_c = _attn_call(pc["qt"], pc["k"], pc["vt"], logit_bound)
            xc = _merge_call(xc, mod_c, pc["hf"], pc["hb"], pc["ml"], _fourier_call(pc["ftu"], ft_c), att_c,
                             pc["of"], pc["ob"], pc["gl"], merge_w, alpha, tm_c)

        j = l // 2
        if l % 2 == 0:
            ffn_w = [bf(ffd_wg[j]), bf(ffd_wu[j]), bf(ffd_wd[j]), row(ln2_g[l]), row(ln2_b[l])]
            x = _ffn_call(x_mid, mod_l, ffn_w, alpha, tm_l)
            if not last:
                xc = _ffn_call(xc, mod_c, ffn_w, alpha, tm_c)
        else:
            wr = jnp.zeros((d, SMALL_W), F32).at[:, 0:N_EXPERTS].set(moe_wr[j])
            wrh = bf(wr)
            wrl = bf(wr - wrh.astype(F32))
            brp = jnp.zeros((1, SMALL_W), F32).at[0, 0:N_EXPERTS].set(moe_br[j])
            moe_args = (wrh, wrl, brp, bf(moe_wg[j]), bf(moe_wu[j]), bf(moe_wd[j]), row(ln2_g[l]), row(ln2_b[l]))
            x = _moe_call(x_mid, mod_l, *moe_args, alpha, _token_tile(s_len, 1024))
            if not last:
                xc = _moe_call(xc, mod_c, *moe_args, alpha, tm_c)
    return x
```

```python
import functools
import math

import jax
import jax.numpy as jnp
import numpy as np
from jax import lax
from jax.experimental import pallas as pl
from jax.experimental.pallas import tpu as pltpu

F32 = jnp.float32
BF16 = jnp.bfloat16

D_MODEL = 1024
GRID_W = 64
HEAD_DIM = 64
ML_HEADS = 4
ML_W = ML_HEADS * HEAD_DIM
FT_GROUPS = 4
FT_GC = 64
FT_W = FT_GROUPS * FT_GC
GQ_KV = 2
GQ_G = 4
GQ_W = GQ_KV * GQ_G * HEAD_DIM
GQ_KW = GQ_KV * HEAD_DIM
ROPE_PAIRS = HEAD_DIM // 4
ROPE_BASE = 10000.0
GL_HEADS = 4
GL_W = GL_HEADS * HEAD_DIM
GL_RANK = 16
GL_TAU = 16.0
N_EXPERTS = 8
N_ADA = 6
LN_EPS = 1e-6
SMALL_W = 128

ML_CHUNK = 256
GL_CHUNK = 128
GL_SUB = 16
GL_EXP_CLAMP = 80.0
ATT_TQ = 512
ATT_TK = 1280
ATT_ROWS = 256
LOG2_E = 1.4426950408889634
MAX_LOGIT_BOUND = 60.0
NEG_BIG = -1e30
MOE_ROWS = 256
FT_TOKENS_PER_STEP = 8
V_ROWS = 80

VMEM_LIMIT = 56 * 1024 * 1024


def _cparams(*sem):
    return pltpu.CompilerParams(dimension_semantics=sem, vmem_limit_bytes=VMEM_LIMIT)


def _resident(shape):
    nd = len(shape)
    return pl.BlockSpec(shape, lambda *_: (0,) * nd, pipeline_mode=pl.Buffered(1))


def _bdot(a, b):
    return jnp.dot(a.astype(BF16), b.astype(BF16), preferred_element_type=F32)


def _split2(x):
    hi = x.astype(BF16)
    lo = (x - hi.astype(F32)).astype(BF16)
    return hi, lo


def _split3(x):
    a = x.astype(BF16)
    r = x - a.astype(F32)
    b = r.astype(BF16)
    c = (r - b.astype(F32)).astype(BF16)
    return a, b, c


def _dot_exact_rhs(x, m_bf16):
    a, b, c = _split3(x)
    d = functools.partial(jnp.dot, preferred_element_type=F32)
    return d(a, m_bf16) + d(b, m_bf16) + d(c, m_bf16)


def _dot_exact_lhs(m_bf16, x):
    a, b, c = _split3(x)
    d = functools.partial(jnp.dot, preferred_element_type=F32)
    return d(m_bf16, a) + d(m_bf16, b) + d(m_bf16, c)


def _dot3(a_hi, a_lo, b_hi, b_lo):
    d = functools.partial(jnp.dot, preferred_element_type=F32)
    return d(a_hi, b_hi) + d(a_hi, b_lo) + d(a_lo, b_hi)


def _sigmoid(x):
    return 1.0 / (1.0 + jnp.exp(-x))


def _silu(x):
    return x * _sigmoid(x)


def _log_sigmoid(x):
    return jnp.minimum(x, 0.0) - jnp.log(1.0 + jnp.exp(-jnp.abs(x)))


def _layernorm(x):
    mu = jnp.mean(x, axis=-1, keepdims=True)
    xc = x - mu
    var = jnp.mean(xc * xc, axis=-1, keepdims=True)
    return xc * lax.rsqrt(var + LN_EPS)


def _modulate(x, shift, scale):
    return _layernorm(x) * (1.0 + scale) + shift


def _group_ones(width):
    r = lax.broadcasted_iota(jnp.int32, (width, width), 0) >> 6
    c = lax.broadcasted_iota(jnp.int32, (width, width), 1) >> 6
    return jnp.where(r == c, 1.0, 0.0).astype(BF16)


def _group_mean(x, ones):
    return _dot_exact_rhs(x, ones) * (1.0 / HEAD_DIM)


def _tri(n, upper):
    r = lax.broadcasted_iota(jnp.int32, (n, n), 0)
    c = lax.broadcasted_iota(jnp.int32, (n, n), 1)
    keep = (c >= r) if upper else (c <= r)
    return jnp.where(keep, 1.0, 0.0).astype(BF16)


def _ada_kernel(c_ref, w_ref, b_ref, o_ref):
    o_ref[...] = _bdot(_silu(c_ref[...]), w_ref[...]) + b_ref[...]


def _ada_call(cc, w, b):
    rows, d = cc.shape
    n = w.shape[1]
    tn = 1024
    return pl.pallas_call(
        _ada_kernel,
        grid=(n // tn,),
        in_specs=[pl.BlockSpec((rows, d), lambda j: (0, 0)),
                  pl.BlockSpec((d, tn), lambda j: (0, j)),
                  pl.BlockSpec((1, tn), lambda j: (0, j))],
        out_specs=pl.BlockSpec((rows, tn), lambda j: (0, j)),
        out_shape=jax.ShapeDtypeStruct((rows, n), F32),
        compiler_params=_cparams("parallel"),
        name="ada",
    )(cc, w, b)


def _inproj_kernel(x_ref, mod_ref, w_ml, w_ft, w_gq, w_gl, w_sm, o_ml, o_ft, o_gq, o_gl, o_sm):
    h = _modulate(x_ref[0], mod_ref[0, 0:1, :], mod_ref[0, 1:2, :]).astype(BF16)
    for w, o in ((w_ml, o_ml), (w_ft, o_ft), (w_gq, o_gq), (w_gl, o_gl), (w_sm, o_sm)):
        o[0] = jnp.dot(h, w[...], preferred_element_type=F32)


def _inproj_call(x, mod, ws, tm):
    b_, t, d = x.shape
    widths = [w.shape[1] for w in ws]
    return pl.pallas_call(
        _inproj_kernel,
        grid=(b_, t // tm),
        in_specs=[pl.BlockSpec((1, tm, d), lambda b, i: (b, i, 0)),
                  pl.BlockSpec((1, N_ADA, d), lambda b, i: (b, 0, 0))]
                 + [_resident(w.shape) for w in ws],
        out_specs=[pl.BlockSpec((1, tm, n), lambda b, i: (b, i, 0)) for n in widths],
        out_shape=[jax.ShapeDtypeStruct((b_, t, n), F32) for n in widths],
        compiler_params=_cparams("parallel", "parallel"),
        name="inproj",
    )(x, mod, *ws)


def _mlstm_kernel(qkv_f, sm_f, qkv_b, sm_b, bias_ref, s0_ref, m0_ref,
                  hf_ref, hb_ref, st_ref, mt_ref, s_scr, m_scr, *, chunk):
    i = pl.program_id(1)
    n_l = chunk

    @pl.when(i == 0)
    def _():
        s_scr[...] = s0_ref[0]
        m_scr[...] = m0_ref[0]

    row = lax.broadcasted_iota(jnp.int32, (n_l, n_l), 0)
    col = lax.broadcasted_iota(jnp.int32, (n_l, n_l), 1)
    lane = lax.broadcasted_iota(jnp.int32, (n_l, 128), 1)
    sub = lax.broadcasted_iota(jnp.int32, (HEAD_DIM, n_l), 0)
    ones_row = jnp.where(sub == 0, 1.0, 0.0)
    for d, (qkv_ref, sm_ref, h_ref) in enumerate(((qkv_f, sm_f, hf_ref), (qkv_b, sm_b, hb_ref))):
        rev = d == 1
        blk = qkv_ref[0]
        pre = sm_ref[0] + bias_ref[...]
        bcum = _dot_exact_lhs(_tri(n_l, upper=rev), _log_sigmoid(pre))
        pre_t = pre.T
        b_t = bcum.T
        q_t = blk[:, 0:ML_W].T.astype(BF16)
        v_t = blk[:, 2 * ML_W:3 * ML_W].T
        mask = (row >= col) if rev else (row <= col)
        last = 0 if rev else n_l - 1
        outs = []
        for h in range(ML_HEADS):
            ci = 8 * d + h
            cf = 8 * d + 4 + h
            idx = 4 * d + h
            pair = h // 2
            own = (lane >= HEAD_DIM) if h % 2 else (lane < HEAD_DIM)
            k_own = jnp.where(own, blk[:, ML_W + pair * 128:ML_W + (pair + 1) * 128], 0.0).astype(BF16)
            q_pair = q_t[pair * 128:(pair + 1) * 128, :]
            v_h = v_t[h * HEAD_DIM:(h + 1) * HEAD_DIM, :]
            c_col = pre[:, ci:ci + 1] - bcum[:, cf:cf + 1]
            b_row = b_t[cf:cf + 1, :]
            i_row = pre_t[ci:ci + 1, :]
            state = s_scr[idx]
            m_prev = m_scr[idx][:, 0:1]

            dmat = jnp.where(mask, b_row + c_col, -jnp.inf)
            inter = b_row + m_prev
            m_t = jnp.maximum(inter, jnp.max(dmat, axis=0, keepdims=True))
            w_intra = jnp.exp(dmat - m_t) * jnp.dot(k_own, q_pair, preferred_element_type=F32)
            w_inter = jnp.exp(inter - m_t)
            sq = jnp.dot(state.astype(BF16), q_pair, preferred_element_type=F32)
            num = (jnp.dot(v_h.astype(BF16), w_intra.astype(BF16), preferred_element_type=F32)
                   + w_inter * sq[0:HEAD_DIM])
            den = jnp.sum(w_intra, axis=0, keepdims=True) + w_inter * sq[HEAD_DIM:HEAD_DIM + 1]
            outs.append(num / jnp.maximum(jnp.abs(den), jnp.exp(-m_t)))

            b_last = b_row[:, last:last + 1]
            g_row = b_last - b_row + i_row
            m_new = jnp.maximum(b_last + m_prev, jnp.max(g_row, axis=1, keepdims=True))
            ws = jnp.exp(g_row - m_new)
            wc = jnp.exp(b_last + m_prev - m_new)
            v_ext = jnp.concatenate([v_h, ones_row], axis=0)
            s_scr[idx] = wc * state + jnp.dot((v_ext * ws).astype(BF16), k_own, preferred_element_type=F32)
            m_scr[idx] = jnp.broadcast_to(m_new, (1, SMALL_W))
        h_ref[0] = jnp.concatenate(outs, axis=0).T

    @pl.when(i == pl.num_programs(1) - 1)
    def _():
        st_ref[0] = s_scr[...]
        mt_ref[0] = m_scr[...]


def _mlstm_call(ml, small, bias, s0, m0):
    b_, t, _ = ml.shape
    chunk = min(ML_CHUNK, t)
    n = t // chunk
    fwd = lambda b, i: (b, i, 0)
    bwd = lambda b, i: (b, n - 1 - i, 0)
    state_spec = pl.BlockSpec((1, 2 * ML_HEADS, 2 * HEAD_DIM, 2 * HEAD_DIM), lambda b, i: (b, 0, 0, 0))
    m_spec = pl.BlockSpec((1, 2 * ML_HEADS, 1, SMALL_W), lambda b, i: (b, 0, 0, 0))
    return pl.pallas_call(
        functools.partial(_mlstm_kernel, chunk=chunk),
        grid=(b_, n),
        in_specs=[pl.BlockSpec((1, chunk, 3 * ML_W), fwd), pl.BlockSpec((1, chunk, SMALL_W), fwd),
                  pl.BlockSpec((1, chunk, 3 * ML_W), bwd), pl.BlockSpec((1, chunk, SMALL_W), bwd),
                  pl.BlockSpec((1, SMALL_W), lambda b, i: (0, 0)), state_spec, m_spec],
        out_specs=[pl.BlockSpec((1, chunk, ML_W), fwd), pl.BlockSpec((1, chunk, ML_W), bwd),
                   state_spec, m_spec],
        out_shape=[jax.ShapeDtypeStruct((b_, t, ML_W), F32), jax.ShapeDtypeStruct((b_, t, ML_W), F32),
                   jax.ShapeDtypeStruct(s0.shape, F32), jax.ShapeDtypeStruct(m0.shape, F32)],
        scratch_shapes=[pltpu.VMEM((2 * ML_HEADS, 2 * HEAD_DIM, 2 * HEAD_DIM), F32),
                        pltpu.VMEM((2 * ML_HEADS, 1, SMALL_W), F32)],
        compiler_params=_cparams("parallel", "arbitrary"),
        name="mlstm",
    )(ml, small, ml, small, bias, s0, m0)


def _gla_kernel(qkv_f, sm_f, qkv_b, sm_b, w2_ref, b2_ref, s0_ref,
                of_ref, ob_ref, st_ref, s_scr, *, chunk):
    i = pl.program_id(1)
    n_l = chunk
    n_sub = n_l // GL_SUB
    width = GL_W

    @pl.when(i == 0)
    def _():
        s_scr[...] = s0_ref[0]

    lane = lax.broadcasted_iota(jnp.int32, (GL_SUB, width), 1) >> 6
    head_masks = [lane == h for h in range(GL_HEADS)]
    chunk_lane = lax.broadcasted_iota(jnp.int32, (n_l, width), 1) >> 6
    chunk_masks = [chunk_lane == h for h in range(GL_HEADS)]
    bd_mask = ((lax.broadcasted_iota(jnp.int32, (width, width), 0) >> 6)
               == (lax.broadcasted_iota(jnp.int32, (width, width), 1) >> 6))
    states = [s_scr[0], s_scr[1]]
    new_states, new_outs = [], []
    for d, (qkv_ref, sm_ref, o_ref) in enumerate(((qkv_f, sm_f, of_ref), (qkv_b, sm_b, ob_ref))):
        rev = d == 1
        blk = qkv_ref[0]
        q = blk[:, 0:width]
        k = blk[:, width:2 * width]
        v = blk[:, 2 * width:3 * width]
        a = _bdot(sm_ref[0], w2_ref[...])[:, d * width:(d + 1) * width] + b2_ref[d:d + 1, :]
        la = _log_sigmoid(a) * (1.0 / GL_TAU)
        g = _dot_exact_lhs(_tri(n_l, upper=rev), la)
        last = 0 if rev else n_l - 1
        g_end = g[last:last + 1, :]
        state = states[d]
        o_inter = lax.dot_general((q * jnp.exp(g)).astype(BF16), state.astype(BF16),
                                  (((1,), (1,)), ((), ())), preferred_element_type=F32)
        v_bf = v.astype(BF16)
        a_blocks = []
        t_idx = lax.broadcasted_iota(jnp.int32, (GL_HEADS * GL_SUB, n_l), 0) & (GL_SUB - 1)
        s_idx = lax.broadcasted_iota(jnp.int32, (GL_HEADS * GL_SUB, n_l), 1)
        for s in range(n_sub):
            lo = s * GL_SUB
            hi = lo + GL_SUB
            r = g[hi - 1:hi, :] if rev else g[lo:lo + 1, :]
            qt = q[lo:hi] * jnp.exp(g[lo:hi] - r)
            kt = k * jnp.exp(jnp.minimum(r - g, GL_EXP_CLAMP))
            qstack = jnp.concatenate([jnp.where(hm, qt, 0.0) for hm in head_masks], axis=0)
            amat = lax.dot_general(qstack.astype(BF16), kt.astype(BF16),
                                   (((1,), (1,)), ((), ())), preferred_element_type=F32)
            keep = (s_idx >= lo + t_idx) if rev else (s_idx <= lo + t_idx)
            a_blocks.append(jnp.where(keep, amat, 0.0).astype(BF16))
        a_all = jnp.concatenate([a_blocks[s][h * GL_SUB:(h + 1) * GL_SUB]
                                 for h in range(GL_HEADS) for s in range(n_sub)], axis=0)
        ov = jnp.dot(a_all, v_bf, preferred_element_type=F32)
        o_intra = jnp.where(chunk_masks[0], ov[0:n_l], 0.0)
        for h in range(1, GL_HEADS):
            o_intra = o_intra + jnp.where(chunk_masks[h], ov[h * n_l:(h + 1) * n_l], 0.0)
        new_outs.append(o_inter + o_intra)

        kg = k * jnp.exp(g_end - g)
        upd = jnp.dot(v.T.astype(BF16), kg.astype(BF16), preferred_element_type=F32)
        new_states.append(jnp.exp(g_end) * state + jnp.where(bd_mask, upd, 0.0))
    of_ref[0] = new_outs[0]
    ob_ref[0] = new_outs[1]
    s_scr[0] = new_states[0]
    s_scr[1] = new_states[1]

    @pl.when(i == pl.num_programs(1) - 1)
    def _():
        st_ref[0] = s_scr[...]


def _gla_call(gl, small, w2e, b2, s0):
    b_, t, _ = gl.shape
    chunk = min(GL_CHUNK, t)
    n = t // chunk
    fwd = lambda b, i: (b, i, 0)
    bwd = lambda b, i: (b, n - 1 - i, 0)
    state_spec = pl.BlockSpec((1, 2, GL_W, GL_W), lambda b, i: (b, 0, 0, 0))
    return pl.pallas_call(
        functools.partial(_gla_kernel, chunk=chunk),
        grid=(b_, n),
        in_specs=[pl.BlockSpec((1, chunk, 3 * GL_W), fwd), pl.BlockSpec((1, chunk, SMALL_W), fwd),
                  pl.BlockSpec((1, chunk, 3 * GL_W), bwd), pl.BlockSpec((1, chunk, SMALL_W), bwd),
                  _resident(w2e.shape), _resident(b2.shape), state_spec],
        out_specs=[pl.BlockSpec((1, chunk, GL_W), fwd), pl.BlockSpec((1, chunk, GL_W), bwd), state_spec],
        out_shape=[jax.ShapeDtypeStruct((b_, t, GL_W), F32), jax.ShapeDtypeStruct((b_, t, GL_W), F32),
                   jax.ShapeDtypeStruct(s0.shape, F32)],
        scratch_shapes=[pltpu.VMEM((2, GL_W, GL_W), F32)],
        compiler_params=_cparams("parallel", "arbitrary"),
        name="gla",
    )(gl, small, gl, small, w2e, b2, s0)


def _qkprep_kernel(gq_ref, cos_ref, sin_ref, gq_g, gk_g, sb_ref, qt_out, k_out, vt_out):
    x = gq_ref[0]
    tm = x.shape[0]

    def norm_rope(z, g, reps):
        width = z.shape[1]
        msq = _group_mean(z * z, _group_ones(width))
        zn = z * lax.rsqrt(msq + LN_EPS) * jnp.tile(g, (1, reps))
        lane = lax.broadcasted_iota(jnp.int32, zn.shape, 1)
        partner = jnp.where((lane & 31) < ROPE_PAIRS,
                            pltpu.roll(zn, width - ROPE_PAIRS, axis=1),
                            pltpu.roll(zn, ROPE_PAIRS, axis=1))
        return zn * jnp.tile(cos_ref[...], (1, reps)) + partner * jnp.tile(sin_ref[...], (1, reps))

    lane = lax.broadcasted_iota(jnp.int32, (tm, 128), 1)

    def pad_heads(z, extra):
        out = []
        for p in range(z.shape[1] // 128):
            pair = z[:, p * 128:(p + 1) * 128]
            for base in (pair, pltpu.roll(pair, HEAD_DIM, axis=1)):
                out.append(jnp.where(lane < HEAD_DIM, base, jnp.where(lane == HEAD_DIM, extra, 0.0)))
        return out

    q = norm_rope(x[:, 0:GQ_W], gq_g[...], GQ_W // 128) * (LOG2_E * HEAD_DIM ** -0.5)
    qt_out[0] = jnp.concatenate(pad_heads(q, 1.0), axis=1).T.astype(BF16)
    k = norm_rope(x[:, GQ_W:GQ_W + GQ_KW], gk_g[...], GQ_KW // 128)
    for j, kj in enumerate(pad_heads(k, sb_ref[0:1, 0:1])):
        k_out[0, j] = kj.astype(BF16)
    v = x[:, GQ_W + GQ_KW:GQ_W + 2 * GQ_KW]
    v_t = v.T
    sub = lax.broadcasted_iota(jnp.int32, (V_ROWS - HEAD_DIM, tm), 0)
    ones_rows = jnp.where(sub == 0, 1.0, 0.0)
    vt_out[0] = jnp.concatenate([v_t[0:HEAD_DIM], ones_rows, v_t[HEAD_DIM:2 * HEAD_DIM], ones_rows],
                                axis=0).astype(BF16)


def _qkprep_call(gq, cos, sin, gq_g, gk_g, score_bias, tm):
    b_, t, w = gq.shape
    n_q = GQ_KV * GQ_G
    return pl.pallas_call(
        _qkprep_kernel,
        grid=(b_, t // tm),
        in_specs=[pl.BlockSpec((1, tm, w), lambda b, i: (b, i, 0)),
                  pl.BlockSpec((tm, 128), lambda b, i: (i, 0)),
                  pl.BlockSpec((tm, 128), lambda b, i: (i, 0)),
                  _resident(gq_g.shape), _resident(gk_g.shape), _resident(score_bias.shape)],
        out_specs=[pl.BlockSpec((1, n_q * 128, tm), lambda b, i: (b, 0, i)),
                   pl.BlockSpec((1, GQ_KV, tm, 128), lambda b, i: (b, 0, i, 0)),
                   pl.BlockSpec((1, GQ_KV * V_ROWS, tm), lambda b, i: (b, 0, i))],
        out_shape=[jax.ShapeDtypeStruct((b_, n_q * 128, t), BF16),
                   jax.ShapeDtypeStruct((b_, GQ_KV, t, 128), BF16),
                   jax.ShapeDtypeStruct((b_, GQ_KV * V_ROWS, t), BF16)],
        compiler_params=_cparams("parallel", "parallel"),
        name="qkprep",
    )(gq, cos, sin, gq_g, gk_g, score_bias)


def _attn_kernel(qt_ref, k_ref, vt_ref, o_ref, m_scr, acc_scr, *, n_tiles, tk, bounded):
    acc_scr[...] = jnp.zeros(acc_scr.shape, F32)
    if not bounded:
        m_scr[...] = jnp.full(m_scr.shape, NEG_BIG, F32)

    def body(j, carry):
        start = pl.multiple_of(j * tk, tk)
        k_tile = k_ref[0, 0, pl.ds(start, tk), :]
        vt_tile = vt_ref[0, :, pl.ds(start, tk)]
        for g in range(GQ_G):
            st = jnp.dot(k_tile, qt_ref[0, g * 128:(g + 1) * 128, :], preferred_element_type=F32)
            if bounded:
                acc_scr[g] += jnp.dot(vt_tile, jnp.exp2(st).astype(BF16), preferred_element_type=F32)
            else:
                m_prev = m_scr[g]
                m_new = jnp.maximum(m_prev, jnp.max(st, axis=0, keepdims=True))
                p = jnp.exp2(st - m_new).astype(BF16)
                acc_scr[g] = (jnp.exp2(m_prev - m_new) * acc_scr[g]
                              + jnp.dot(vt_tile, p, preferred_element_type=F32))
                m_scr[g] = m_new
        return carry

    lax.fori_loop(0, n_tiles, body, 0)
    outs = []
    for g in range(GQ_G):
        acc = acc_scr[g]
        outs.append(acc[0:HEAD_DIM] / acc[HEAD_DIM:HEAD_DIM + 1])
    o_ref[0] = jnp.concatenate(outs, axis=0).T


def _attn_call(qt, k, vt, logit_bound):
    b_, _, t = qt.shape
    n_keys = k.shape[2]
    tq = min(ATT_TQ, t)
    tk = ATT_TK if n_keys % ATT_TK == 0 else n_keys
    gw = GQ_G * HEAD_DIM

    def call(bounded):
        return pl.pallas_call(
            functools.partial(_attn_kernel, n_tiles=n_keys // tk, tk=tk, bounded=bounded),
            grid=(b_, GQ_KV, t // tq),
            in_specs=[pl.BlockSpec((1, GQ_G * 128, tq), lambda b, kv, i: (b, kv, i)),
                      pl.BlockSpec((1, 1, n_keys, 128), lambda b, kv, i: (b, kv, 0, 0)),
                      pl.BlockSpec((1, V_ROWS, n_keys), lambda b, kv, i: (b, kv, 0))],
            out_specs=pl.BlockSpec((1, tq, gw), lambda b, kv, i: (b, i, kv)),
            out_shape=jax.ShapeDtypeStruct((b_, t, GQ_W), F32),
            scratch_shapes=[pltpu.VMEM((GQ_G, 1, tq), F32),
                            pltpu.VMEM((GQ_G, V_ROWS, tq), F32)],
            compiler_params=_cparams("parallel", "parallel", "arbitrary"),
            name="attention_bounded" if bounded else "attention_online",
        )(qt, k, vt)

    return lax.cond(logit_bound <= MAX_LOGIT_BOUND, lambda: call(True), lambda: call(False))


def _fourier_factors(t):
    bits = int(round(math.log2(t)))
    assert 1 << bits == t
    n1 = 1 << (bits // 2)
    return n1, t // n1


def _hi_lo(a):
    a = np.asarray(a, np.float64)
    hi = jnp.asarray(a, F32).astype(BF16)
    lo = (jnp.asarray(a, F32) - hi.astype(F32)).astype(BF16)
    return hi, lo


def _fourier_tables(t):
    n1, n2 = _fourier_factors(t)
    c = np.arange(FT_GC)
    ang = 2.0 * np.pi * np.outer(c, c) / FT_GC
    eye = np.eye(FT_GROUPS)
    w0 = np.concatenate([np.kron(eye, np.cos(ang)), -np.kron(eye, np.sin(ang))], axis=1)
    a1 = 2.0 * np.pi * np.outer(np.arange(n1), np.arange(n1)) / n1
    fr, fi = np.cos(a1), -np.sin(a1)
    m1 = np.block([[fr, -fi], [fi, fr]])
    tw = 2.0 * np.pi * np.outer(np.arange(n1), np.arange(n2)) / t
    a2 = 2.0 * np.pi * np.outer(np.arange(n2), np.arange(n2)) / n2
    m2 = np.concatenate([np.cos(a2), np.sin(a2)], axis=1) / math.sqrt(t * FT_GC)
    return dict(n1=n1, n2=n2, w0=_hi_lo(w0), m1=_hi_lo(m1), m2=_hi_lo(m2),
                twr=np.cos(tw).astype(np.float32), twi=(-np.sin(tw)).astype(np.float32))


def _ft_stage1_kernel(u_ref, w0h, w0l, m1h, m1l, twr_ref, twi_ref, yr_ref, yi_ref, *, n_tok):
    n1 = u_ref.shape[1]
    zr, zi = [], []
    for j in range(n_tok):
        uh, ul = _split2(u_ref[0, :, j, :])
        z = _dot3(uh, ul, w0h[...], w0l[...])
        zr.append(z[:, :FT_W])
        zi.append(z[:, FT_W:])
    z = jnp.concatenate([jnp.concatenate(zr, axis=1), jnp.concatenate(zi, axis=1)], axis=0)
    zh, zl = _split2(z)
    y = _dot3(m1h[...], m1l[...], zh, zl)
    y_r, y_i = y[:n1], y[n1:]
    twr, twi = twr_ref[0], twi_ref[0]
    for j in range(n_tok):
        sl = slice(j * FT_W, (j + 1) * FT_W)
        cr, ci = twr[:, j:j + 1], twi[:, j:j + 1]
        yr_ref[0, :, j, :] = y_r[:, sl] * cr - y_i[:, sl] * ci
        yi_ref[0, :, j, :] = y_r[:, sl] * ci + y_i[:, sl] * cr


def _ft_stage2_kernel(yr_ref, yi_ref, m2h, m2l, o_ref, *, n_k1, n2):
    for j in range(n_k1):
        y = jnp.concatenate([yr_ref[0, j * n2:(j + 1) * n2, :], yi_ref[0, j * n2:(j + 1) * n2, :]], axis=0)
        yh, yl = _split2(y)
        o_ref[0, :, j, :] = _dot3(m2h[...], m2l[...], yh, yl)


def _fourier_call(u, tabs):
    b_, t, w = u.shape
    n1, n2 = tabs["n1"], tabs["n2"]
    n_tok = min(FT_TOKENS_PER_STEP, n2)
    twr = jnp.asarray(tabs["twr"]).reshape(n1, n2 // n_tok, n_tok).transpose(1, 0, 2)
    twi = jnp.asarray(tabs["twi"]).reshape(n1, n2 // n_tok, n_tok).transpose(1, 0, 2)
    grid_view = jax.ShapeDtypeStruct((b_, n1, n2, w), F32)
    tok_block = pl.BlockSpec((1, n1, n_tok, w), lambda b, j: (b, 0, j, 0))
    yr, yi = pl.pallas_call(
        functools.partial(_ft_stage1_kernel, n_tok=n_tok),
        grid=(b_, n2 // n_tok),
        in_specs=[tok_block,
                  _resident(tabs["w0"][0].shape), _resident(tabs["w0"][1].shape),
                  _resident(tabs["m1"][0].shape), _resident(tabs["m1"][1].shape),
                  pl.BlockSpec((1, n1, n_tok), lambda b, j: (j, 0, 0)),
                  pl.BlockSpec((1, n1, n_tok), lambda b, j: (j, 0, 0))],
        out_specs=[tok_block] * 2,
        out_shape=[grid_view, grid_view],
        compiler_params=_cparams("parallel", "parallel"),
        name="fourier_stage1",
    )(u.reshape(b_, n1, n2, w), *tabs["w0"], *tabs["m1"], twr, twi)
    n_k1 = min(FT_TOKENS_PER_STEP, n1)
    out = pl.pallas_call(
        functools.partial(_ft_stage2_kernel, n_k1=n_k1, n2=n2),
        grid=(b_, n1 // n_k1),
        in_specs=[pl.BlockSpec((1, n_k1 * n2, w), lambda b, j: (b, j, 0)),
                  pl.BlockSpec((1, n_k1 * n2, w), lambda b, j: (b, j, 0)),
                  _resident(tabs["m2"][0].shape), _resident(tabs["m2"][1].shape)],
        out_specs=pl.BlockSpec((1, n2, n_k1, w), lambda b, j: (b, 0, j, 0)),
        out_shape=jax.ShapeDtypeStruct((b_, n2, n1, w), F32),
        compiler_params=_cparams("parallel", "parallel"),
        name="fourier_stage2",
    )(yr.reshape(b_, t, w), yi.reshape(b_, t, w), *tabs["m2"])
    return out.reshape(b_, t, w)


def _merge_kernel(x_ref, mod_ref, hf_ref, hb_ref, mlo_ref, ft_ref, att_ref, of_ref, ob_ref, glr_ref,
                  wgate, bgate, wbr, wout, mlg, glg, lng, lnb, o_ref, *, alpha):
    x = x_ref[0]
    h = _modulate(x, mod_ref[0, 0:1, :], mod_ref[0, 1:2, :]).astype(BF16)
    ones = _group_ones(ML_W)
    hs = hf_ref[0] + hb_ref[0]
    hc = hs - _group_mean(hs, ones)
    hn = hc * lax.rsqrt(_group_mean(hc * hc, ones) + LN_EPS) * mlg[...]
    br_ml = hn * _sigmoid(mlo_ref[0])
    os_ = of_ref[0] + ob_ref[0]
    on = os_ * lax.rsqrt(_group_mean(os_ * os_, ones) + LN_EPS) * glg[...]
    br_gl = on * _silu(glr_ref[0])
    branches = (br_ml, ft_ref[0], att_ref[0], br_gl)
    mixed = None
    off = 0
    for j, br in enumerate(branches):
        wd = br.shape[1]
        gate = _sigmoid(jnp.dot(h, wgate[:, j * D_MODEL:(j + 1) * D_MODEL], preferred_element_type=F32)
                        + bgate[:, j * D_MODEL:(j + 1) * D_MODEL])
        u = gate * jnp.dot(br.astype(BF16), wbr[off:off + wd, :], preferred_element_type=F32)
        mixed = u if mixed is None else mixed + u
        off += wd
    y = jnp.dot(mixed.astype(BF16), wout[...], preferred_element_type=F32)
    o_ref[0] = _layernorm(alpha * x + mod_ref[0, 2:3, :] * y) * lng[...] + lnb[...]


def _merge_call(x, mod, hf, hb, ml, ftb, att, of, ob, gl, weights, alpha, tm):
    b_, t, d = x.shape
    tok = lambda w: pl.BlockSpec((1, tm, w), lambda b, i: (b, i, 0))
    last_quarter = pl.BlockSpec((1, tm, ML_W), lambda b, i: (b, i, 3))
    return pl.pallas_call(
        functools.partial(_merge_kernel, alpha=alpha),
        grid=(b_, t // tm),
        in_specs=[tok(d), pl.BlockSpec((1, N_ADA, d), lambda b, i: (b, 0, 0)),
                  tok(ML_W), tok(ML_W), last_quarter, tok(FT_W), tok(GQ_W), tok(GL_W), tok(GL_W),
                  last_quarter] + [_resident(w.shape) for w in weights],
        out_specs=tok(d),
        out_shape=jax.ShapeDtypeStruct((b_, t, d), F32),
        compiler_params=_cparams("parallel", "parallel"),
        name="merge",
    )(x, mod, hf, hb, ml, ftb, att, of, ob, gl, *weights)


def _ffn_kernel(x_ref, mod_ref, wg, wu, wd, lng, lnb, o_ref, *, alpha):
    x = x_ref[0]
    h = _modulate(x, mod_ref[0, 3:4, :], mod_ref[0, 4:5, :]).astype(BF16)
    a = jnp.dot(h, wg[...], preferred_element_type=F32)
    u = jnp.dot(h, wu[...], preferred_element_type=F32)
    f = jnp.dot((_silu(a) * u).astype(BF16), wd[...], preferred_element_type=F32)
    o_ref[0] = _layernorm(alpha * x + mod_ref[0, 5:6, :] * f) * lng[...] + lnb[...]


def _ffn_call(x, mod, weights, alpha, tm):
    b_, t, d = x.shape
    return pl.pallas_call(
        functools.partial(_ffn_kernel, alpha=alpha),
        grid=(b_, t // tm),
        in_specs=[pl.BlockSpec((1, tm, d), lambda b, i: (b, i, 0)),
                  pl.BlockSpec((1, N_ADA, d), lambda b, i: (b, 0, 0))]
                 + [_resident(w.shape) for w in weights],
        out_specs=pl.BlockSpec((1, tm, d), lambda b, i: (b, i, 0)),
        out_shape=jax.ShapeDtypeStruct((b_, t, d), F32),
        compiler_params=_cparams("parallel", "parallel"),
        name="ffn",
    )(x, mod, *weights)


def _router_kernel(x_ref, mod_ref, wrh, wrl, br_ref, h_out, dense_out, pos_out, post_out, cnt_out):
    tm = x_ref.shape[1]
    h = _modulate(x_ref[0], mod_ref[0, 3:4, :], mod_ref[0, 4:5, :])
    h_out[0] = h.astype(BF16)
    hh, hl = _split2(h)
    logits = _dot3(hh, hl, wrh[...], wrl[...])
    lane = lax.broadcasted_iota(jnp.int32, logits.shape, 1).astype(F32)
    sel = jnp.where(lane < N_EXPERTS, logits + br_ref[...], -jnp.inf)

    def pick(scores):
        mx = jnp.max(scores, axis=1, keepdims=True)
        idx = jnp.min(jnp.where(scores == mx, lane, 2.0 * SMALL_W), axis=1, keepdims=True)
        return lane == idx

    first = pick(sel)
    second = pick(jnp.where(first, -jnp.inf, sel))
    l1 = jnp.sum(jnp.where(first, logits, 0.0), axis=1, keepdims=True)
    l2 = jnp.sum(jnp.where(second, logits, 0.0), axis=1, keepdims=True)
    mx = jnp.maximum(l1, l2)
    e1, e2 = jnp.exp(l1 - mx), jnp.exp(l2 - mx)
    inv = 1.0 / (e1 + e2)
    dense_out[0] = jnp.where(first, e1 * inv, 0.0) + jnp.where(second, e2 * inv, 0.0)
    routed = first | second
    counts = jnp.dot(_tri(tm, upper=False), jnp.where(routed, 1.0, 0.0).astype(BF16),
                     preferred_element_type=F32)
    pos = jnp.where(routed, counts - 1.0, -1.0)
    pos_out[0] = pos
    post_out[0] = pos.T[0:8, :]
    cnt_out[0, 0] = jnp.broadcast_to(counts[tm - 1:tm, :], (8, SMALL_W))


def _router_call(x, mod, wrh, wrl, br, tm):
    b_, t, d = x.shape
    nt = t // tm
    tok = lambda w: pl.BlockSpec((1, tm, w), lambda b, i: (b, i, 0))
    return pl.pallas_call(
        _router_kernel,
        grid=(b_, nt),
        in_specs=[tok(d), pl.BlockSpec((1, N_ADA, d), lambda b, i: (b, 0, 0)),
                  _resident(wrh.shape), _resident(wrl.shape), _resident(br.shape)],
        out_specs=[tok(d), tok(SMALL_W), tok(SMALL_W),
                   pl.BlockSpec((1, 8, tm), lambda b, i: (b, 0, i)),
                   pl.BlockSpec((1, 1, 8, SMALL_W), lambda b, i: (b, i, 0, 0))],
        out_shape=[jax.ShapeDtypeStruct((b_, t, d), BF16), jax.ShapeDtypeStruct((b_, t, SMALL_W), F32),
                   jax.ShapeDtypeStruct((b_, t, SMALL_W), F32), jax.ShapeDtypeStruct((b_, 8, t), F32),
                   jax.ShapeDtypeStruct((b_, nt, 8, SMALL_W), F32)],
        compiler_params=_cparams("parallel", "parallel"),
        name="router",
    )(x, mod, wrh, wrl, br)


def _moe_kernel(cnt_ref, x_ref, mod_ref, h_ref, dense_ref, pos_ref, post_ref, wg, wu, wd, lng, lnb,
                o_ref, acc_scr, *, alpha, rows):
    b, i, e = pl.program_id(0), pl.program_id(1), pl.program_id(2)
    tm = x_ref.shape[1]

    @pl.when(e == 0)
    def _():
        acc_scr[...] = jnp.zeros(acc_scr.shape, F32)

    lane = lax.broadcasted_iota(jnp.int32, (tm, SMALL_W), 1)
    w_col = jnp.sum(jnp.where(lane == e, dense_ref[0], 0.0), axis=1, keepdims=True)
    pos_col = jnp.sum(jnp.where(lane == e, pos_ref[0], 0.0), axis=1, keepdims=True)
    pos_row = post_ref[0, pl.ds(e, 1), :]
    count = cnt_ref[(b * pl.num_programs(1) + i) * N_EXPERTS + e]
    r_iota = lax.broadcasted_iota(jnp.int32, (rows, 1), 0).astype(F32)
    c_iota = lax.broadcasted_iota(jnp.int32, (1, rows), 1).astype(F32)

    def block(j, carry):
        base = (j * rows).astype(F32)
        gather = jnp.where(pos_row == r_iota + base, 1.0, 0.0).astype(BF16)
        xs = jnp.dot(gather, h_ref[0], preferred_element_type=F32).astype(BF16)
        a = jnp.dot(xs, wg[0], preferred_element_type=F32)
        u = jnp.dot(xs, wu[0], preferred_element_type=F32)
        f = jnp.dot((_silu(a) * u).astype(BF16), wd[0], preferred_element_type=F32)
        scatter = jnp.where(pos_col == c_iota + base, 1.0, 0.0).astype(BF16)
        acc_scr[...] += w_col * jnp.dot(scatter, f.astype(BF16), preferred_element_type=F32)
        return carry

    lax.fori_loop(0, (count + rows - 1) // rows, block, 0)

    @pl.when(e == pl.num_programs(2) - 1)
    def _():
        o_ref[0] = (_layernorm(alpha * x_ref[0] + mod_ref[0, 5:6, :] * acc_scr[...]) * lng[...]
                    + lnb[...])


def _moe_call(x, mod, wrh, wrl, br, wg, wu, wd, lng, lnb, alpha, tm):
    b_, t, d = x.shape
    n_e, _, ff = wg.shape
    nt = t // tm
    h, dense, pos, post, cnt = _router_call(x, mod, wrh, wrl, br, tm)
    counts = cnt[:, :, 0, 0:n_e].astype(jnp.int32).reshape(-1)
    tok = lambda w: pl.BlockSpec((1, tm, w), lambda b, i, e, c: (b, i, 0))
    grid_spec = pltpu.PrefetchScalarGridSpec(
        num_scalar_prefetch=1,
        grid=(b_, nt, n_e),
        in_specs=[tok(d), pl.BlockSpec((1, N_ADA, d), lambda b, i, e, c: (b, 0, 0)),
                  tok(d), tok(SMALL_W), tok(SMALL_W),
                  pl.BlockSpec((1, 8, tm), lambda b, i, e, c: (b, 0, i)),
                  pl.BlockSpec((1, d, ff), lambda b, i, e, c: (e, 0, 0)),
                  pl.BlockSpec((1, d, ff), lambda b, i, e, c: (e, 0, 0)),
                  pl.BlockSpec((1, ff, d), lambda b, i, e, c: (e, 0, 0)),
                  _resident(lng.shape), _resident(lnb.shape)],
        out_specs=tok(d),
        scratch_shapes=[pltpu.VMEM((tm, d), F32)])
    return pl.pallas_call(
        functools.partial(_moe_kernel, alpha=alpha, rows=min(MOE_ROWS, tm)),
        grid_spec=grid_spec,
        out_shape=jax.ShapeDtypeStruct((b_, t, d), F32),
        compiler_params=_cparams("parallel", "parallel", "arbitrary"),
        name="moe",
    )(counts, x, mod, h, dense, pos, post, wg, wu, wd, lng, lnb)


def _rope_tables(t):
    rows = t // GRID_W
    row = jnp.repeat(jnp.arange(rows, dtype=F32), GRID_W)
    col = jnp.tile(jnp.arange(GRID_W, dtype=F32), rows)
    inv = jnp.power(ROPE_BASE, -jnp.arange(ROPE_PAIRS, dtype=F32) / ROPE_PAIRS)
    ar, ac = row[:, None] * inv, col[:, None] * inv
    cos = jnp.concatenate([jnp.cos(ar), jnp.cos(ar), jnp.cos(ac), jnp.cos(ac)], axis=1)
    sin = jnp.concatenate([-jnp.sin(ar), jnp.sin(ar), -jnp.sin(ac), jnp.sin(ac)], axis=1)
    return jnp.tile(cos, (1, 2)), jnp.tile(sin, (1, 2))


def _token_tile(t, pref):
    return pref if t % pref == 0 else t


def kernel(x, c, ctx, c_ctx, w_ada, b_ada, w_in, ml_gate_b, ml_norm_g, gq_qnorm_g, gq_knorm_g, gl_w2, gl_b2,
           gl_norm_g, w_branch, w_gate, b_gate, w_out, ln1_g, ln1_b, ln2_g, ln2_b, ffd_wg, ffd_wu, ffd_wd,
           moe_wr, moe_br, moe_wg, moe_wu, moe_wd):
    b_, s_len, d = x.shape
    n_ctx = ctx.shape[1]
    depth = w_in.shape[0]
    alpha = (2.0 * depth) ** 0.25
    bf = lambda a: a.astype(BF16)
    row = lambda a: a.reshape(1, -1).astype(F32)

    cos_l, sin_l = _rope_tables(s_len)
    cos_c, sin_c = jnp.ones((n_ctx, 128), F32), jnp.zeros((n_ctx, 128), F32)
    ft_l, ft_c = _fourier_tables(s_len), _fourier_tables(n_ctx)
    tm_l, tm_c = _token_tile(s_len, 512), _token_tile(n_ctx, 512)

    cc = jnp.zeros((8, d), F32).at[0:b_].set(c).at[b_].set(c_ctx)
    xc = ctx
    for l in range(depth):
        last = l == depth - 1
        ada = _ada_call(cc, bf(w_ada[l]), row(b_ada[l])).reshape(8, N_ADA, d)
        mod_l = ada[0:b_]
        mod_c = jnp.broadcast_to(ada[b_][None], (b_, N_ADA, d))

        w = w_in[l]
        o_gate, o_ft, o_gq, o_gl, o_af = 4 * ML_W, 4 * ML_W + 16, 4 * ML_W + 16 + FT_W, \
            4 * ML_W + 16 + FT_W + GQ_W + 2 * GQ_KW, 4 * ML_W + 16 + FT_W + GQ_W + 2 * GQ_KW + 4 * GL_W
        ml_scale = jnp.concatenate([jnp.ones((ML_W,)), jnp.full((ML_W,), HEAD_DIM ** -0.5), jnp.ones((2 * ML_W,))])
        gl_scale = jnp.concatenate([jnp.full((GL_W,), HEAD_DIM ** -0.5), jnp.ones((3 * GL_W,))])
        w_small = jnp.concatenate([w[:, o_gate:o_gate + 16], w[:, o_af:o_af + 2 * GL_RANK],
                                   jnp.zeros((d, SMALL_W - 16 - 2 * GL_RANK), F32)], axis=1)
        ws = [bf(w[:, 0:4 * ML_W] * ml_scale), bf(w[:, o_ft:o_ft + FT_W]), bf(w[:, o_gq:o_gl]),
              bf(w[:, o_gl:o_af] * gl_scale), bf(w_small)]
        gate_bias = jnp.zeros((1, SMALL_W), F32).at[0, 0:16].set(ml_gate_b[l].reshape(-1))
        w2e = jnp.zeros((SMALL_W, 2 * GL_W), F32)
        w2e = w2e.at[16:16 + GL_RANK, 0:GL_W].set(gl_w2[l, 0]).at[16 + GL_RANK:16 + 2 * GL_RANK, GL_W:].set(gl_w2[l, 1])
        merge_w = [bf(w_gate[l]), row(b_gate[l]), bf(w_branch[l]), bf(w_out[l]), row(ml_norm_g[l]),
                   row(gl_norm_g[l]), row(ln1_g[l]), row(ln1_b[l])]
        gq_g = jnp.tile(row(gq_qnorm_g[l]), (1, 2))
        gk_g = jnp.tile(row(gq_knorm_g[l]), (1, 2))
        logit_bound = (1.02 * LOG2_E * HEAD_DIM ** 0.5) * jnp.max(jnp.abs(gq_qnorm_g[l])) * jnp.max(jnp.abs(gq_knorm_g[l]))
        score_bias = jnp.zeros((1, HEAD_DIM), F32).at[0, 0].set(-logit_bound)

        def mixers(xs, mod, tm, cos, sin, states):
            ml, ftu, gq, gl, small = _inproj_call(xs, mod, ws, tm)
            hf, hb, ml_s, ml_m = _mlstm_call(ml, small, gate_bias, states[0], states[1])
            of, ob, gl_s = _gla_call(gl, small, bf(w2e), gl_b2[l].astype(F32), states[2])
            qt, k, vt = _qkprep_call(gq, cos, sin, gq_g, gk_g, score_bias, tm)
            return dict(ml=ml, ftu=ftu, gl=gl, hf=hf, hb=hb, of=of, ob=ob, qt=qt, k=k, vt=vt), (ml_s, ml_m, gl_s)

        zero_states = (jnp.zeros((b_, 2 * ML_HEADS, 2 * HEAD_DIM, 2 * HEAD_DIM), F32),
                       jnp.zeros((b_, 2 * ML_HEADS, 1, SMALL_W), F32),
                       jnp.zeros((b_, 2, GL_W, GL_W), F32))
        pc, ctx_states = mixers(xc, mod_c, tm_c, cos_c, sin_c, zero_states)
        pl_, _ = mixers(x, mod_l, tm_l, cos_l, sin_l, ctx_states)

        att_l = _attn_call(pl_["qt"], jnp.concatenate([pc["k"], pl_["k"]], axis=2),
                           jnp.concatenate([pc["vt"], pl_["vt"]], axis=2), logit_bound)
        x_mid = _merge_call(x, mod_l, pl_["hf"], pl_["hb"], pl_["ml"], _fourier_call(pl_["ftu"], ft_l), att_l,
                            pl_["of"], pl_["ob"], pl_["gl"], merge_w, alpha, tm_l)
        if not last:
            att_c = _attn_call(pc["qt"], pc["k"], pc["vt"], logit_bound)
            xc = _merge_call(xc, mod_c, pc["hf"], pc["hb"], pc["ml"], _fourier_call(pc["ftu"], ft_c), att_c,
                             pc["of"], pc["ob"], pc["gl"], merge_w, alpha, tm_c)

        j = l // 2
        if l % 2 == 0:
            ffn_w = [bf(ffd_wg[j]), bf(ffd_wu[j]), bf(ffd_wd[j]), row(ln2_g[l]), row(ln2_b[l])]
            x = _ffn_call(x_mid, mod_l, ffn_w, alpha, tm_l)
            if not last:
                xc = _ffn_call(xc, mod_c, ffn_w, alpha, tm_c)
        else:
            wr = jnp.zeros((d, SMALL_W), F32).at[:, 0:N_EXPERTS].set(moe_wr[j])
            wrh = bf(wr)
            wrl = bf(wr - wrh.astype(F32))
            brp = jnp.zeros((1, SMALL_W), F32).at[0, 0:N_EXPERTS].set(moe_br[j])
            moe_args = (wrh, wrl, brp, bf(moe_wg[j]), bf(moe_wu[j]), bf(moe_wd[j]), row(ln2_g[l]), row(ln2_b[l]))
            x = _moe_call(x_mid, mod_l, *moe_args, alpha, _token_tile(s_len, 1024))
            if not last:
                xc = _moe_call(xc, mod_c, *moe_args, alpha, tm_c)
    return x
```

```python
import functools
import math

import jax
import jax.numpy as jnp
import numpy as np
from jax import lax
from jax.experimental import pallas as pl
from jax.experimental.pallas import tpu as pltpu

F32 = jnp.float32
BF16 = jnp.bfloat16

D_MODEL = 1024
GRID_W = 64
HEAD_DIM = 64
ML_HEADS = 4
ML_W = ML_HEADS * HEAD_DIM
FT_GROUPS = 4
FT_GC = 64
FT_W = FT_GROUPS * FT_GC
GQ_KV = 2
GQ_G = 4
GQ_W = GQ_KV * GQ_G * HEAD_DIM
GQ_KW = GQ_KV * HEAD_DIM
ROPE_PAIRS = HEAD_DIM // 4
ROPE_BASE = 10000.0
GL_HEADS = 4
GL_W = GL_HEADS * HEAD_DIM
GL_RANK = 16
GL_TAU = 16.0
N_EXPERTS = 8
N_ADA = 6
LN_EPS = 1e-6
SMALL_W = 128

ML_CHUNK = 256
GL_CHUNK = 128
GL_SUB = 16
GL_EXP_CLAMP = 80.0
ATT_TQ = 512
ATT_TK = 1280
ATT_ROWS = 256
LOG2_E = 1.4426950408889634
MAX_LOGIT_BOUND = 60.0
NEG_BIG = -1e30
MOE_ROWS = 256
FT_TOKENS_PER_STEP = 8
V_ROWS = 128

VMEM_LIMIT = 56 * 1024 * 1024


def _cparams(*sem):
    return pltpu.CompilerParams(dimension_semantics=sem, vmem_limit_bytes=VMEM_LIMIT)


def _resident(shape):
    nd = len(shape)
    return pl.BlockSpec(shape, lambda *_: (0,) * nd, pipeline_mode=pl.Buffered(1))


def _bdot(a, b):
    return jnp.dot(a.astype(BF16), b.astype(BF16), preferred_element_type=F32)


def _split2(x):
    hi = x.astype(BF16)
    lo = (x - hi.astype(F32)).astype(BF16)
    return hi, lo


def _split3(x):
    a = x.astype(BF16)
    r = x - a.astype(F32)
    b = r.astype(BF16)
    c = (r - b.astype(F32)).astype(BF16)
    return a, b, c


def _dot_exact_rhs(x, m_bf16):
    a, b, c = _split3(x)
    d = functools.partial(jnp.dot, preferred_element_type=F32)
    return d(a, m_bf16) + d(b, m_bf16) + d(c, m_bf16)


def _dot_exact_lhs(m_bf16, x):
    a, b, c = _split3(x)
    d = functools.partial(jnp.dot, preferred_element_type=F32)
    return d(m_bf16, a) + d(m_bf16, b) + d(m_bf16, c)


def _dot3(a_hi, a_lo, b_hi, b_lo):
    d = functools.partial(jnp.dot, preferred_element_type=F32)
    return d(a_hi, b_hi) + d(a_hi, b_lo) + d(a_lo, b_hi)


def _sigmoid(x):
    return 1.0 / (1.0 + jnp.exp(-x))


def _silu(x):
    return x * _sigmoid(x)


def _log_sigmoid(x):
    return jnp.minimum(x, 0.0) - jnp.log(1.0 + jnp.exp(-jnp.abs(x)))


def _layernorm(x):
    mu = jnp.mean(x, axis=-1, keepdims=True)
    xc = x - mu
    var = jnp.mean(xc * xc, axis=-1, keepdims=True)
    return xc * lax.rsqrt(var + LN_EPS)


def _modulate(x, shift, scale):
    return _layernorm(x) * (1.0 + scale) + shift


def _group_ones(width):
    r = lax.broadcasted_iota(jnp.int32, (width, width), 0) >> 6
    c = lax.broadcasted_iota(jnp.int32, (width, width), 1) >> 6
    return jnp.where(r == c, 1.0, 0.0).astype(BF16)


def _group_mean(x, ones):
    return _dot_exact_rhs(x, ones) * (1.0 / HEAD_DIM)


def _tri(n, upper):
    r = lax.broadcasted_iota(jnp.int32, (n, n), 0)
    c = lax.broadcasted_iota(jnp.int32, (n, n), 1)
    keep = (c >= r) if upper else (c <= r)
    return jnp.where(keep, 1.0, 0.0).astype(BF16)


def _ada_kernel(c_ref, w_ref, b_ref, o_ref):
    o_ref[...] = _bdot(_silu(c_ref[...]), w_ref[...]) + b_ref[...]


def _ada_call(cc, w, b):
    rows, d = cc.shape
    n = w.shape[1]
    tn = 1024
    return pl.pallas_call(
        _ada_kernel,
        grid=(n // tn,),
        in_specs=[pl.BlockSpec((rows, d), lambda j: (0, 0)),
                  pl.BlockSpec((d, tn), lambda j: (0, j)),
                  pl.BlockSpec((1, tn), lambda j: (0, j))],
        out_specs=pl.BlockSpec((rows, tn), lambda j: (0, j)),
        out_shape=jax.ShapeDtypeStruct((rows, n), F32),
        compiler_params=_cparams("parallel"),
        name="ada",
    )(cc, w, b)


def _inproj_kernel(x_ref, mod_ref, w_ml, w_ft, w_gq, w_gl, w_sm, o_ml, o_ft, o_gq, o_gl, o_sm):
    h = _modulate(x_ref[0], mod_ref[0, 0:1, :], mod_ref[0, 1:2, :]).astype(BF16)
    for w, o in ((w_ml, o_ml), (w_ft, o_ft), (w_gq, o_gq), (w_gl, o_gl), (w_sm, o_sm)):
        o[0] = jnp.dot(h, w[...], preferred_element_type=F32)


def _inproj_call(x, mod, ws, tm):
    b_, t, d = x.shape
    widths = [w.shape[1] for w in ws]
    return pl.pallas_call(
        _inproj_kernel,
        grid=(b_, t // tm),
        in_specs=[pl.BlockSpec((1, tm, d), lambda b, i: (b, i, 0)),
                  pl.BlockSpec((1, N_ADA, d), lambda b, i: (b, 0, 0))]
                 + [_resident(w.shape) for w in ws],
        out_specs=[pl.BlockSpec((1, tm, n), lambda b, i: (b, i, 0)) for n in widths],
        out_shape=[jax.ShapeDtypeStruct((b_, t, n), F32) for n in widths],
        compiler_params=_cparams("parallel", "parallel"),
        name="inproj",
    )(x, mod, *ws)


def _mlstm_kernel(qkv_f, sm_f, qkv_b, sm_b, bias_ref, s0_ref, m0_ref,
                  hf_ref, hb_ref, st_ref, mt_ref, s_scr, m_scr, *, chunk):
    i = pl.program_id(1)
    n_l = chunk

    @pl.when(i == 0)
    def _():
        s_scr[...] = s0_ref[0]
        m_scr[...] = m0_ref[0]

    row = lax.broadcasted_iota(jnp.int32, (n_l, n_l), 0)
    col = lax.broadcasted_iota(jnp.int32, (n_l, n_l), 1)
    lane = lax.broadcasted_iota(jnp.int32, (n_l, 128), 1)
    sub = lax.broadcasted_iota(jnp.int32, (HEAD_DIM, n_l), 0)
    ones_row = jnp.where(sub == 0, 1.0, 0.0)
    for d, (qkv_ref, sm_ref, h_ref) in enumerate(((qkv_f, sm_f, hf_ref), (qkv_b, sm_b, hb_ref))):
        rev = d == 1
        blk = qkv_ref[0]
        pre = sm_ref[0] + bias_ref[...]
        bcum = _dot_exact_lhs(_tri(n_l, upper=rev), _log_sigmoid(pre))
        pre_t = pre.T
        b_t = bcum.T
        q_t = blk[:, 0:ML_W].T.astype(BF16)
        v_t = blk[:, 2 * ML_W:3 * ML_W].T
        mask = (row >= col) if rev else (row <= col)
        last = 0 if rev else n_l - 1
        outs = []
        for h in range(ML_HEADS):
            ci = 8 * d + h
            cf = 8 * d + 4 + h
            idx = 4 * d + h
            pair = h // 2
            own = (lane >= HEAD_DIM) if h % 2 else (lane < HEAD_DIM)
            k_own = jnp.where(own, blk[:, ML_W + pair * 128:ML_W + (pair + 1) * 128], 0.0).astype(BF16)
            q_pair = q_t[pair * 128:(pair + 1) * 128, :]
            v_h = v_t[h * HEAD_DIM:(h + 1) * HEAD_DIM, :]
            c_col = pre[:, ci:ci + 1] - bcum[:, cf:cf + 1]
            b_row = b_t[cf:cf + 1, :]
            i_row = pre_t[ci:ci + 1, :]
            state = s_scr[idx]
            m_prev = m_scr[idx][:, 0:1]

            dmat = jnp.where(mask, b_row + c_col, -jnp.inf)
            inter = b_row + m_prev
            m_t = jnp.maximum(inter, jnp.max(dmat, axis=0, keepdims=True))
            w_intra = jnp.exp(dmat - m_t) * jnp.dot(k_own, q_pair, preferred_element_type=F32)
            w_inter = jnp.exp(inter - m_t)
            sq = jnp.dot(state.astype(BF16), q_pair, preferred_element_type=F32)
            num = (jnp.dot(v_h.astype(BF16), w_intra.astype(BF16), preferred_element_type=F32)
                   + w_inter * sq[0:HEAD_DIM])
            den = jnp.sum(w_intra, axis=0, keepdims=True) + w_inter * sq[HEAD_DIM:HEAD_DIM + 1]
            outs.append(num / jnp.maximum(jnp.abs(den), jnp.exp(-m_t)))

            b_last = b_row[:, last:last + 1]
            g_row = b_last - b_row + i_row
            m_new = jnp.maximum(b_last + m_prev, jnp.max(g_row, axis=1, keepdims=True))
            ws = jnp.exp(g_row - m_new)
            wc = jnp.exp(b_last + m_prev - m_new)
            v_ext = jnp.concatenate([v_h, ones_row], axis=0)
            s_scr[idx] = wc * state + jnp.dot((v_ext * ws).astype(BF16), k_own, preferred_element_type=F32)
            m_scr[idx] = jnp.broadcast_to(m_new, (1, SMALL_W))
        h_ref[0] = jnp.concatenate(outs, axis=0).T

    @pl.when(i == pl.num_programs(1) - 1)
    def _():
        st_ref[0] = s_scr[...]
        mt_ref[0] = m_scr[...]


def _mlstm_call(ml, small, bias, s0, m0):
    b_, t, _ = ml.shape
    chunk = min(ML_CHUNK, t)
    n = t // chunk
    fwd = lambda b, i: (b, i, 0)
    bwd = lambda b, i: (b, n - 1 - i, 0)
    state_spec = pl.BlockSpec((1, 2 * ML_HEADS, 2 * HEAD_DIM, 2 * HEAD_DIM), lambda b, i: (b, 0, 0, 0))
    m_spec = pl.BlockSpec((1, 2 * ML_HEADS, 1, SMALL_W), lambda b, i: (b, 0, 0, 0))
    return pl.pallas_call(
        functools.partial(_mlstm_kernel, chunk=chunk),
        grid=(b_, n),
        in_specs=[pl.BlockSpec((1, chunk, 3 * ML_W), fwd), pl.BlockSpec((1, chunk, SMALL_W), fwd),
                  pl.BlockSpec((1, chunk, 3 * ML_W), bwd), pl.BlockSpec((1, chunk, SMALL_W), bwd),
                  pl.BlockSpec((1, SMALL_W), lambda b, i: (0, 0)), state_spec, m_spec],
        out_specs=[pl.BlockSpec((1, chunk, ML_W), fwd), pl.BlockSpec((1, chunk, ML_W), bwd),
                   state_spec, m_spec],
        out_shape=[jax.ShapeDtypeStruct((b_, t, ML_W), F32), jax.ShapeDtypeStruct((b_, t, ML_W), F32),
                   jax.ShapeDtypeStruct(s0.shape, F32), jax.ShapeDtypeStruct(m0.shape, F32)],
        scratch_shapes=[pltpu.VMEM((2 * ML_HEADS, 2 * HEAD_DIM, 2 * HEAD_DIM), F32),
                        pltpu.VMEM((2 * ML_HEADS, 1, SMALL_W), F32)],
        compiler_params=_cparams("parallel", "arbitrary"),
        name="mlstm",
    )(ml, small, ml, small, bias, s0, m0)


def _gla_kernel(qkv_f, sm_f, qkv_b, sm_b, w2_ref, b2_ref, s0_ref,
                of_ref, ob_ref, st_ref, s_scr, *, chunk):
    i = pl.program_id(1)
    n_l = chunk
    n_sub = n_l // GL_SUB
    width = GL_W

    @pl.when(i == 0)
    def _():
        s_scr[...] = s0_ref[0]

    lane = lax.broadcasted_iota(jnp.int32, (GL_SUB, width), 1) >> 6
    head_masks = [lane == h for h in range(GL_HEADS)]
    chunk_lane = lax.broadcasted_iota(jnp.int32, (n_l, width), 1) >> 6
    chunk_masks = [chunk_lane == h for h in range(GL_HEADS)]
    bd_mask = ((lax.broadcasted_iota(jnp.int32, (width, width), 0) >> 6)
               == (lax.broadcasted_iota(jnp.int32, (width, width), 1) >> 6))
    states = [s_scr[0], s_scr[1]]
    new_states, new_outs = [], []
    for d, (qkv_ref, sm_ref, o_ref) in enumerate(((qkv_f, sm_f, of_ref), (qkv_b, sm_b, ob_ref))):
        rev = d == 1
        blk = qkv_ref[0]
        q = blk[:, 0:width]
        k = blk[:, width:2 * width]
        v = blk[:, 2 * width:3 * width]
        a = _bdot(sm_ref[0], w2_ref[...])[:, d * width:(d + 1) * width] + b2_ref[d:d + 1, :]
        la = _log_sigmoid(a) * (1.0 / GL_TAU)
        g = _dot_exact_lhs(_tri(n_l, upper=rev), la)
        last = 0 if rev else n_l - 1
        g_end = g[last:last + 1, :]
        state = states[d]
        o_inter = lax.dot_general((q * jnp.exp(g)).astype(BF16), state.astype(BF16),
                                  (((1,), (1,)), ((), ())), preferred_element_type=F32)
        v_bf = v.astype(BF16)
        a_blocks = []
        t_idx = lax.broadcasted_iota(jnp.int32, (GL_HEADS * GL_SUB, n_l), 0) & (GL_SUB - 1)
        s_idx = lax.broadcasted_iota(jnp.int32, (GL_HEADS * GL_SUB, n_l), 1)
        for s in range(n_sub):
            lo = s * GL_SUB
            hi = lo + GL_SUB
            r = g[hi - 1:hi, :] if rev else g[lo:lo + 1, :]
            qt = q[lo:hi] * jnp.exp(g[lo:hi] - r)
            kt = k * jnp.exp(jnp.minimum(r - g, GL_EXP_CLAMP))
            qstack = jnp.concatenate([jnp.where(hm, qt, 0.0) for hm in head_masks], axis=0)
            amat = lax.dot_general(qstack.astype(BF16), kt.astype(BF16),
                                   (((1,), (1,)), ((), ())), preferred_element_type=F32)
            keep = (s_idx >= lo + t_idx) if rev else (s_idx <= lo + t_idx)
            a_blocks.append(jnp.where(keep, amat, 0.0).astype(BF16))
        a_all = jnp.concatenate([a_blocks[s][h * GL_SUB:(h + 1) * GL_SUB]
                                 for h in range(GL_HEADS) for s in range(n_sub)], axis=0)
        ov = jnp.dot(a_all, v_bf, preferred_element_type=F32)
        o_intra = jnp.where(chunk_masks[0], ov[0:n_l], 0.0)
        for h in range(1, GL_HEADS):
            o_intra = o_intra + jnp.where(chunk_masks[h], ov[h * n_l:(h + 1) * n_l], 0.0)
        new_outs.append(o_inter + o_intra)

        kg = k * jnp.exp(g_end - g)
        upd = jnp.dot(v.T.astype(BF16), kg.astype(BF16), preferred_element_type=F32)
        new_states.append(jnp.exp(g_end) * state + jnp.where(bd_mask, upd, 0.0))
    of_ref[0] = new_outs[0]
    ob_ref[0] = new_outs[1]
    s_scr[0] = new_states[0]
    s_scr[1] = new_states[1]

    @pl.when(i == pl.num_programs(1) - 1)
    def _():
        st_ref[0] = s_scr[...]


def _gla_call(gl, small, w2e, b2, s0):
    b_, t, _ = gl.shape
    chunk = min(GL_CHUNK, t)
    n = t // chunk
    fwd = lambda b, i: (b, i, 0)
    bwd = lambda b, i: (b, n - 1 - i, 0)
    state_spec = pl.BlockSpec((1, 2, GL_W, GL_W), lambda b, i: (b, 0, 0, 0))
    return pl.pallas_call(
        functools.partial(_gla_kernel, chunk=chunk),
        grid=(b_, n),
        in_specs=[pl.BlockSpec((1, chunk, 3 * GL_W), fwd), pl.BlockSpec((1, chunk, SMALL_W), fwd),
                  pl.BlockSpec((1, chunk, 3 * GL_W), bwd), pl.BlockSpec((1, chunk, SMALL_W), bwd),
                  _resident(w2e.shape), _resident(b2.shape), state_spec],
        out_specs=[pl.BlockSpec((1, chunk, GL_W), fwd), pl.BlockSpec((1, chunk, GL_W), bwd), state_spec],
        out_shape=[jax.ShapeDtypeStruct((b_, t, GL_W), F32), jax.ShapeDtypeStruct((b_, t, GL_W), F32),
                   jax.ShapeDtypeStruct(s0.shape, F32)],
        scratch_shapes=[pltpu.VMEM((2, GL_W, GL_W), F32)],
        compiler_params=_cparams("parallel", "arbitrary"),
        name="gla",
    )(gl, small, gl, small, w2e, b2, s0)


def _qkprep_kernel(gq_ref, cos_ref, sin_ref, gq_g, gk_g, sb_ref, qt_out, k_out, vt_out):
    x = gq_ref[0]
    tm = x.shape[0]

    def norm_rope(z, g, reps):
        width = z.shape[1]
        msq = _group_mean(z * z, _group_ones(width))
        zn = z * lax.rsqrt(msq + LN_EPS) * jnp.tile(g, (1, reps))
        lane = lax.broadcasted_iota(jnp.int32, zn.shape, 1)
        partner = jnp.where((lane & 31) < ROPE_PAIRS,
                            pltpu.roll(zn, width - ROPE_PAIRS, axis=1),
                            pltpu.roll(zn, ROPE_PAIRS, axis=1))
        return zn * jnp.tile(cos_ref[...], (1, reps)) + partner * jnp.tile(sin_ref[...], (1, reps))

    lane = lax.broadcasted_iota(jnp.int32, (tm, 128), 1)

    def pad_heads(z, extra):
        out = []
        for p in range(z.shape[1] // 128):
            pair = z[:, p * 128:(p + 1) * 128]
            for base in (pair, pltpu.roll(pair, HEAD_DIM, axis=1)):
                out.append(jnp.where(lane < HEAD_DIM, base, jnp.where(lane == HEAD_DIM, extra, 0.0)))
        return out

    q = norm_rope(x[:, 0:GQ_W], gq_g[...], GQ_W // 128) * (LOG2_E * HEAD_DIM ** -0.5)
    qt_out[0] = jnp.concatenate(pad_heads(q, 1.0), axis=1).T.astype(BF16)
    k = norm_rope(x[:, GQ_W:GQ_W + GQ_KW], gk_g[...], GQ_KW // 128)
    for j, kj in enumerate(pad_heads(k, sb_ref[0:1, 0:1])):
        k_out[0, j] = kj.astype(BF16)
    v = x[:, GQ_W + GQ_KW:GQ_W + 2 * GQ_KW]
    v_t = v.T
    sub = lax.broadcasted_iota(jnp.int32, (V_ROWS - HEAD_DIM, tm), 0)
    ones_rows = jnp.where(sub == 0, 1.0, 0.0)
    vt_out[0] = jnp.concatenate([v_t[0:HEAD_DIM], ones_rows, v_t[HEAD_DIM:2 * HEAD_DIM], ones_rows],
                                axis=0).astype(BF16)


def _qkprep_call(gq, cos, sin, gq_g, gk_g, score_bias, tm):
    b_, t, w = gq.shape
    n_q = GQ_KV * GQ_G
    return pl.pallas_call(
        _qkprep_kernel,
        grid=(b_, t // tm),
        in_specs=[pl.BlockSpec((1, tm, w), lambda b, i: (b, i, 0)),
                  pl.BlockSpec((tm, 128), lambda b, i: (i, 0)),
                  pl.BlockSpec((tm, 128), lambda b, i: (i, 0)),
                  _resident(gq_g.shape), _resident(gk_g.shape), _resident(score_bias.shape)],
        out_specs=[pl.BlockSpec((1, n_q * 128, tm), lambda b, i: (b, 0, i)),
                   pl.BlockSpec((1, GQ_KV, tm, 128), lambda b, i: (b, 0, i, 0)),
                   pl.BlockSpec((1, GQ_KV * V_ROWS, tm), lambda b, i: (b, 0, i))],
        out_shape=[jax.ShapeDtypeStruct((b_, n_q * 128, t), BF16),
                   jax.ShapeDtypeStruct((b_, GQ_KV, t, 128), BF16),
                   jax.ShapeDtypeStruct((b_, GQ_KV * V_ROWS, t), BF16)],
        compiler_params=_cparams("parallel", "parallel"),
        name="qkprep",
    )(gq, cos, sin, gq_g, gk_g, score_bias)


def _attn_kernel(qt_ref, k_ref, vt_ref, o_ref, m_scr, acc_scr, *, n_tiles, tk, bounded):
    acc_scr[...] = jnp.zeros(acc_scr.shape, F32)
    if not bounded:
        m_scr[...] = jnp.full(m_scr.shape, NEG_BIG, F32)

    def body(j, carry):
        start = pl.multiple_of(j * tk, tk)
        k_tile = k_ref[0, 0, pl.ds(start, tk), :]
        vt_tile = vt_ref[0, :, pl.ds(start, tk)]
        for g in range(GQ_G):
            st = jnp.dot(k_tile, qt_ref[0, g * 128:(g + 1) * 128, :], preferred_element_type=F32)
            if bounded:
                acc_scr[g] += jnp.dot(vt_tile, jnp.exp2(st).astype(BF16), preferred_element_type=F32)
            else:
                m_prev = m_scr[g]
                m_new = jnp.maximum(m_prev, jnp.max(st, axis=0, keepdims=True))
                p = jnp.exp2(st - m_new).astype(BF16)
                acc_scr[g] = (jnp.exp2(m_prev - m_new) * acc_scr[g]
                              + jnp.dot(vt_tile, p, preferred_element_type=F32))
                m_scr[g] = m_new
        return carry

    lax.fori_loop(0, n_tiles, body, 0)
    outs = []
    for g in range(GQ_G):
        acc = acc_scr[g]
        outs.append(acc[0:HEAD_DIM] / acc[HEAD_DIM:HEAD_DIM + 1])
    o_ref[0] = jnp.concatenate(outs, axis=0).T


def _attn_call(qt, k, vt, logit_bound):
    b_, _, t = qt.shape
    n_keys = k.shape[2]
    tq = min(ATT_TQ, t)
    tk = ATT_TK if n_keys % ATT_TK == 0 else n_keys
    gw = GQ_G * HEAD_DIM

    def call(bounded):
        return pl.pallas_call(
            functools.partial(_attn_kernel, n_tiles=n_keys // tk, tk=tk, bounded=bounded),
            grid=(b_, GQ_KV, t // tq),
            in_specs=[pl.BlockSpec((1, GQ_G * 128, tq), lambda b, kv, i: (b, kv, i)),
                      pl.BlockSpec((1, 1, n_keys, 128), lambda b, kv, i: (b, kv, 0, 0)),
                      pl.BlockSpec((1, V_ROWS, n_keys), lambda b, kv, i: (b, kv, 0))],
            out_specs=pl.BlockSpec((1, tq, gw), lambda b, kv, i: (b, i, kv)),
            out_shape=jax.ShapeDtypeStruct((b_, t, GQ_W), F32),
            scratch_shapes=[pltpu.VMEM((GQ_G, 1, tq), F32),
                            pltpu.VMEM((GQ_G, V_ROWS, tq), F32)],
            compiler_params=_cparams("parallel", "parallel", "arbitrary"),
            name="attention_bounded" if bounded else "attention_online",
        )(qt, k, vt)

    return lax.cond(logit_bound <= MAX_LOGIT_BOUND, lambda: call(True), lambda: call(False))


def _fourier_factors(t):
    bits = int(round(math.log2(t)))
    assert 1 << bits == t
    n1 = 1 << (bits // 2)
    return n1, t // n1


def _hi_lo(a):
    a = np.asarray(a, np.float64)
    hi = jnp.asarray(a, F32).astype(BF16)
    lo = (jnp.asarray(a, F32) - hi.astype(F32)).astype(BF16)
    return hi, lo


def _fourier_tables(t):
    n1, n2 = _fourier_factors(t)
    c = np.arange(FT_GC)
    ang = 2.0 * np.pi * np.outer(c, c) / FT_GC
    eye = np.eye(FT_GROUPS)
    w0 = np.concatenate([np.kron(eye, np.cos(ang)), -np.kron(eye, np.sin(ang))], axis=1)
    a1 = 2.0 * np.pi * np.outer(np.arange(n1), np.arange(n1)) / n1
    fr, fi = np.cos(a1), -np.sin(a1)
    m1 = np.block([[fr, -fi], [fi, fr]])
    tw = 2.0 * np.pi * np.outer(np.arange(n1), np.arange(n2)) / t
    a2 = 2.0 * np.pi * np.outer(np.arange(n2), np.arange(n2)) / n2
    m2 = np.concatenate([np.cos(a2), np.sin(a2)], axis=1) / math.sqrt(t * FT_GC)
    return dict(n1=n1, n2=n2, w0=_hi_lo(w0), m1=_hi_lo(m1), m2=_hi_lo(m2),
                twr=np.cos(tw).astype(np.float32), twi=(-np.sin(tw)).astype(np.float32))


def _ft_stage1_kernel(u_ref, w0h, w0l, m1h, m1l, twr_ref, twi_ref, yr_ref, yi_ref, *, n_tok):
    n1 = u_ref.shape[1]
    zr, zi = [], []
    for j in range(n_tok):
        uh, ul = _split2(u_ref[0, :, j * FT_W:(j + 1) * FT_W])
        z = _dot3(uh, ul, w0h[...], w0l[...])
        zr.append(z[:, :FT_W])
        zi.append(z[:, FT_W:])
    z = jnp.concatenate([jnp.concatenate(zr, axis=1), jnp.concatenate(zi, axis=1)], axis=0)
    zh, zl = _split2(z)
    y = _dot3(m1h[...], m1l[...], zh, zl)
    y_r, y_i = y[:n1], y[n1:]
    twr, twi = twr_ref[0], twi_ref[0]
    for j in range(n_tok):
        sl = slice(j * FT_W, (j + 1) * FT_W)
        cr, ci = twr[:, j:j + 1], twi[:, j:j + 1]
        yr_ref[0, :, j, :] = y_r[:, sl] * cr - y_i[:, sl] * ci
        yi_ref[0, :, j, :] = y_r[:, sl] * ci + y_i[:, sl] * cr


def _ft_stage2_kernel(yr_ref, yi_ref, m2h, m2l, o_ref, *, n_k1, n2):
    for j in range(n_k1):
        y = jnp.concatenate([yr_ref[0, j * n2:(j + 1) * n2, :], yi_ref[0, j * n2:(j + 1) * n2, :]], axis=0)
        yh, yl = _split2(y)
        o_ref[0, :, j, :] = _dot3(m2h[...], m2l[...], yh, yl)


def _fourier_call(u, tabs):
    b_, t, w = u.shape
    n1, n2 = tabs["n1"], tabs["n2"]
    n_tok = min(FT_TOKENS_PER_STEP, n2)
    twr = jnp.asarray(tabs["twr"]).reshape(n1, n2 // n_tok, n_tok).transpose(1, 0, 2)
    twi = jnp.asarray(tabs["twi"]).reshape(n1, n2 // n_tok, n_tok).transpose(1, 0, 2)
    grid_view = jax.ShapeDtypeStruct((b_, n1, n2, w), F32)
    tok_block = pl.BlockSpec((1, n1, n_tok, w), lambda b, j: (b, 0, j, 0))
    yr, yi = pl.pallas_call(
        functools.partial(_ft_stage1_kernel, n_tok=n_tok),
        grid=(b_, n2 // n_tok),
        in_specs=[pl.BlockSpec((1, n1, n_tok * w), lambda b, j: (b, 0, j)),
                  _resident(tabs["w0"][0].shape), _resident(tabs["w0"][1].shape),
                  _resident(tabs["m1"][0].shape), _resident(tabs["m1"][1].shape),
                  pl.BlockSpec((1, n1, n_tok), lambda b, j: (j, 0, 0)),
                  pl.BlockSpec((1, n1, n_tok), lambda b, j: (j, 0, 0))],
        out_specs=[tok_block] * 2,
        out_shape=[grid_view, grid_view],
        compiler_params=_cparams("parallel", "parallel"),
        name="fourier_stage1",
    )(u.reshape(b_, n1, n2 * w), *tabs["w0"], *tabs["m1"], twr, twi)
    n_k1 = min(FT_TOKENS_PER_STEP, n1)
    out = pl.pallas_call(
        functools.partial(_ft_stage2_kernel, n_k1=n_k1, n2=n2),
        grid=(b_, n1 // n_k1),
        in_specs=[pl.BlockSpec((1, n_k1 * n2, w), lambda b, j: (b, j, 0)),
                  pl.BlockSpec((1, n_k1 * n2, w), lambda b, j: (b, j, 0)),
                  _resident(tabs["m2"][0].shape), _resident(tabs["m2"][1].shape)],
        out_specs=pl.BlockSpec((1, n2, n_k1, w), lambda b, j: (b, 0, j, 0)),
        out_shape=jax.ShapeDtypeStruct((b_, n2, n1, w), F32),
        compiler_params=_cparams("parallel", "parallel"),
        name="fourier_stage2",
    )(yr.reshape(b_, t, w), yi.reshape(b_, t, w), *tabs["m2"])
    return out.reshape(b_, t, w)


def _merge_kernel(x_ref, mod_ref, hf_ref, hb_ref, mlo_ref, ft_ref, att_ref, of_ref, ob_ref, glr_ref,
                  wgate, bgate, wbr, wout, mlg, glg, lng, lnb, o_ref, *, alpha):
    x = x_ref[0]
    h = _modulate(x, mod_ref[0, 0:1, :], mod_ref[0, 1:2, :]).astype(BF16)
    ones = _group_ones(ML_W)
    hs = hf_ref[0] + hb_ref[0]
    hc = hs - _group_mean(hs, ones)
    hn = hc * lax.rsqrt(_group_mean(hc * hc, ones) + LN_EPS) * mlg[...]
    br_ml = hn * _sigmoid(mlo_ref[0])
    os_ = of_ref[0] + ob_ref[0]
    on = os_ * lax.rsqrt(_group_mean(os_ * os_, ones) + LN_EPS) * glg[...]
    br_gl = on * _silu(glr_ref[0])
    branches = (br_ml, ft_ref[0], att_ref[0], br_gl)
    mixed = None
    off = 0
    for j, br in enumerate(branches):
        wd = br.shape[1]
        gate = _sigmoid(jnp.dot(h, wgate[:, j * D_MODEL:(j + 1) * D_MODEL], preferred_element_type=F32)
                        + bgate[:, j * D_MODEL:(j + 1) * D_MODEL])
        u = gate * jnp.dot(br.astype(BF16), wbr[off:off + wd, :], preferred_element_type=F32)
        mixed = u if mixed is None else mixed + u
        off += wd
    y = jnp.dot(mixed.astype(BF16), wout[...], preferred_element_type=F32)
    o_ref[0] = _layernorm(alpha * x + mod_ref[0, 2:3, :] * y) * lng[...] + lnb[...]


def _merge_call(x, mod, hf, hb, ml, ftb, att, of, ob, gl, weights, alpha, tm):
    b_, t, d = x.shape
    tok = lambda w: pl.BlockSpec((1, tm, w), lambda b, i: (b, i, 0))
    last_quarter = pl.BlockSpec((1, tm, ML_W), lambda b, i: (b, i, 3))
    return pl.pallas_call(
        functools.partial(_merge_kernel, alpha=alpha),
        grid=(b_, t // tm),
        in_specs=[tok(d), pl.BlockSpec((1, N_ADA, d), lambda b, i: (b, 0, 0)),
                  tok(ML_W), tok(ML_W), last_quarter, tok(FT_W), tok(GQ_W), tok(GL_W), tok(GL_W),
                  last_quarter] + [_resident(w.shape) for w in weights],
        out_specs=tok(d),
        out_shape=jax.ShapeDtypeStruct((b_, t, d), F32),
        compiler_params=_cparams("parallel", "parallel"),
        name="merge",
    )(x, mod, hf, hb, ml, ftb, att, of, ob, gl, *weights)


def _ffn_kernel(x_ref, mod_ref, wg, wu, wd, lng, lnb, o_ref, *, alpha):
    x = x_ref[0]
    h = _modulate(x, mod_ref[0, 3:4, :], mod_ref[0, 4:5, :]).astype(BF16)
    a = jnp.dot(h, wg[...], preferred_element_type=F32)
    u = jnp.dot(h, wu[...], preferred_element_type=F32)
    f = jnp.dot((_silu(a) * u).astype(BF16), wd[...], preferred_element_type=F32)
    o_ref[0] = _layernorm(alpha * x + mod_ref[0, 5:6, :] * f) * lng[...] + lnb[...]


def _ffn_call(x, mod, weights, alpha, tm):
    b_, t, d = x.shape
    return pl.pallas_call(
        functools.partial(_ffn_kernel, alpha=alpha),
        grid=(b_, t // tm),
        in_specs=[pl.BlockSpec((1, tm, d), lambda b, i: (b, i, 0)),
                  pl.BlockSpec((1, N_ADA, d), lambda b, i: (b, 0, 0))]
                 + [_resident(w.shape) for w in weights],
        out_specs=pl.BlockSpec((1, tm, d), lambda b, i: (b, i, 0)),
        out_shape=jax.ShapeDtypeStruct((b_, t, d), F32),
        compiler_params=_cparams("parallel", "parallel"),
        name="ffn",
    )(x, mod, *weights)


def _router_kernel(x_ref, mod_ref, wrh, wrl, br_ref, h_out, dense_out, pos_out, post_out, cnt_out):
    tm = x_ref.shape[1]
    h = _modulate(x_ref[0], mod_ref[0, 3:4, :], mod_ref[0, 4:5, :])
    h_out[0] = h.astype(BF16)
    hh, hl = _split2(h)
    logits = _dot3(hh, hl, wrh[...], wrl[...])
    lane = lax.broadcasted_iota(jnp.int32, logits.shape, 1).astype(F32)
    sel = jnp.where(lane < N_EXPERTS, logits + br_ref[...], -jnp.inf)

    def pick(scores):
        mx = jnp.max(scores, axis=1, keepdims=True)
        idx = jnp.min(jnp.where(scores == mx, lane, 2.0 * SMALL_W), axis=1, keepdims=True)
        return lane == idx

    first = pick(sel)
    second = pick(jnp.where(first, -jnp.inf, sel))
    l1 = jnp.sum(jnp.where(first, logits, 0.0), axis=1, keepdims=True)
    l2 = jnp.sum(jnp.where(second, logits, 0.0), axis=1, keepdims=True)
    mx = jnp.maximum(l1, l2)
    e1, e2 = jnp.exp(l1 - mx), jnp.exp(l2 - mx)
    inv = 1.0 / (e1 + e2)
    dense_out[0] = jnp.where(first, e1 * inv, 0.0) + jnp.where(second, e2 * inv, 0.0)
    routed = first | second
    counts = jnp.dot(_tri(tm, upper=False), jnp.where(routed, 1.0, 0.0).astype(BF16),
                     preferred_element_type=F32)
    pos = jnp.where(routed, counts - 1.0, -1.0)
    pos_out[0] = pos
    post_out[0] = pos.T[0:8, :]
    cnt_out[0, 0] = jnp.broadcast_to(counts[tm - 1:tm, :], (8, SMALL_W))


def _router_call(x, mod, wrh, wrl, br, tm):
    b_, t, d = x.shape
    nt = t // tm
    tok = lambda w: pl.BlockSpec((1, tm, w), lambda b, i: (b, i, 0))
    return pl.pallas_call(
        _router_kernel,
        grid=(b_, nt),
        in_specs=[tok(d), pl.BlockSpec((1, N_ADA, d), lambda b, i: (b, 0, 0)),
                  _resident(wrh.shape), _resident(wrl.shape), _resident(br.shape)],
        out_specs=[tok(d), tok(SMALL_W), tok(SMALL_W),
                   pl.BlockSpec((1, 8, tm), lambda b, i: (b, 0, i)),
                   pl.BlockSpec((1, 1, 8, SMALL_W), lambda b, i: (b, i, 0, 0))],
        out_shape=[jax.ShapeDtypeStruct((b_, t, d), BF16), jax.ShapeDtypeStruct((b_, t, SMALL_W), F32),
                   jax.ShapeDtypeStruct((b_, t, SMALL_W), F32), jax.ShapeDtypeStruct((b_, 8, t), F32),
                   jax.ShapeDtypeStruct((b_, nt, 8, SMALL_W), F32)],
        compiler_params=_cparams("parallel", "parallel"),
        name="router",
    )(x, mod, wrh, wrl, br)


def _moe_kernel(cnt_ref, x_ref, mod_ref, h_ref, dense_ref, pos_ref, post_ref, wg, wu, wd, lng, lnb,
                o_ref, acc_scr, *, alpha, rows):
    b, i, e = pl.program_id(0), pl.program_id(1), pl.program_id(2)
    tm = x_ref.shape[1]

    @pl.when(e == 0)
    def _():
        acc_scr[...] = jnp.zeros(acc_scr.shape, F32)

    lane = lax.broadcasted_iota(jnp.int32, (tm, SMALL_W), 1)
    w_col = jnp.sum(jnp.where(lane == e, dense_ref[0], 0.0), axis=1, keepdims=True)
    pos_col = jnp.sum(jnp.where(lane == e, pos_ref[0], 0.0), axis=1, keepdims=True)
    pos_row = post_ref[0, pl.ds(e, 1), :]
    count = cnt_ref[(b * pl.num_programs(1) + i) * N_EXPERTS + e]

    def run_block(first_row, n):
        base = first_row.astype(F32)
        r_iota = lax.broadcasted_iota(jnp.int32, (n, 1), 0).astype(F32) + base
        c_iota = lax.broadcasted_iota(jnp.int32, (1, n), 1).astype(F32) + base
        gather = jnp.where(pos_row == r_iota, 1.0, 0.0).astype(BF16)
        xs = jnp.dot(gather, h_ref[0], preferred_element_type=F32).astype(BF16)
        a = jnp.dot(xs, wg[0], preferred_element_type=F32)
        u = jnp.dot(xs, wu[0], preferred_element_type=F32)
        f = jnp.dot((_silu(a) * u).astype(BF16), wd[0], preferred_element_type=F32)
        scatter = jnp.where(pos_col == c_iota, 1.0, 0.0).astype(BF16)
        acc_scr[...] += w_col * jnp.dot(scatter, f.astype(BF16), preferred_element_type=F32)

    n_full = count // rows
    rem = count - n_full * rows
    half = rows // 2

    def full_block(j, carry):
        run_block(j * rows, rows)
        return carry

    lax.fori_loop(0, n_full, full_block, 0)

    @pl.when(rem > half)
    def _():
        run_block(n_full * rows, rows)

    @pl.when((rem > 0) & (rem <= half))
    def _():
        run_block(n_full * rows, half)

    @pl.when(e == pl.num_programs(2) - 1)
    def _():
        o_ref[0] = (_layernorm(alpha * x_ref[0] + mod_ref[0, 5:6, :] * acc_scr[...]) * lng[...]
                    + lnb[...])


def _moe_call(x, mod, wrh, wrl, br, wg, wu, wd, lng, lnb, alpha, tm):
    b_, t, d = x.shape
    n_e, _, ff = wg.shape
    nt = t // tm
    h, dense, pos, post, cnt = _router_call(x, mod, wrh, wrl, br, tm)
    counts = cnt[:, :, 0, 0:n_e].astype(jnp.int32).reshape(-1)
    tok = lambda w: pl.BlockSpec((1, tm, w), lambda b, i, e, c: (b, i, 0))
    grid_spec = pltpu.PrefetchScalarGridSpec(
        num_scalar_prefetch=1,
        grid=(b_, nt, n_e),
        in_specs=[tok(d), pl.BlockSpec((1, N_ADA, d), lambda b, i, e, c: (b, 0, 0)),
                  tok(d), tok(SMALL_W), tok(SMALL_W),
                  pl.BlockSpec((1, 8, tm), lambda b, i, e, c: (b, 0, i)),
                  pl.BlockSpec((1, d, ff), lambda b, i, e, c: (e, 0, 0)),
                  pl.BlockSpec((1, d, ff), lambda b, i, e, c: (e, 0, 0)),
                  pl.BlockSpec((1, ff, d), lambda b, i, e, c: (e, 0, 0)),
                  _resident(lng.shape), _resident(lnb.shape)],
        out_specs=tok(d),
        scratch_shapes=[pltpu.VMEM((tm, d), F32)])
    return pl.pallas_call(
        functools.partial(_moe_kernel, alpha=alpha, rows=min(MOE_ROWS, tm)),
        grid_spec=grid_spec,
        out_shape=jax.ShapeDtypeStruct((b_, t, d), F32),
        compiler_params=_cparams("parallel", "parallel", "arbitrary"),
        name="moe",
    )(counts, x, mod, h, dense, pos, post, wg, wu, wd, lng, lnb)


def _rope_tables(t):
    rows = t // GRID_W
    row = jnp.repeat(jnp.arange(rows, dtype=F32), GRID_W)
    col = jnp.tile(jnp.arange(GRID_W, dtype=F32), rows)
    inv = jnp.power(ROPE_BASE, -jnp.arange(ROPE_PAIRS, dtype=F32) / ROPE_PAIRS)
    ar, ac = row[:, None] * inv, col[:, None] * inv
    cos = jnp.concatenate([jnp.cos(ar), jnp.cos(ar), jnp.cos(ac), jnp.cos(ac)], axis=1)
    sin = jnp.concatenate([-jnp.sin(ar), jnp.sin(ar), -jnp.sin(ac), jnp.sin(ac)], axis=1)
    return jnp.tile(cos, (1, 2)), jnp.tile(sin, (1, 2))


def _token_tile(t, pref):
    return pref if t % pref == 0 else t


def kernel(x, c, ctx, c_ctx, w_ada, b_ada, w_in, ml_gate_b, ml_norm_g, gq_qnorm_g, gq_knorm_g, gl_w2, gl_b2,
           gl_norm_g, w_branch, w_gate, b_gate, w_out, ln1_g, ln1_b, ln2_g, ln2_b, ffd_wg, ffd_wu, ffd_wd,
           moe_wr, moe_br, moe_wg, moe_wu, moe_wd):
    b_, s_len, d = x.shape
    n_ctx = ctx.shape[1]
    depth = w_in.shape[0]
    alpha = (2.0 * depth) ** 0.25
    bf = lambda a: a.astype(BF16)
    row = lambda a: a.reshape(1, -1).astype(F32)

    cos_l, sin_l = _rope_tables(s_len)
    cos_c, sin_c = jnp.ones((n_ctx, 128), F32), jnp.zeros((n_ctx, 128), F32)
    ft_l, ft_c = _fourier_tables(s_len), _fourier_tables(n_ctx)
    tm_l, tm_c = _token_tile(s_len, 512), _token_tile(n_ctx, 512)

    cc = jnp.zeros((8, d), F32).at[0:b_].set(c).at[b_].set(c_ctx)
    xc = ctx
    for l in range(depth):
        last = l == depth - 1
        ada = _ada_call(cc, bf(w_ada[l]), row(b_ada[l])).reshape(8, N_ADA, d)
        mod_l = ada[0:b_]
        mod_c = jnp.broadcast_to(ada[b_][None], (b_, N_ADA, d))

        w = w_in[l]
        o_gate, o_ft, o_gq, o_gl, o_af = 4 * ML_W, 4 * ML_W + 16, 4 * ML_W + 16 + FT_W, \
            4 * ML_W + 16 + FT_W + GQ_W + 2 * GQ_KW, 4 * ML_W + 16 + FT_W + GQ_W + 2 * GQ_KW + 4 * GL_W
        ml_scale = jnp.concatenate([jnp.ones((ML_W,)), jnp.full((ML_W,), HEAD_DIM ** -0.5), jnp.ones((2 * ML_W,))])
        gl_scale = jnp.concatenate([jnp.full((GL_W,), HEAD_DIM ** -0.5), jnp.ones((3 * GL_W,))])
        w_small = jnp.concatenate([w[:, o_gate:o_gate + 16], w[:, o_af:o_af + 2 * GL_RANK],
                                   jnp.zeros((d, SMALL_W - 16 - 2 * GL_RANK), F32)], axis=1)
        ws = [bf(w[:, 0:4 * ML_W] * ml_scale), bf(w[:, o_ft:o_ft + FT_W]), bf(w[:, o_gq:o_gl]),
              bf(w[:, o_gl:o_af] * gl_scale), bf(w_small)]
        gate_bias = jnp.zeros((1, SMALL_W), F32).at[0, 0:16].set(ml_gate_b[l].reshape(-1))
        w2e = jnp.zeros((SMALL_W, 2 * GL_W), F32)
        w2e = w2e.at[16:16 + GL_RANK, 0:GL_W].set(gl_w2[l, 0]).at[16 + GL_RANK:16 + 2 * GL_RANK, GL_W:].set(gl_w2[l, 1])
        merge_w = [bf(w_gate[l]), row(b_gate[l]), bf(w_branch[l]), bf(w_out[l]), row(ml_norm_g[l]),
                   row(gl_norm_g[l]), row(ln1_g[l]), row(ln1_b[l])]
        gq_g = jnp.tile(row(gq_qnorm_g[l]), (1, 2))
        gk_g = jnp.tile(row(gq_knorm_g[l]), (1, 2))
        logit_bound = (1.02 * LOG2_E * HEAD_DIM ** 0.5) * jnp.max(jnp.abs(gq_qnorm_g[l])) * jnp.max(jnp.abs(gq_knorm_g[l]))
        score_bias = jnp.zeros((1, HEAD_DIM), F32).at[0, 0].set(-logit_bound)

        def mixers(xs, mod, tm, cos, sin, states):
            ml, ftu, gq, gl, small = _inproj_call(xs, mod, ws, tm)
            hf, hb, ml_s, ml_m = _mlstm_call(ml, small, gate_bias, states[0], states[1])
            of, ob, gl_s = _gla_call(gl, small, bf(w2e), gl_b2[l].astype(F32), states[2])
            qt, k, vt = _qkprep_call(gq, cos, sin, gq_g, gk_g, score_bias, tm)
            return dict(ml=ml, ftu=ftu, gl=gl, hf=hf, hb=hb, of=of, ob=ob, qt=qt, k=k, vt=vt), (ml_s, ml_m, gl_s)

        zero_states = (jnp.zeros((b_, 2 * ML_HEADS, 2 * HEAD_DIM, 2 * HEAD_DIM), F32),
                       jnp.zeros((b_, 2 * ML_HEADS, 1, SMALL_W), F32),
                       jnp.zeros((b_, 2, GL_W, GL_W), F32))
        pc, ctx_states = mixers(xc, mod_c, tm_c, cos_c, sin_c, zero_states)
        pl_, _ = mixers(x, mod_l, tm_l, cos_l, sin_l, ctx_states)

        att_l = _attn_call(pl_["qt"], jnp.concatenate([pc["k"], pl_["k"]], axis=2),
                           jnp.concatenate([pc["vt"], pl_["vt"]], axis=2), logit_bound)
        x_mid = _merge_call(x, mod_l, pl_["hf"], pl_["hb"], pl_["ml"], _fourier_call(pl_["ftu"], ft_l), att_l,
                            pl_["of"], pl_["ob"], pl_["gl"], merge_w, alpha, tm_l)
        if not last:
            att_c = _attn_call(pc["qt"], pc["k"], pc["vt"], logit_bound)
            xc = _merge_call(xc, mod_c, pc["hf"], pc["hb"], pc["ml"], _fourier_call(pc["ftu"], ft_c), att_c,
                             pc["of"], pc["ob"], pc["gl"], merge_w, alpha, tm_c)

        j = l // 2
        if l % 2 == 0:
            ffn_w = [bf(ffd_wg[j]), bf(ffd_wu[j]), bf(ffd_wd[j]), row(ln2_g[l]), row(ln2_b[l])]
            x = _ffn_call(x_mid, mod_l, ffn_w, alpha, tm_l)
            if not last:
                xc = _ffn_call(xc, mod_c, ffn_w, alpha, tm_c)
        else:
            wr = jnp.zeros((d, SMALL_W), F32).at[:, 0:N_EXPERTS].set(moe_wr[j])
            wrh = bf(wr)
            wrl = bf(wr - wrh.astype(F32))
            brp = jnp.zeros((1, SMALL_W), F32).at[0, 0:N_EXPERTS].set(moe_br[j])
            moe_args = (wrh, wrl, brp, bf(moe_wg[j]), bf(moe_wu[j]), bf(moe_wd[j]), row(ln2_g[l]), row(ln2_b[l]))
            x = _moe_call(x_mid, mod_l, *moe_args, alpha, _token_tile(s_len, 1024))
            if not last:
                xc = _moe_call(xc, mod_c, *moe_args, alpha, tm_c)
    return x
```

```python
import functools
import math

import jax
import jax.numpy as jnp
import numpy as np
from jax import lax
from jax.experimental import pallas as pl
from jax.experimental.pallas import tpu as pltpu

F32 = jnp.float32
BF16 = jnp.bfloat16

D_MODEL = 1024
GRID_W = 64
HEAD_DIM = 64
ML_HEADS = 4
ML_W = ML_HEADS * HEAD_DIM
FT_GROUPS = 4
FT_GC = 64
FT_W = FT_GROUPS * FT_GC
GQ_KV = 2
GQ_G = 4
GQ_W = GQ_KV * GQ_G * HEAD_DIM
GQ_KW = GQ_KV * HEAD_DIM
ROPE_PAIRS = HEAD_DIM // 4
ROPE_BASE = 10000.0
GL_HEADS = 4
GL_W = GL_HEADS * HEAD_DIM
GL_RANK = 16
GL_TAU = 16.0
N_EXPERTS = 8
N_ADA = 6
LN_EPS = 1e-6
SMALL_W = 128

ML_CHUNK = 256
GL_CHUNK = 128
GL_SUB = 16
GL_EXP_CLAMP = 80.0
ATT_TQ = 512
ATT_TK = 3328
ATT_ROWS = 256
LOG2_E = 1.4426950408889634
MAX_LOGIT_BOUND = 60.0
NEG_BIG = -1e30
MOE_ROWS = 256
FT_TOKENS_PER_STEP = 8
V_ROWS = 128

VMEM_LIMIT = 56 * 1024 * 1024


def _cparams(*sem):
    return pltpu.CompilerParams(dimension_semantics=sem, vmem_limit_bytes=VMEM_LIMIT)


def _resident(shape):
    nd = len(shape)
    return pl.BlockSpec(shape, lambda *_: (0,) * nd, pipeline_mode=pl.Buffered(1))


def _bdot(a, b):
    return jnp.dot(a.astype(BF16), b.astype(BF16), preferred_element_type=F32)


def _split2(x):
    hi = x.astype(BF16)
    lo = (x - hi.astype(F32)).astype(BF16)
    return hi, lo


def _split3(x):
    a = x.astype(BF16)
    r = x - a.astype(F32)
    b = r.astype(BF16)
    c = (r - b.astype(F32)).astype(BF16)
    return a, b, c


def _dot_exact_rhs(x, m_bf16):
    a, b, c = _split3(x)
    d = functools.partial(jnp.dot, preferred_element_type=F32)
    return d(a, m_bf16) + d(b, m_bf16) + d(c, m_bf16)


def _dot_exact_lhs(m_bf16, x):
    a, b, c = _split3(x)
    d = functools.partial(jnp.dot, preferred_element_type=F32)
    return d(m_bf16, a) + d(m_bf16, b) + d(m_bf16, c)


def _dot3(a_hi, a_lo, b_hi, b_lo):
    d = functools.partial(jnp.dot, preferred_element_type=F32)
    return d(a_hi, b_hi) + d(a_hi, b_lo) + d(a_lo, b_hi)


def _sigmoid(x):
    return 1.0 / (1.0 + jnp.exp(-x))


def _silu(x):
    return x * _sigmoid(x)


def _log_sigmoid(x):
    return jnp.minimum(x, 0.0) - jnp.log(1.0 + jnp.exp(-jnp.abs(x)))


def _layernorm(x):
    mu = jnp.mean(x, axis=-1, keepdims=True)
    xc = x - mu
    var = jnp.mean(xc * xc, axis=-1, keepdims=True)
    return xc * lax.rsqrt(var + LN_EPS)


def _modulate(x, shift, scale):
    return _layernorm(x) * (1.0 + scale) + shift


def _group_ones(width):
    r = lax.broadcasted_iota(jnp.int32, (width, width), 0) >> 6
    c = lax.broadcasted_iota(jnp.int32, (width, width), 1) >> 6
    return jnp.where(r == c, 1.0, 0.0).astype(BF16)


def _group_mean(x, ones):
    return _dot_exact_rhs(x, ones) * (1.0 / HEAD_DIM)


def _tri(n, upper):
    r = lax.broadcasted_iota(jnp.int32, (n, n), 0)
    c = lax.broadcasted_iota(jnp.int32, (n, n), 1)
    keep = (c >= r) if upper else (c <= r)
    return jnp.where(keep, 1.0, 0.0).astype(BF16)


def _ada_kernel(c_ref, w_ref, b_ref, o_ref):
    o_ref[...] = _bdot(_silu(c_ref[...]), w_ref[...]) + b_ref[...]


def _ada_call(cc, w, b):
    rows, d = cc.shape
    n = w.shape[1]
    tn = 1024
    return pl.pallas_call(
        _ada_kernel,
        grid=(n // tn,),
        in_specs=[pl.BlockSpec((rows, d), lambda j: (0, 0)),
                  pl.BlockSpec((d, tn), lambda j: (0, j)),
                  pl.BlockSpec((1, tn), lambda j: (0, j))],
        out_specs=pl.BlockSpec((rows, tn), lambda j: (0, j)),
        out_shape=jax.ShapeDtypeStruct((rows, n), F32),
        compiler_params=_cparams("parallel"),
        name="ada",
    )(cc, w, b)


def _inproj_kernel(x_ref, mod_ref, w_ml, w_ft, w_gq, w_gl, w_sm, o_ml, o_ft, o_gq, o_gl, o_sm):
    h = _modulate(x_ref[0], mod_ref[0, 0:1, :], mod_ref[0, 1:2, :]).astype(BF16)
    for w, o in ((w_ml, o_ml), (w_ft, o_ft), (w_gq, o_gq), (w_gl, o_gl), (w_sm, o_sm)):
        o[0] = jnp.dot(h, w[...], preferred_element_type=F32)


def _inproj_call(x, mod, ws, tm):
    b_, t, d = x.shape
    widths = [w.shape[1] for w in ws]
    return pl.pallas_call(
        _inproj_kernel,
        grid=(b_, t // tm),
        in_specs=[pl.BlockSpec((1, tm, d), lambda b, i: (b, i, 0)),
                  pl.BlockSpec((1, N_ADA, d), lambda b, i: (b, 0, 0))]
                 + [_resident(w.shape) for w in ws],
        out_specs=[pl.BlockSpec((1, tm, n), lambda b, i: (b, i, 0)) for n in widths],
        out_shape=[jax.ShapeDtypeStruct((b_, t, n), F32) for n in widths],
        compiler_params=_cparams("parallel", "parallel"),
        name="inproj",
    )(x, mod, *ws)


def _mlstm_kernel(qkv_f, sm_f, qkv_b, sm_b, bias_ref, s0_ref, m0_ref,
                  hf_ref, hb_ref, st_ref, mt_ref, s_scr, m_scr, *, chunk):
    i = pl.program_id(1)
    n_l = chunk

    @pl.when(i == 0)
    def _():
        s_scr[...] = s0_ref[0]
        m_scr[...] = m0_ref[0]

    row = lax.broadcasted_iota(jnp.int32, (n_l, n_l), 0)
    col = lax.broadcasted_iota(jnp.int32, (n_l, n_l), 1)
    lane = lax.broadcasted_iota(jnp.int32, (n_l, 128), 1)
    sub = lax.broadcasted_iota(jnp.int32, (HEAD_DIM, n_l), 0)
    ones_row = jnp.where(sub == 0, 1.0, 0.0)
    for d, (qkv_ref, sm_ref, h_ref) in enumerate(((qkv_f, sm_f, hf_ref), (qkv_b, sm_b, hb_ref))):
        rev = d == 1
        blk = qkv_ref[0]
        pre = sm_ref[0] + bias_ref[...]
        bcum = _dot_exact_lhs(_tri(n_l, upper=rev), _log_sigmoid(pre))
        pre_t = pre.T
        b_t = bcum.T
        q_t = blk[:, 0:ML_W].T.astype(BF16)
        v_t = blk[:, 2 * ML_W:3 * ML_W].T
        mask = (row >= col) if rev else (row <= col)
        last = 0 if rev else n_l - 1
        outs = []
        for h in range(ML_HEADS):
            ci = 8 * d + h
            cf = 8 * d + 4 + h
            idx = 4 * d + h
            pair = h // 2
            own = (lane >= HEAD_DIM) if h % 2 else (lane < HEAD_DIM)
            k_own = jnp.where(own, blk[:, ML_W + pair * 128:ML_W + (pair + 1) * 128], 0.0).astype(BF16)
            q_pair = q_t[pair * 128:(pair + 1) * 128, :]
            v_h = v_t[h * HEAD_DIM:(h + 1) * HEAD_DIM, :]
            c_col = pre[:, ci:ci + 1] - bcum[:, cf:cf + 1]
            b_row = b_t[cf:cf + 1, :]
            i_row = pre_t[ci:ci + 1, :]
            state = s_scr[idx]
            m_prev = m_scr[idx][:, 0:1]

            dmat = jnp.where(mask, b_row + c_col, -jnp.inf)
            inter = b_row + m_prev
            m_t = jnp.maximum(inter, jnp.max(dmat, axis=0, keepdims=True))
            w_intra = jnp.exp(dmat - m_t) * jnp.dot(k_own, q_pair, preferred_element_type=F32)
            w_inter = jnp.exp(inter - m_t)
            sq = jnp.dot(state.astype(BF16), q_pair, preferred_element_type=F32)
            num = (jnp.dot(v_h.astype(BF16), w_intra.astype(BF16), preferred_element_type=F32)
                   + w_inter * sq[0:HEAD_DIM])
            den = jnp.sum(w_intra, axis=0, keepdims=True) + w_inter * sq[HEAD_DIM:HEAD_DIM + 1]
            outs.append(num / jnp.maximum(jnp.abs(den), jnp.exp(-m_t)))

            b_last = b_row[:, last:last + 1]
            g_row = b_last - b_row + i_row
            m_new = jnp.maximum(b_last + m_prev, jnp.max(g_row, axis=1, keepdims=True))
            ws = jnp.exp(g_row - m_new)
            wc = jnp.exp(b_last + m_prev - m_new)
            v_ext = jnp.concatenate([v_h, ones_row], axis=0)
            s_scr[idx] = wc * state + jnp.dot((v_ext * ws).astype(BF16), k_own, preferred_element_type=F32)
            m_scr[idx] = jnp.broadcast_to(m_new, (1, SMALL_W))
        h_ref[0] = jnp.concatenate(outs, axis=0).T

    @pl.when(i == pl.num_programs(1) - 1)
    def _():
        st_ref[0] = s_scr[...]
        mt_ref[0] = m_scr[...]


def _mlstm_call(ml, small, bias, s0, m0):
    b_, t, _ = ml.shape
    chunk = min(ML_CHUNK, t)
    n = t // chunk
    fwd = lambda b, i: (b, i, 0)
    bwd = lambda b, i: (b, n - 1 - i, 0)
    state_spec = pl.BlockSpec((1, 2 * ML_HEADS, 2 * HEAD_DIM, 2 * HEAD_DIM), lambda b, i: (b, 0, 0, 0))
    m_spec = pl.BlockSpec((1, 2 * ML_HEADS, 1, SMALL_W), lambda b, i: (b, 0, 0, 0))
    return pl.pallas_call(
        functools.partial(_mlstm_kernel, chunk=chunk),
        grid=(b_, n),
        in_specs=[pl.BlockSpec((1, chunk, 3 * ML_W), fwd), pl.BlockSpec((1, chunk, SMALL_W), fwd),
                  pl.BlockSpec((1, chunk, 3 * ML_W), bwd), pl.BlockSpec((1, chunk, SMALL_W), bwd),
                  pl.BlockSpec((1, SMALL_W), lambda b, i: (0, 0)), state_spec, m_spec],
        out_specs=[pl.BlockSpec((1, chunk, ML_W), fwd), pl.BlockSpec((1, chunk, ML_W), bwd),
                   state_spec, m_spec],
        out_shape=[jax.ShapeDtypeStruct((b_, t, ML_W), F32), jax.ShapeDtypeStruct((b_, t, ML_W), F32),
                   jax.ShapeDtypeStruct(s0.shape, F32), jax.ShapeDtypeStruct(m0.shape, F32)],
        scratch_shapes=[pltpu.VMEM((2 * ML_HEADS, 2 * HEAD_DIM, 2 * HEAD_DIM), F32),
                        pltpu.VMEM((2 * ML_HEADS, 1, SMALL_W), F32)],
        compiler_params=_cparams("parallel", "arbitrary"),
        name="mlstm",
    )(ml, small, ml, small, bias, s0, m0)


def _gla_kernel(qkv_f, sm_f, qkv_b, sm_b, w2_ref, b2_ref, s0_ref,
                of_ref, ob_ref, st_ref, s_scr, *, chunk):
    i = pl.program_id(1)
    n_l = chunk
    n_sub = n_l // GL_SUB
    width = GL_W

    @pl.when(i == 0)
    def _():
        s_scr[...] = s0_ref[0]

    lane = lax.broadcasted_iota(jnp.int32, (GL_SUB, width), 1) >> 6
    head_masks = [lane == h for h in range(GL_HEADS)]
    chunk_lane = lax.broadcasted_iota(jnp.int32, (n_l, width), 1) >> 6
    chunk_masks = [chunk_lane == h for h in range(GL_HEADS)]
    bd_mask = ((lax.broadcasted_iota(jnp.int32, (width, width), 0) >> 6)
               == (lax.broadcasted_iota(jnp.int32, (width, width), 1) >> 6))
    states = [s_scr[0], s_scr[1]]
    new_states, new_outs = [], []
    for d, (qkv_ref, sm_ref, o_ref) in enumerate(((qkv_f, sm_f, of_ref), (qkv_b, sm_b, ob_ref))):
        rev = d == 1
        blk = qkv_ref[0]
        q = blk[:, 0:width]
        k = blk[:, width:2 * width]
        v = blk[:, 2 * width:3 * width]
        a = _bdot(sm_ref[0], w2_ref[...])[:, d * width:(d + 1) * width] + b2_ref[d:d + 1, :]
        la = _log_sigmoid(a) * (1.0 / GL_TAU)
        g = _dot_exact_lhs(_tri(n_l, upper=rev), la)
        last = 0 if rev else n_l - 1
        g_end = g[last:last + 1, :]
        state = states[d]
        o_inter = lax.dot_general((q * jnp.exp(g)).astype(BF16), state.astype(BF16),
                                  (((1,), (1,)), ((), ())), preferred_element_type=F32)
        v_bf = v.astype(BF16)
        a_blocks = []
        t_idx = lax.broadcasted_iota(jnp.int32, (GL_HEADS * GL_SUB, n_l), 0) & (GL_SUB - 1)
        s_idx = lax.broadcasted_iota(jnp.int32, (GL_HEADS * GL_SUB, n_l), 1)
        for s in range(n_sub):
            lo = s * GL_SUB
            hi = lo + GL_SUB
            r = g[hi - 1:hi, :] if rev else g[lo:lo + 1, :]
            qt = q[lo:hi] * jnp.exp(g[lo:hi] - r)
            kt = k * jnp.exp(jnp.minimum(r - g, GL_EXP_CLAMP))
            qstack = jnp.concatenate([jnp.where(hm, qt, 0.0) for hm in head_masks], axis=0)
            amat = lax.dot_general(qstack.astype(BF16), kt.astype(BF16),
                                   (((1,), (1,)), ((), ())), preferred_element_type=F32)
            keep = (s_idx >= lo + t_idx) if rev else (s_idx <= lo + t_idx)
            a_blocks.append(jnp.where(keep, amat, 0.0).astype(BF16))
        a_all = jnp.concatenate([a_blocks[s][h * GL_SUB:(h + 1) * GL_SUB]
                                 for h in range(GL_HEADS) for s in range(n_sub)], axis=0)
        ov = jnp.dot(a_all, v_bf, preferred_element_type=F32)
        o_intra = jnp.where(chunk_masks[0], ov[0:n_l], 0.0)
        for h in range(1, GL_HEADS):
            o_intra = o_intra + jnp.where(chunk_masks[h], ov[h * n_l:(h + 1) * n_l], 0.0)
        new_outs.append(o_inter + o_intra)

        kg = k * jnp.exp(g_end - g)
        upd = jnp.dot(v.T.astype(BF16), kg.astype(BF16), preferred_element_type=F32)
        new_states.append(jnp.exp(g_end) * state + jnp.where(bd_mask, upd, 0.0))
    of_ref[0] = new_outs[0]
    ob_ref[0] = new_outs[1]
    s_scr[0] = new_states[0]
    s_scr[1] = new_states[1]

    @pl.when(i == pl.num_programs(1) - 1)
    def _():
        st_ref[0] = s_scr[...]


def _gla_call(gl, small, w2e, b2, s0):
    b_, t, _ = gl.shape
    chunk = min(GL_CHUNK, t)
    n = t // chunk
    fwd = lambda b, i: (b, i, 0)
    bwd = lambda b, i: (b, n - 1 - i, 0)
    state_spec = pl.BlockSpec((1, 2, GL_W, GL_W), lambda b, i: (b, 0, 0, 0))
    return pl.pallas_call(
        functools.partial(_gla_kernel, chunk=chunk),
        grid=(b_, n),
        in_specs=[pl.BlockSpec((1, chunk, 3 * GL_W), fwd), pl.BlockSpec((1, chunk, SMALL_W), fwd),
                  pl.BlockSpec((1, chunk, 3 * GL_W), bwd), pl.BlockSpec((1, chunk, SMALL_W), bwd),
                  _resident(w2e.shape), _resident(b2.shape), state_spec],
        out_specs=[pl.BlockSpec((1, chunk, GL_W), fwd), pl.BlockSpec((1, chunk, GL_W), bwd), state_spec],
        out_shape=[jax.ShapeDtypeStruct((b_, t, GL_W), F32), jax.ShapeDtypeStruct((b_, t, GL_W), F32),
                   jax.ShapeDtypeStruct(s0.shape, F32)],
        scratch_shapes=[pltpu.VMEM((2, GL_W, GL_W), F32)],
        compiler_params=_cparams("parallel", "arbitrary"),
        name="gla",
    )(gl, small, gl, small, w2e, b2, s0)


def _qkprep_kernel(gq_ref, cos_ref, sin_ref, gq_g, gk_g, sb_ref, qt_out, k_out, vt_out):
    x = gq_ref[0]
    tm = x.shape[0]

    def norm_rope(z, g, reps):
        width = z.shape[1]
        msq = _group_mean(z * z, _group_ones(width))
        zn = z * lax.rsqrt(msq + LN_EPS) * jnp.tile(g, (1, reps))
        lane = lax.broadcasted_iota(jnp.int32, zn.shape, 1)
        partner = jnp.where((lane & 31) < ROPE_PAIRS,
                            pltpu.roll(zn, width - ROPE_PAIRS, axis=1),
                            pltpu.roll(zn, ROPE_PAIRS, axis=1))
        return zn * jnp.tile(cos_ref[...], (1, reps)) + partner * jnp.tile(sin_ref[...], (1, reps))

    lane = lax.broadcasted_iota(jnp.int32, (tm, 128), 1)

    def pad_heads(z, extra):
        out = []
        for p in range(z.shape[1] // 128):
            pair = z[:, p * 128:(p + 1) * 128]
            for base in (pair, pltpu.roll(pair, HEAD_DIM, axis=1)):
                out.append(jnp.where(lane < HEAD_DIM, base, jnp.where(lane == HEAD_DIM, extra, 0.0)))
        return out

    q = norm_rope(x[:, 0:GQ_W], gq_g[...], GQ_W // 128) * (LOG2_E * HEAD_DIM ** -0.5)
    qt_out[0] = jnp.concatenate(pad_heads(q, 1.0), axis=1).T.astype(BF16)
    k = norm_rope(x[:, GQ_W:GQ_W + GQ_KW], gk_g[...], GQ_KW // 128)
    for j, kj in enumerate(pad_heads(k, sb_ref[0:1, 0:1])):
        k_out[0, j] = kj.astype(BF16)
    v = x[:, GQ_W + GQ_KW:GQ_W + 2 * GQ_KW]
    v_t = v.T
    sub = lax.broadcasted_iota(jnp.int32, (V_ROWS - HEAD_DIM, tm), 0)
    ones_rows = jnp.where(sub == 0, 1.0, 0.0)
    vt_out[0] = jnp.concatenate([v_t[0:HEAD_DIM], ones_rows, v_t[HEAD_DIM:2 * HEAD_DIM], ones_rows],
                                axis=0).astype(BF16)


def _qkprep_call(gq, cos, sin, gq_g, gk_g, score_bias, tm):
    b_, t, w = gq.shape
    n_q = GQ_KV * GQ_G
    return pl.pallas_call(
        _qkprep_kernel,
        grid=(b_, t // tm),
        in_specs=[pl.BlockSpec((1, tm, w), lambda b, i: (b, i, 0)),
                  pl.BlockSpec((tm, 128), lambda b, i: (i, 0)),
                  pl.BlockSpec((tm, 128), lambda b, i: (i, 0)),
                  _resident(gq_g.shape), _resident(gk_g.shape), _resident(score_bias.shape)],
        out_specs=[pl.BlockSpec((1, n_q * 128, tm), lambda b, i: (b, 0, i)),
                   pl.BlockSpec((1, GQ_KV, tm, 128), lambda b, i: (b, 0, i, 0)),
                   pl.BlockSpec((1, GQ_KV * V_ROWS, tm), lambda b, i: (b, 0, i))],
        out_shape=[jax.ShapeDtypeStruct((b_, n_q * 128, t), BF16),
                   jax.ShapeDtypeStruct((b_, GQ_KV, t, 128), BF16),
                   jax.ShapeDtypeStruct((b_, GQ_KV * V_ROWS, t), BF16)],
        compiler_params=_cparams("parallel", "parallel"),
        name="qkprep",
    )(gq, cos, sin, gq_g, gk_g, score_bias)


def _attn_kernel(qt_ref, k_ref, vt_ref, o_ref, m_scr, acc_scr, *, n_tiles, tk, bounded):
    acc_scr[...] = jnp.zeros(acc_scr.shape, F32)
    if not bounded:
        m_scr[...] = jnp.full(m_scr.shape, NEG_BIG, F32)

    def body(j, carry):
        start = pl.multiple_of(j * tk, tk)
        k_tile = k_ref[0, 0, pl.ds(start, tk), :]
        vt_tile = vt_ref[0, :, pl.ds(start, tk)]
        for g in range(GQ_G):
            st = jnp.dot(k_tile, qt_ref[0, g * 128:(g + 1) * 128, :], preferred_element_type=F32)
            if bounded:
                acc_scr[g] += jnp.dot(vt_tile, jnp.exp2(st).astype(BF16), preferred_element_type=F32)
            else:
                m_prev = m_scr[g]
                m_new = jnp.maximum(m_prev, jnp.max(st, axis=0, keepdims=True))
                p = jnp.exp2(st - m_new).astype(BF16)
                acc_scr[g] = (jnp.exp2(m_prev - m_new) * acc_scr[g]
                              + jnp.dot(vt_tile, p, preferred_element_type=F32))
                m_scr[g] = m_new
        return carry

    lax.fori_loop(0, n_tiles, body, 0)
    outs = []
    for g in range(GQ_G):
        acc = acc_scr[g]
        outs.append(acc[0:HEAD_DIM] / acc[HEAD_DIM:HEAD_DIM + 1])
    o_ref[0] = jnp.concatenate(outs, axis=0).T


def _attn_call(qt, k, vt, logit_bound):
    b_, _, t = qt.shape
    n_keys = k.shape[2]
    tq = min(ATT_TQ, t)
    tk = ATT_TK if n_keys % ATT_TK == 0 else n_keys
    gw = GQ_G * HEAD_DIM

    def call(bounded):
        return pl.pallas_call(
            functools.partial(_attn_kernel, n_tiles=n_keys // tk, tk=tk, bounded=bounded),
            grid=(b_, GQ_KV, t // tq),
            in_specs=[pl.BlockSpec((1, GQ_G * 128, tq), lambda b, kv, i: (b, kv, i)),
                      pl.BlockSpec((1, 1, n_keys, 128), lambda b, kv, i: (b, kv, 0, 0)),
                      pl.BlockSpec((1, V_ROWS, n_keys), lambda b, kv, i: (b, kv, 0))],
            out_specs=pl.BlockSpec((1, tq, gw), lambda b, kv, i: (b, i, kv)),
            out_shape=jax.ShapeDtypeStruct((b_, t, GQ_W), F32),
            scratch_shapes=[pltpu.VMEM((GQ_G, 1, tq), F32),
                            pltpu.VMEM((GQ_G, V_ROWS, tq), F32)],
            compiler_params=_cparams("parallel", "parallel", "arbitrary"),
            name="attention_bounded" if bounded else "attention_online",
        )(qt, k, vt)

    return lax.cond(logit_bound <= MAX_LOGIT_BOUND, lambda: call(True), lambda: call(False))


def _fourier_factors(t):
    bits = int(round(math.log2(t)))
    assert 1 << bits == t
    n1 = 1 << (bits // 2)
    return n1, t // n1


def _hi_lo(a):
    a = np.asarray(a, np.float64)
    hi = jnp.asarray(a, F32).astype(BF16)
    lo = (jnp.asarray(a, F32) - hi.astype(F32)).astype(BF16)
    return hi, lo


def _fourier_tables(t):
    n1, n2 = _fourier_factors(t)
    c = np.arange(FT_GC)
    ang = 2.0 * np.pi * np.outer(c, c) / FT_GC
    eye = np.eye(FT_GROUPS)
    w0 = np.concatenate([np.kron(eye, np.cos(ang)), -np.kron(eye, np.sin(ang))], axis=1)
    a1 = 2.0 * np.pi * np.outer(np.arange(n1), np.arange(n1)) / n1
    fr, fi = np.cos(a1), -np.sin(a1)
    m1 = np.block([[fr, -fi], [fi, fr]])
    tw = 2.0 * np.pi * np.outer(np.arange(n1), np.arange(n2)) / t
    a2 = 2.0 * np.pi * np.outer(np.arange(n2), np.arange(n2)) / n2
    m2 = np.concatenate([np.cos(a2), np.sin(a2)], axis=1) / math.sqrt(t * FT_GC)
    return dict(n1=n1, n2=n2, w0=_hi_lo(w0), m1=_hi_lo(m1), m2=_hi_lo(m2),
                twr=np.cos(tw).astype(np.float32), twi=(-np.sin(tw)).astype(np.float32))


def _ft_stage1_kernel(u_ref, w0h, w0l, m1h, m1l, twr_ref, twi_ref, yr_ref, yi_ref, *, n_tok):
    n1 = u_ref.shape[1]
    zr, zi = [], []
    for j in range(n_tok):
        uh, ul = _split2(u_ref[0, :, j * FT_W:(j + 1) * FT_W])
        z = _dot3(uh, ul, w0h[...], w0l[...])
        zr.append(z[:, :FT_W])
        zi.append(z[:, FT_W:])
    z = jnp.concatenate([jnp.concatenate(zr, axis=1), jnp.concatenate(zi, axis=1)], axis=0)
    zh, zl = _split2(z)
    y = _dot3(m1h[...], m1l[...], zh, zl)
    y_r, y_i = y[:n1], y[n1:]
    twr, twi = twr_ref[0], twi_ref[0]
    for j in range(n_tok):
        sl = slice(j * FT_W, (j + 1) * FT_W)
        cr, ci = twr[:, j:j + 1], twi[:, j:j + 1]
        yr_ref[0, :, j, :] = y_r[:, sl] * cr - y_i[:, sl] * ci
        yi_ref[0, :, j, :] = y_r[:, sl] * ci + y_i[:, sl] * cr


def _ft_stage2_kernel(yr_ref, yi_ref, m2h, m2l, o_ref, *, n_k1, n2):
    for j in range(n_k1):
        y = jnp.concatenate([yr_ref[0, j * n2:(j + 1) * n2, :], yi_ref[0, j * n2:(j + 1) * n2, :]], axis=0)
        yh, yl = _split2(y)
        o_ref[0, :, j, :] = _dot3(m2h[...], m2l[...], yh, yl)


def _fourier_call(u, tabs):
    b_, t, w = u.shape
    n1, n2 = tabs["n1"], tabs["n2"]
    n_tok = min(FT_TOKENS_PER_STEP, n2)
    twr = jnp.asarray(tabs["twr"]).reshape(n1, n2 // n_tok, n_tok).transpose(1, 0, 2)
    twi = jnp.asarray(tabs["twi"]).reshape(n1, n2 // n_tok, n_tok).transpose(1, 0, 2)
    grid_view = jax.ShapeDtypeStruct((b_, n1, n2, w), F32)
    tok_block = pl.BlockSpec((1, n1, n_tok, w), lambda b, j: (b, 0, j, 0))
    yr, yi = pl.pallas_call(
        functools.partial(_ft_stage1_kernel, n_tok=n_tok),
        grid=(b_, n2 // n_tok),
        in_specs=[pl.BlockSpec((1, n1, n_tok * w), lambda b, j: (b, 0, j)),
                  _resident(tabs["w0"][0].shape), _resident(tabs["w0"][1].shape),
                  _resident(tabs["m1"][0].shape), _resident(tabs["m1"][1].shape),
                  pl.BlockSpec((1, n1, n_tok), lambda b, j: (j, 0, 0)),
                  pl.BlockSpec((1, n1, n_tok), lambda b, j: (j, 0, 0))],
        out_specs=[tok_block] * 2,
        out_shape=[grid_view, grid_view],
        compiler_params=_cparams("parallel", "parallel"),
        name="fourier_stage1",
    )(u.reshape(b_, n1, n2 * w), *tabs["w0"], *tabs["m1"], twr, twi)
    n_k1 = min(FT_TOKENS_PER_STEP, n1)
    out = pl.pallas_call(
        functools.partial(_ft_stage2_kernel, n_k1=n_k1, n2=n2),
        grid=(b_, n1 // n_k1),
        in_specs=[pl.BlockSpec((1, n_k1 * n2, w), lambda b, j: (b, j, 0)),
                  pl.BlockSpec((1, n_k1 * n2, w), lambda b, j: (b, j, 0)),
                  _resident(tabs["m2"][0].shape), _resident(tabs["m2"][1].shape)],
        out_specs=pl.BlockSpec((1, n2, n_k1, w), lambda b, j: (b, 0, j, 0)),
        out_shape=jax.ShapeDtypeStruct((b_, n2, n1, w), F32),
        compiler_params=_cparams("parallel", "parallel"),
        name="fourier_stage2",
    )(yr.reshape(b_, t, w), yi.reshape(b_, t, w), *tabs["m2"])
    return out.reshape(b_, t, w)


def _merge_kernel(x_ref, mod_ref, hf_ref, hb_ref, mlo_ref, ft_ref, att_ref, of_ref, ob_ref, glr_ref,
                  wgate, bgate, wbr, wout, mlg, glg, lng, lnb, o_ref, *, alpha):
    x = x_ref[0]
    h = _modulate(x, mod_ref[0, 0:1, :], mod_ref[0, 1:2, :]).astype(BF16)
    ones = _group_ones(ML_W)
    hs = hf_ref[0] + hb_ref[0]
    hc = hs - _group_mean(hs, ones)
    hn = hc * lax.rsqrt(_group_mean(hc * hc, ones) + LN_EPS) * mlg[...]
    br_ml = hn * _sigmoid(mlo_ref[0])
    os_ = of_ref[0] + ob_ref[0]
    on = os_ * lax.rsqrt(_group_mean(os_ * os_, ones) + LN_EPS) * glg[...]
    br_gl = on * _silu(glr_ref[0])
    branches = (br_ml, ft_ref[0], att_ref[0], br_gl)
    mixed = None
    off = 0
    for j, br in enumerate(branches):
        wd = br.shape[1]
        gate = _sigmoid(jnp.dot(h, wgate[:, j * D_MODEL:(j + 1) * D_MODEL], preferred_element_type=F32)
                        + bgate[:, j * D_MODEL:(j + 1) * D_MODEL])
        u = gate * jnp.dot(br.astype(BF16), wbr[off:off + wd, :], preferred_element_type=F32)
        mixed = u if mixed is None else mixed + u
        off += wd
    y = jnp.dot(mixed.astype(BF16), wout[...], preferred_element_type=F32)
    o_ref[0] = _layernorm(alpha * x + mod_ref[0, 2:3, :] * y) * lng[...] + lnb[...]


def _merge_call(x, mod, hf, hb, ml, ftb, att, of, ob, gl, weights, alpha, tm):
    b_, t, d = x.shape
    tok = lambda w: pl.BlockSpec((1, tm, w), lambda b, i: (b, i, 0))
    last_quarter = pl.BlockSpec((1, tm, ML_W), lambda b, i: (b, i, 3))
    return pl.pallas_call(
        functools.partial(_merge_kernel, alpha=alpha),
        grid=(b_, t // tm),
        in_specs=[tok(d), pl.BlockSpec((1, N_ADA, d), lambda b, i: (b, 0, 0)),
                  tok(ML_W), tok(ML_W), last_quarter, tok(FT_W), tok(GQ_W), tok(GL_W), tok(GL_W),
                  last_quarter] + [_resident(w.shape) for w in weights],
        out_specs=tok(d),
        out_shape=jax.ShapeDtypeStruct((b_, t, d), F32),
        compiler_params=_cparams("parallel", "parallel"),
        name="merge",
    )(x, mod, hf, hb, ml, ftb, att, of, ob, gl, *weights)


def _ffn_kernel(x_ref, mod_ref, wg, wu, wd, lng, lnb, o_ref, *, alpha):
    x = x_ref[0]
    h = _modulate(x, mod_ref[0, 3:4, :], mod_ref[0, 4:5, :]).astype(BF16)
    a = jnp.dot(h, wg[...], preferred_element_type=F32)
    u = jnp.dot(h, wu[...], preferred_element_type=F32)
    f = jnp.dot((_silu(a) * u).astype(BF16), wd[...], preferred_element_type=F32)
    o_ref[0] = _layernorm(alpha * x + mod_ref[0, 5:6, :] * f) * lng[...] + lnb[...]


def _ffn_call(x, mod, weights, alpha, tm):
    b_, t, d = x.shape
    return pl.pallas_call(
        functools.partial(_ffn_kernel, alpha=alpha),
        grid=(b_, t // tm),
        in_specs=[pl.BlockSpec((1, tm, d), lambda b, i: (b, i, 0)),
                  pl.BlockSpec((1, N_ADA, d), lambda b, i: (b, 0, 0))]
                 + [_resident(w.shape) for w in weights],
        out_specs=pl.BlockSpec((1, tm, d), lambda b, i: (b, i, 0)),
        out_shape=jax.ShapeDtypeStruct((b_, t, d), F32),
        compiler_params=_cparams("parallel", "parallel"),
        name="ffn",
    )(x, mod, *weights)


def _router_kernel(x_ref, mod_ref, wrh, wrl, br_ref, h_out, dense_out, pos_out, post_out, cnt_out):
    tm = x_ref.shape[1]
    h = _modulate(x_ref[0], mod_ref[0, 3:4, :], mod_ref[0, 4:5, :])
    h_out[0] = h.astype(BF16)
    hh, hl = _split2(h)
    logits = _dot3(hh, hl, wrh[...], wrl[...])
    lane = lax.broadcasted_iota(jnp.int32, logits.shape, 1).astype(F32)
    sel = jnp.where(lane < N_EXPERTS, logits + br_ref[...], -jnp.inf)

    def pick(scores):
        mx = jnp.max(scores, axis=1, keepdims=True)
        idx = jnp.min(jnp.where(scores == mx, lane, 2.0 * SMALL_W), axis=1, keepdims=True)
        return lane == idx

    first = pick(sel)
    second = pick(jnp.where(first, -jnp.inf, sel))
    l1 = jnp.sum(jnp.where(first, logits, 0.0), axis=1, keepdims=True)
    l2 = jnp.sum(jnp.where(second, logits, 0.0), axis=1, keepdims=True)
    mx = jnp.maximum(l1, l2)
    e1, e2 = jnp.exp(l1 - mx), jnp.exp(l2 - mx)
    inv = 1.0 / (e1 + e2)
    dense_out[0] = jnp.where(first, e1 * inv, 0.0) + jnp.where(second, e2 * inv, 0.0)
    routed = first | second
    counts = jnp.dot(_tri(tm, upper=False), jnp.where(routed, 1.0, 0.0).astype(BF16),
                     preferred_element_type=F32)
    pos = jnp.where(routed, counts - 1.0, -1.0)
    pos_out[0] = pos
    post_out[0] = pos.T[0:8, :]
    cnt_out[0, 0] = jnp.broadcast_to(counts[tm - 1:tm, :], (8, SMALL_W))


def _router_call(x, mod, wrh, wrl, br, tm):
    b_, t, d = x.shape
    nt = t // tm
    tok = lambda w: pl.BlockSpec((1, tm, w), lambda b, i: (b, i, 0))
    return pl.pallas_call(
        _router_kernel,
        grid=(b_, nt),
        in_specs=[tok(d), pl.BlockSpec((1, N_ADA, d), lambda b, i: (b, 0, 0)),
                  _resident(wrh.shape), _resident(wrl.shape), _resident(br.shape)],
        out_specs=[tok(d), tok(SMALL_W), tok(SMALL_W),
                   pl.BlockSpec((1, 8, tm), lambda b, i: (b, 0, i)),
                   pl.BlockSpec((1, 1, 8, SMALL_W), lambda b, i: (b, i, 0, 0))],
        out_shape=[jax.ShapeDtypeStruct((b_, t, d), BF16), jax.ShapeDtypeStruct((b_, t, SMALL_W), F32),
                   jax.ShapeDtypeStruct((b_, t, SMALL_W), F32), jax.ShapeDtypeStruct((b_, 8, t), F32),
                   jax.ShapeDtypeStruct((b_, nt, 8, SMALL_W), F32)],
        compiler_params=_cparams("parallel", "parallel"),
        name="router",
    )(x, mod, wrh, wrl, br)


def _moe_kernel(cnt_ref, x_ref, mod_ref, h_ref, dense_ref, pos_ref, post_ref, wg, wu, wd, lng, lnb,
                o_ref, acc_scr, *, alpha, rows):
    b, i, e = pl.program_id(0), pl.program_id(1), pl.program_id(2)
    tm = x_ref.shape[1]

    @pl.when(e == 0)
    def _():
        acc_scr[...] = jnp.zeros(acc_scr.shape, F32)

    lane = lax.broadcasted_iota(jnp.int32, (tm, SMALL_W), 1)
    w_col = jnp.sum(jnp.where(lane == e, dense_ref[0], 0.0), axis=1, keepdims=True)
    pos_col = jnp.sum(jnp.where(lane == e, pos_ref[0], 0.0), axis=1, keepdims=True)
    pos_row = post_ref[0, pl.ds(e, 1), :]
    count = cnt_ref[(b * pl.num_programs(1) + i) * N_EXPERTS + e]

    def run_block(first_row, n):
        base = first_row.astype(F32)
        r_iota = lax.broadcasted_iota(jnp.int32, (n, 1), 0).astype(F32) + base
        c_iota = lax.broadcasted_iota(jnp.int32, (1, n), 1).astype(F32) + base
        gather = jnp.where(pos_row == r_iota, 1.0, 0.0).astype(BF16)
        xs = jnp.dot(gather, h_ref[0], preferred_element_type=F32).astype(BF16)
        a = jnp.dot(xs, wg[0], preferred_element_type=F32)
        u = jnp.dot(xs, wu[0], preferred_element_type=F32)
        f = jnp.dot((_silu(a) * u).astype(BF16), wd[0], preferred_element_type=F32)
        scatter = jnp.where(pos_col == c_iota, 1.0, 0.0).astype(BF16)
        acc_scr[...] += w_col * jnp.dot(scatter, f.astype(BF16), preferred_element_type=F32)

    n_full = count // rows
    rem = count - n_full * rows
    half = rows // 2

    def full_block(j, carry):
        run_block(j * rows, rows)
        return carry

    lax.fori_loop(0, n_full, full_block, 0)

    @pl.when(rem > half)
    def _():
        run_block(n_full * rows, rows)

    @pl.when((rem > 0) & (rem <= half))
    def _():
        run_block(n_full * rows, half)

    @pl.when(e == pl.num_programs(2) - 1)
    def _():
        o_ref[0] = (_layernorm(alpha * x_ref[0] + mod_ref[0, 5:6, :] * acc_scr[...]) * lng[...]
                    + lnb[...])


def _moe_call(x, mod, wrh, wrl, br, wg, wu, wd, lng, lnb, alpha, tm):
    b_, t, d = x.shape
    n_e, _, ff = wg.shape
    nt = t // tm
    h, dense, pos, post, cnt = _router_call(x, mod, wrh, wrl, br, tm)
    counts = cnt[:, :, 0, 0:n_e].astype(jnp.int32).reshape(-1)
    tok = lambda w: pl.BlockSpec((1, tm, w), lambda b, i, e, c: (b, i, 0))
    grid_spec = pltpu.PrefetchScalarGridSpec(
        num_scalar_prefetch=1,
        grid=(b_, nt, n_e),
        in_specs=[tok(d), pl.BlockSpec((1, N_ADA, d), lambda b, i, e, c: (b, 0, 0)),
                  tok(d), tok(SMALL_W), tok(SMALL_W),
                  pl.BlockSpec((1, 8, tm), lambda b, i, e, c: (b, 0, i)),
                  pl.BlockSpec((1, d, ff), lambda b, i, e, c: (e, 0, 0)),
                  pl.BlockSpec((1, d, ff), lambda b, i, e, c: (e, 0, 0)),
                  pl.BlockSpec((1, ff, d), lambda b, i, e, c: (e, 0, 0)),
                  _resident(lng.shape), _resident(lnb.shape)],
        out_specs=tok(d),
        scratch_shapes=[pltpu.VMEM((tm, d), F32)])
    return pl.pallas_call(
        functools.partial(_moe_kernel, alpha=alpha, rows=min(MOE_ROWS, tm)),
        grid_spec=grid_spec,
        out_shape=jax.ShapeDtypeStruct((b_, t, d), F32),
        compiler_params=_cparams("parallel", "parallel", "arbitrary"),
        name="moe",
    )(counts, x, mod, h, dense, pos, post, wg, wu, wd, lng, lnb)


def _rope_tables(t):
    rows = t // GRID_W
    row = jnp.repeat(jnp.arange(rows, dtype=F32), GRID_W)
    col = jnp.tile(jnp.arange(GRID_W, dtype=F32), rows)
    inv = jnp.power(ROPE_BASE, -jnp.arange(ROPE_PAIRS, dtype=F32) / ROPE_PAIRS)
    ar, ac = row[:, None] * inv, col[:, None] * inv
    cos = jnp.concatenate([jnp.cos(ar), jnp.cos(ar), jnp.cos(ac), jnp.cos(ac)], axis=1)
    sin = jnp.concatenate([-jnp.sin(ar), jnp.sin(ar), -jnp.sin(ac), jnp.sin(ac)], axis=1)
    return jnp.tile(cos, (1, 2)), jnp.tile(sin, (1, 2))


def _token_tile(t, pref):
    return pref if t % pref == 0 else t


def kernel(x, c, ctx, c_ctx, w_ada, b_ada, w_in, ml_gate_b, ml_norm_g, gq_qnorm_g, gq_knorm_g, gl_w2, gl_b2,
           gl_norm_g, w_branch, w_gate, b_gate, w_out, ln1_g, ln1_b, ln2_g, ln2_b, ffd_wg, ffd_wu, ffd_wd,
           moe_wr, moe_br, moe_wg, moe_wu, moe_wd):
    b_, s_len, d = x.shape
    n_ctx = ctx.shape[1]
    depth = w_in.shape[0]
    alpha = (2.0 * depth) ** 0.25
    bf = lambda a: a.astype(BF16)
    row = lambda a: a.reshape(1, -1).astype(F32)

    cos_l, sin_l = _rope_tables(s_len)
    cos_c, sin_c = jnp.ones((n_ctx, 128), F32), jnp.zeros((n_ctx, 128), F32)
    ft_l, ft_c = _fourier_tables(s_len), _fourier_tables(n_ctx)
    tm_l, tm_c = _token_tile(s_len, 512), _token_tile(n_ctx, 512)

    cc = jnp.zeros((8, d), F32).at[0:b_].set(c).at[b_].set(c_ctx)
    xc = ctx
    for l in range(depth):
        last = l == depth - 1
        ada = _ada_call(cc, bf(w_ada[l]), row(b_ada[l])).reshape(8, N_ADA, d)
        mod_l = ada[0:b_]
        mod_c = jnp.broadcast_to(ada[b_][None], (b_, N_ADA, d))

        w = w_in[l]
        o_gate, o_ft, o_gq, o_gl, o_af = 4 * ML_W, 4 * ML_W + 16, 4 * ML_W + 16 + FT_W, \
            4 * ML_W + 16 + FT_W + GQ_W + 2 * GQ_KW, 4 * ML_W + 16 + FT_W + GQ_W + 2 * GQ_KW + 4 * GL_W
        ml_scale = jnp.concatenate([jnp.ones((ML_W,)), jnp.full((ML_W,), HEAD_DIM ** -0.5), jnp.ones((2 * ML_W,))])
        gl_scale = jnp.concatenate([jnp.full((GL_W,), HEAD_DIM ** -0.5), jnp.ones((3 * GL_W,))])
        w_small = jnp.concatenate([w[:, o_gate:o_gate + 16], w[:, o_af:o_af + 2 * GL_RANK],
                                   jnp.zeros((d, SMALL_W - 16 - 2 * GL_RANK), F32)], axis=1)
        ws = [bf(w[:, 0:4 * ML_W] * ml_scale), bf(w[:, o_ft:o_ft + FT_W]), bf(w[:, o_gq:o_gl]),
              bf(w[:, o_gl:o_af] * gl_scale), bf(w_small)]
        gate_bias = jnp.zeros((1, SMALL_W), F32).at[0, 0:16].set(ml_gate_b[l].reshape(-1))
        w2e = jnp.zeros((SMALL_W, 2 * GL_W), F32)
        w2e = w2e.at[16:16 + GL_RANK, 0:GL_W].set(gl_w2[l, 0]).at[16 + GL_RANK:16 + 2 * GL_RANK, GL_W:].set(gl_w2[l, 1])
        merge_w = [bf(w_gate[l]), row(b_gate[l]), bf(w_branch[l]), bf(w_out[l]), row(ml_norm_g[l]),
                   row(gl_norm_g[l]), row(ln1_g[l]), row(ln1_b[l])]
        gq_g = jnp.tile(row(gq_qnorm_g[l]), (1, 2))
        gk_g = jnp.tile(row(gq_knorm_g[l]), (1, 2))
        logit_bound = (1.02 * LOG2_E * HEAD_DIM ** 0.5) * jnp.max(jnp.abs(gq_qnorm_g[l])) * jnp.max(jnp.abs(gq_knorm_g[l]))
        score_bias = jnp.zeros((1, HEAD_DIM), F32).at[0, 0].set(-logit_bound)

        def mixers(xs, mod, tm, cos, sin, states):
            ml, ftu, gq, gl, small = _inproj_call(xs, mod, ws, tm)
            hf, hb, ml_s, ml_m = _mlstm_call(ml, small, gate_bias, states[0], states[1])
            of, ob, gl_s = _gla_call(gl, small, bf(w2e), gl_b2[l].astype(F32), states[2])
            qt, k, vt = _qkprep_call(gq, cos, sin, gq_g, gk_g, score_bias, tm)
            return dict(ml=ml, ftu=ftu, gl=gl, hf=hf, hb=hb, of=of, ob=ob, qt=qt, k=k, vt=vt), (ml_s, ml_m, gl_s)

        zero_states = (jnp.zeros((b_, 2 * ML_HEADS, 2 * HEAD_DIM, 2 * HEAD_DIM), F32),
                       jnp.zeros((b_, 2 * ML_HEADS, 1, SMALL_W), F32),
                       jnp.zeros((b_, 2, GL_W, GL_W), F32))
        pc, ctx_states = mixers(xc, mod_c, tm_c, cos_c, sin_c, zero_states)
        pl_, _ = mixers(x, mod_l, tm_l, cos_l, sin_l, ctx_states)

        att_l = _attn_call(pl_["qt"], jnp.concatenate([pc["k"], pl_["k"]], axis=2),
                           jnp.concatenate([pc["vt"], pl_["vt"]], axis=2), logit_bound)
        x_mid = _merge_call(x, mod_l, pl_["hf"], pl_["hb"], pl_["ml"], _fourier_call(pl_["ftu"], ft_l), att_l,
                            pl_["of"], pl_["ob"], pl_["gl"], merge_w, alpha, tm_l)
        if not last:
            att_c = _attn_call(pc["qt"], pc["k"], pc["vt"], logit_bound)
            xc = _merge_call(xc, mod_c, pc["hf"], pc["hb"], pc["ml"], _fourier_call(pc["ftu"], ft_c), att_c,
                             pc["of"], pc["ob"], pc["gl"], merge_w, alpha, tm_c)

        j = l // 2
        if l % 2 == 0:
            ffn_w = [bf(ffd_wg[j]), bf(ffd_wu[j]), bf(ffd_wd[j]), row(ln2_g[l]), row(ln2_b[l])]
            x = _ffn_call(x_mid, mod_l, ffn_w, alpha, tm_l)
            if not last:
                xc = _ffn_call(xc, mod_c, ffn_w, alpha, tm_c)
        else:
            wr = jnp.zeros((d, SMALL_W), F32).at[:, 0:N_EXPERTS].set(moe_wr[j])
            wrh = bf(wr)
            wrl = bf(wr - wrh.astype(F32))
            brp = jnp.zeros((1, SMALL_W), F32).at[0, 0:N_EXPERTS].set(moe_br[j])
            moe_args = (wrh, wrl, brp, bf(moe_wg[j]), bf(moe_wu[j]), bf(moe_wd[j]), row(ln2_g[l]), row(ln2_b[l]))
            x = _moe_call(x_mid, mod_l, *moe_args, alpha, _token_tile(s_len, 1024))
            if not last:
                xc = _moe_call(xc, mod_c, *moe_args, alpha, tm_c)
    return x
```

```python
import functools
import math

import jax
import jax.numpy as jnp
import numpy as np
from jax import lax
from jax.experimental import pallas as pl
from jax.experimental.pallas import tpu as pltpu

F32 = jnp.float32
BF16 = jnp.bfloat16

D_MODEL = 1024
GRID_W = 64
HEAD_DIM = 64
ML_HEADS = 4
ML_W = ML_HEADS * HEAD_DIM
FT_GROUPS = 4
FT_GC = 64
FT_W = FT_GROUPS * FT_GC
GQ_KV = 2
GQ_G = 4
GQ_W = GQ_KV * GQ_G * HEAD_DIM
GQ_KW = GQ_KV * HEAD_DIM
ROPE_PAIRS = HEAD_DIM // 4
ROPE_BASE = 10000.0
GL_HEADS = 4
GL_W = GL_HEADS * HEAD_DIM
GL_RANK = 16
GL_TAU = 16.0
N_EXPERTS = 8
N_ADA = 6
LN_EPS = 1e-6
SMALL_W = 128

ML_CHUNK = 256
GL_CHUNK = 128
GL_SUB = 16
GL_EXP_CLAMP = 80.0
ATT_TQ = 512
ATT_TK = 8320
ATT_ROWS = 256
LOG2_E = 1.4426950408889634
MAX_LOGIT_BOUND = 60.0
NEG_BIG = -1e30
MOE_ROWS = 256
FT_TOKENS_PER_STEP = 8
V_ROWS = 128

VMEM_LIMIT = 56 * 1024 * 1024


def _cparams(*sem):
    return pltpu.CompilerParams(dimension_semantics=sem, vmem_limit_bytes=VMEM_LIMIT)


def _resident(shape):
    nd = len(shape)
    return pl.BlockSpec(shape, lambda *_: (0,) * nd, pipeline_mode=pl.Buffered(1))


def _bdot(a, b):
    return jnp.dot(a.astype(BF16), b.astype(BF16), preferred_element_type=F32)


def _split2(x):
    hi = x.astype(BF16)
    lo = (x - hi.astype(F32)).astype(BF16)
    return hi, lo


def _split3(x):
    a = x.astype(BF16)
    r = x - a.astype(F32)
    b = r.astype(BF16)
    c = (r - b.astype(F32)).astype(BF16)
    return a, b, c


def _dot_exact_rhs(x, m_bf16):
    a, b, c = _split3(x)
    d = functools.partial(jnp.dot, preferred_element_type=F32)
    return d(a, m_bf16) + d(b, m_bf16) + d(c, m_bf16)


def _dot_exact_lhs(m_bf16, x):
    a, b, c = _split3(x)
    d = functools.partial(jnp.dot, preferred_element_type=F32)
    return d(m_bf16, a) + d(m_bf16, b) + d(m_bf16, c)


def _dot3(a_hi, a_lo, b_hi, b_lo):
    d = functools.partial(jnp.dot, preferred_element_type=F32)
    return d(a_hi, b_hi) + d(a_hi, b_lo) + d(a_lo, b_hi)


def _sigmoid(x):
    return 1.0 / (1.0 + jnp.exp(-x))


def _silu(x):
    return x * _sigmoid(x)


def _log_sigmoid(x):
    return jnp.minimum(x, 0.0) - jnp.log(1.0 + jnp.exp(-jnp.abs(x)))


def _layernorm(x):
    mu = jnp.mean(x, axis=-1, keepdims=True)
    xc = x - mu
    var = jnp.mean(xc * xc, axis=-1, keepdims=True)
    return xc * lax.rsqrt(var + LN_EPS)


def _modulate(x, shift, scale):
    return _layernorm(x) * (1.0 + scale) + shift


def _group_ones(width):
    r = lax.broadcasted_iota(jnp.int32, (width, width), 0) >> 6
    c = lax.broadcasted_iota(jnp.int32, (width, width), 1) >> 6
    return jnp.where(r == c, 1.0, 0.0).astype(BF16)


def _group_mean(x, ones):
    return _dot_exact_rhs(x, ones) * (1.0 / HEAD_DIM)


def _tri(n, upper):
    r = lax.broadcasted_iota(jnp.int32, (n, n), 0)
    c = lax.broadcasted_iota(jnp.int32, (n, n), 1)
    keep = (c >= r) if upper else (c <= r)
    return jnp.where(keep, 1.0, 0.0).astype(BF16)


def _ada_kernel(c_ref, w_ref, b_ref, o_ref):
    o_ref[...] = _bdot(_silu(c_ref[...]), w_ref[...]) + b_ref[...]


def _ada_call(cc, w, b):
    rows, d = cc.shape
    n = w.shape[1]
    tn = 1024
    return pl.pallas_call(
        _ada_kernel,
        grid=(n // tn,),
        in_specs=[pl.BlockSpec((rows, d), lambda j: (0, 0)),
                  pl.BlockSpec((d, tn), lambda j: (0, j)),
                  pl.BlockSpec((1, tn), lambda j: (0, j))],
        out_specs=pl.BlockSpec((rows, tn), lambda j: (0, j)),
        out_shape=jax.ShapeDtypeStruct((rows, n), F32),
        compiler_params=_cparams("parallel"),
        name="ada",
    )(cc, w, b)


def _inproj_kernel(x_ref, mod_ref, w_ml, w_ft, w_gq, w_gl, w_sm, o_ml, o_ft, o_gq, o_gl, o_sm):
    h = _modulate(x_ref[0], mod_ref[0, 0:1, :], mod_ref[0, 1:2, :]).astype(BF16)
    for w, o in ((w_ml, o_ml), (w_ft, o_ft), (w_gq, o_gq), (w_gl, o_gl), (w_sm, o_sm)):
        o[0] = jnp.dot(h, w[...], preferred_element_type=F32)


def _inproj_call(x, mod, ws, tm):
    b_, t, d = x.shape
    widths = [w.shape[1] for w in ws]
    return pl.pallas_call(
        _inproj_kernel,
        grid=(b_, t // tm),
        in_specs=[pl.BlockSpec((1, tm, d), lambda b, i: (b, i, 0)),
                  pl.BlockSpec((1, N_ADA, d), lambda b, i: (b, 0, 0))]
                 + [_resident(w.shape) for w in ws],
        out_specs=[pl.BlockSpec((1, tm, n), lambda b, i: (b, i, 0)) for n in widths],
        out_shape=[jax.ShapeDtypeStruct((b_, t, n), F32) for n in widths],
        compiler_params=_cparams("parallel", "parallel"),
        name="inproj",
    )(x, mod, *ws)


def _mlstm_kernel(qkv_f, sm_f, qkv_b, sm_b, bias_ref, s0_ref, m0_ref,
                  hf_ref, hb_ref, st_ref, mt_ref, s_scr, m_scr, *, chunk):
    i = pl.program_id(1)
    n_l = chunk

    @pl.when(i == 0)
    def _():
        s_scr[...] = s0_ref[0]
        m_scr[...] = m0_ref[0]

    row = lax.broadcasted_iota(jnp.int32, (n_l, n_l), 0)
    col = lax.broadcasted_iota(jnp.int32, (n_l, n_l), 1)
    lane = lax.broadcasted_iota(jnp.int32, (n_l, 128), 1)
    sub = lax.broadcasted_iota(jnp.int32, (HEAD_DIM, n_l), 0)
    ones_row = jnp.where(sub == 0, 1.0, 0.0)
    for d, (qkv_ref, sm_ref, h_ref) in enumerate(((qkv_f, sm_f, hf_ref), (qkv_b, sm_b, hb_ref))):
        rev = d == 1
        blk = qkv_ref[0]
        pre = sm_ref[0] + bias_ref[...]
        bcum = _dot_exact_lhs(_tri(n_l, upper=rev), _log_sigmoid(pre))
        pre_t = pre.T
        b_t = bcum.T
        q_t = blk[:, 0:ML_W].T.astype(BF16)
        v_t = blk[:, 2 * ML_W:3 * ML_W].T
        mask = (row >= col) if rev else (row <= col)
        last = 0 if rev else n_l - 1
        outs = []
        for h in range(ML_HEADS):
            ci = 8 * d + h
            cf = 8 * d + 4 + h
            idx = 4 * d + h
            pair = h // 2
            own = (lane >= HEAD_DIM) if h % 2 else (lane < HEAD_DIM)
            k_own = jnp.where(own, blk[:, ML_W + pair * 128:ML_W + (pair + 1) * 128], 0.0).astype(BF16)
            q_pair = q_t[pair * 128:(pair + 1) * 128, :]
            v_h = v_t[h * HEAD_DIM:(h + 1) * HEAD_DIM, :]
            c_col = pre[:, ci:ci + 1] - bcum[:, cf:cf + 1]
            b_row = b_t[cf:cf + 1, :]
            i_row = pre_t[ci:ci + 1, :]
            state = s_scr[idx]
            m_prev = m_scr[idx][:, 0:1]

            dmat = jnp.where(mask, b_row + c_col, -jnp.inf)
            inter = b_row + m_prev
            m_t = jnp.maximum(inter, jnp.max(dmat, axis=0, keepdims=True))
            w_intra = jnp.exp(dmat - m_t) * jnp.dot(k_own, q_pair, preferred_element_type=F32)
            w_inter = jnp.exp(inter - m_t)
            sq = jnp.dot(state.astype(BF16), q_pair, preferred_element_type=F32)
            num = (jnp.dot(v_h.astype(BF16), w_intra.astype(BF16), preferred_element_type=F32)
                   + w_inter * sq[0:HEAD_DIM])
            den = jnp.sum(w_intra, axis=0, keepdims=True) + w_inter * sq[HEAD_DIM:HEAD_DIM + 1]
            outs.append(num / jnp.maximum(jnp.abs(den), jnp.exp(-m_t)))

            b_last = b_row[:, last:last + 1]
            g_row = b_last - b_row + i_row
            m_new = jnp.maximum(b_last + m_prev, jnp.max(g_row, axis=1, keepdims=True))
            ws = jnp.exp(g_row - m_new)
            wc = jnp.exp(b_last + m_prev - m_new)
            v_ext = jnp.concatenate([v_h, ones_row], axis=0)
            s_scr[idx] = wc * state + jnp.dot((v_ext * ws).astype(BF16), k_own, preferred_element_type=F32)
            m_scr[idx] = jnp.broadcast_to(m_new, (1, SMALL_W))
        h_ref[0] = jnp.concatenate(outs, axis=0).T

    @pl.when(i == pl.num_programs(1) - 1)
    def _():
        st_ref[0] = s_scr[...]
        mt_ref[0] = m_scr[...]


def _mlstm_call(ml, small, bias, s0, m0):
    b_, t, _ = ml.shape
    chunk = min(ML_CHUNK, t)
    n = t // chunk
    fwd = lambda b, i: (b, i, 0)
    bwd = lambda b, i: (b, n - 1 - i, 0)
    state_spec = pl.BlockSpec((1, 2 * ML_HEADS, 2 * HEAD_DIM, 2 * HEAD_DIM), lambda b, i: (b, 0, 0, 0))
    m_spec = pl.BlockSpec((1, 2 * ML_HEADS, 1, SMALL_W), lambda b, i: (b, 0, 0, 0))
    return pl.pallas_call(
        functools.partial(_mlstm_kernel, chunk=chunk),
        grid=(b_, n),
        in_specs=[pl.BlockSpec((1, chunk, 3 * ML_W), fwd), pl.BlockSpec((1, chunk, SMALL_W), fwd),
                  pl.BlockSpec((1, chunk, 3 * ML_W), bwd), pl.BlockSpec((1, chunk, SMALL_W), bwd),
                  pl.BlockSpec((1, SMALL_W), lambda b, i: (0, 0)), state_spec, m_spec],
        out_specs=[pl.BlockSpec((1, chunk, ML_W), fwd), pl.BlockSpec((1, chunk, ML_W), bwd),
                   state_spec, m_spec],
        out_shape=[jax.ShapeDtypeStruct((b_, t, ML_W), F32), jax.ShapeDtypeStruct((b_, t, ML_W), F32),
                   jax.ShapeDtypeStruct(s0.shape, F32), jax.ShapeDtypeStruct(m0.shape, F32)],
        scratch_shapes=[pltpu.VMEM((2 * ML_HEADS, 2 * HEAD_DIM, 2 * HEAD_DIM), F32),
                        pltpu.VMEM((2 * ML_HEADS, 1, SMALL_W), F32)],
        compiler_params=_cparams("parallel", "arbitrary"),
        name="mlstm",
    )(ml, small, ml, small, bias, s0, m0)


def _gla_kernel(qkv_f, sm_f, qkv_b, sm_b, w2_ref, b2_ref, s0_ref,
                of_ref, ob_ref, st_ref, s_scr, *, chunk):
    i = pl.program_id(1)
    n_l = chunk
    n_sub = n_l // GL_SUB
    width = GL_W

    @pl.when(i == 0)
    def _():
        s_scr[...] = s0_ref[0]

    lane = lax.broadcasted_iota(jnp.int32, (GL_SUB, width), 1) >> 6
    head_masks = [lane == h for h in range(GL_HEADS)]
    chunk_lane = lax.broadcasted_iota(jnp.int32, (n_l, width), 1) >> 6
    chunk_masks = [chunk_lane == h for h in range(GL_HEADS)]
    bd_mask = ((lax.broadcasted_iota(jnp.int32, (width, width), 0) >> 6)
               == (lax.broadcasted_iota(jnp.int32, (width, width), 1) >> 6))
    states = [s_scr[0], s_scr[1]]
    new_states, new_outs = [], []
    for d, (qkv_ref, sm_ref, o_ref) in enumerate(((qkv_f, sm_f, of_ref), (qkv_b, sm_b, ob_ref))):
        rev = d == 1
        blk = qkv_ref[0]
        q = blk[:, 0:width]
        k = blk[:, width:2 * width]
        v = blk[:, 2 * width:3 * width]
        a = _bdot(sm_ref[0], w2_ref[...])[:, d * width:(d + 1) * width] + b2_ref[d:d + 1, :]
        la = _log_sigmoid(a) * (1.0 / GL_TAU)
        g = _dot_exact_lhs(_tri(n_l, upper=rev), la)
        last = 0 if rev else n_l - 1
        g_end = g[last:last + 1, :]
        state = states[d]
        o_inter = lax.dot_general((q * jnp.exp(g)).astype(BF16), state.astype(BF16),
                                  (((1,), (1,)), ((), ())), preferred_element_type=F32)
        v_bf = v.astype(BF16)
        a_blocks = []
        t_idx = lax.broadcasted_iota(jnp.int32, (GL_HEADS * GL_SUB, n_l), 0) & (GL_SUB - 1)
        s_idx = lax.broadcasted_iota(jnp.int32, (GL_HEADS * GL_SUB, n_l), 1)
        for s in range(n_sub):
            lo = s * GL_SUB
            hi = lo + GL_SUB
            r = g[hi - 1:hi, :] if rev else g[lo:lo + 1, :]
            qt = q[lo:hi] * jnp.exp(g[lo:hi] - r)
            kt = k * jnp.exp(jnp.minimum(r - g, GL_EXP_CLAMP))
            qstack = jnp.concatenate([jnp.where(hm, qt, 0.0) for hm in head_masks], axis=0)
            amat = lax.dot_general(qstack.astype(BF16), kt.astype(BF16),
                                   (((1,), (1,)), ((), ())), preferred_element_type=F32)
            keep = (s_idx >= lo + t_idx) if rev else (s_idx <= lo + t_idx)
            a_blocks.append(jnp.where(keep, amat, 0.0).astype(BF16))
        a_all = jnp.concatenate([a_blocks[s][h * GL_SUB:(h + 1) * GL_SUB]
                                 for h in range(GL_HEADS) for s in range(n_sub)], axis=0)
        ov = jnp.dot(a_all, v_bf, preferred_element_type=F32)
        o_intra = jnp.where(chunk_masks[0], ov[0:n_l], 0.0)
        for h in range(1, GL_HEADS):
            o_intra = o_intra + jnp.where(chunk_masks[h], ov[h * n_l:(h + 1) * n_l], 0.0)
        new_outs.append(o_inter + o_intra)

        kg = k * jnp.exp(g_end - g)
        upd = jnp.dot(v.T.astype(BF16), kg.astype(BF16), preferred_element_type=F32)
        new_states.append(jnp.exp(g_end) * state + jnp.where(bd_mask, upd, 0.0))
    of_ref[0] = new_outs[0]
    ob_ref[0] = new_outs[1]
    s_scr[0] = new_states[0]
    s_scr[1] = new_states[1]

    @pl.when(i == pl.num_programs(1) - 1)
    def _():
        st_ref[0] = s_scr[...]


def _gla_call(gl, small, w2e, b2, s0):
    b_, t, _ = gl.shape
    chunk = min(GL_CHUNK, t)
    n = t // chunk
    fwd = lambda b, i: (b, i, 0)
    bwd = lambda b, i: (b, n - 1 - i, 0)
    state_spec = pl.BlockSpec((1, 2, GL_W, GL_W), lambda b, i: (b, 0, 0, 0))
    return pl.pallas_call(
        functools.partial(_gla_kernel, chunk=chunk),
        grid=(b_, n),
        in_specs=[pl.BlockSpec((1, chunk, 3 * GL_W), fwd), pl.BlockSpec((1, chunk, SMALL_W), fwd),
                  pl.BlockSpec((1, chunk, 3 * GL_W), bwd), pl.BlockSpec((1, chunk, SMALL_W), bwd),
                  _resident(w2e.shape), _resident(b2.shape), state_spec],
        out_specs=[pl.BlockSpec((1, chunk, GL_W), fwd), pl.BlockSpec((1, chunk, GL_W), bwd), state_spec],
        out_shape=[jax.ShapeDtypeStruct((b_, t, GL_W), F32), jax.ShapeDtypeStruct((b_, t, GL_W), F32),
                   jax.ShapeDtypeStruct(s0.shape, F32)],
        scratch_shapes=[pltpu.VMEM((2, GL_W, GL_W), F32)],
        compiler_params=_cparams("parallel", "arbitrary"),
        name="gla",
    )(gl, small, gl, small, w2e, b2, s0)


def _qkprep_kernel(gq_ref, cos_ref, sin_ref, gq_g, gk_g, sb_ref, qt_out, k_out, vt_out):
    x = gq_ref[0]
    tm = x.shape[0]

    def norm_rope(z, g, reps):
        width = z.shape[1]
        msq = _group_mean(z * z, _group_ones(width))
        zn = z * lax.rsqrt(msq + LN_EPS) * jnp.tile(g, (1, reps))
        lane = lax.broadcasted_iota(jnp.int32, zn.shape, 1)
        partner = jnp.where((lane & 31) < ROPE_PAIRS,
                            pltpu.roll(zn, width - ROPE_PAIRS, axis=1),
                            pltpu.roll(zn, ROPE_PAIRS, axis=1))
        return zn * jnp.tile(cos_ref[...], (1, reps)) + partner * jnp.tile(sin_ref[...], (1, reps))

    lane = lax.broadcasted_iota(jnp.int32, (tm, 128), 1)

    def pad_heads(z, extra):
        out = []
        for p in range(z.shape[1] // 128):
            pair = z[:, p * 128:(p + 1) * 128]
            for base in (pair, pltpu.roll(pair, HEAD_DIM, axis=1)):
                out.append(jnp.where(lane < HEAD_DIM, base, jnp.where(lane == HEAD_DIM, extra, 0.0)))
        return out

    q = norm_rope(x[:, 0:GQ_W], gq_g[...], GQ_W // 128) * (LOG2_E * HEAD_DIM ** -0.5)
    qt_out[0] = jnp.concatenate(pad_heads(q, 1.0), axis=1).T.astype(BF16)
    k = norm_rope(x[:, GQ_W:GQ_W + GQ_KW], gk_g[...], GQ_KW // 128)
    for j, kj in enumerate(pad_heads(k, sb_ref[0:1, 0:1])):
        k_out[0, j] = kj.astype(BF16)
    v = x[:, GQ_W + GQ_KW:GQ_W + 2 * GQ_KW]
    v_t = v.T
    sub = lax.broadcasted_iota(jnp.int32, (V_ROWS - HEAD_DIM, tm), 0)
    ones_rows = jnp.where(sub == 0, 1.0, 0.0)
    vt_out[0] = jnp.concatenate([v_t[0:HEAD_DIM], ones_rows, v_t[HEAD_DIM:2 * HEAD_DIM], ones_rows],
                                axis=0).astype(BF16)


def _qkprep_call(gq, cos, sin, gq_g, gk_g, score_bias, tm):
    b_, t, w = gq.shape
    n_q = GQ_KV * GQ_G
    return pl.pallas_call(
        _qkprep_kernel,
        grid=(b_, t // tm),
        in_specs=[pl.BlockSpec((1, tm, w), lambda b, i: (b, i, 0)),
                  pl.BlockSpec((tm, 128), lambda b, i: (i, 0)),
                  pl.BlockSpec((tm, 128), lambda b, i: (i, 0)),
                  _resident(gq_g.shape), _resident(gk_g.shape), _resident(score_bias.shape)],
        out_specs=[pl.BlockSpec((1, n_q * 128, tm), lambda b, i: (b, 0, i)),
                   pl.BlockSpec((1, GQ_KV, tm, 128), lambda b, i: (b, 0, i, 0)),
                   pl.BlockSpec((1, GQ_KV * V_ROWS, tm), lambda b, i: (b, 0, i))],
        out_shape=[jax.ShapeDtypeStruct((b_, n_q * 128, t), BF16),
                   jax.ShapeDtypeStruct((b_, GQ_KV, t, 128), BF16),
                   jax.ShapeDtypeStruct((b_, GQ_KV * V_ROWS, t), BF16)],
        compiler_params=_cparams("parallel", "parallel"),
        name="qkprep",
    )(gq, cos, sin, gq_g, gk_g, score_bias)


def _attn_kernel(qt_ref, k_ref, vt_ref, o_ref, m_scr, acc_scr, *, n_tiles, tk, bounded):
    acc_scr[...] = jnp.zeros(acc_scr.shape, F32)
    if not bounded:
        m_scr[...] = jnp.full(m_scr.shape, NEG_BIG, F32)

    def body(j, carry):
        start = pl.multiple_of(j * tk, tk)
        k_tile = k_ref[0, 0, pl.ds(start, tk), :]
        vt_tile = vt_ref[0, :, pl.ds(start, tk)]
        for g in range(GQ_G):
            st = jnp.dot(k_tile, qt_ref[0, g * 128:(g + 1) * 128, :], preferred_element_type=F32)
            if bounded:
                acc_scr[g] += jnp.dot(vt_tile, jnp.exp2(st).astype(BF16), preferred_element_type=F32)
            else:
                m_prev = m_scr[g]
                m_new = jnp.maximum(m_prev, jnp.max(st, axis=0, keepdims=True))
                p = jnp.exp2(st - m_new).astype(BF16)
                acc_scr[g] = (jnp.exp2(m_prev - m_new) * acc_scr[g]
                              + jnp.dot(vt_tile, p, preferred_element_type=F32))
                m_scr[g] = m_new
        return carry

    lax.fori_loop(0, n_tiles, body, 0)
    outs = []
    for g in range(GQ_G):
        acc = acc_scr[g]
        outs.append(acc[0:HEAD_DIM] / acc[HEAD_DIM:HEAD_DIM + 1])
    o_ref[0] = jnp.concatenate(outs, axis=0).T


def _attn_call(qt, k, vt, logit_bound):
    b_, _, t = qt.shape
    n_keys = k.shape[2]
    tq = min(ATT_TQ, t)
    tk = ATT_TK if n_keys % ATT_TK == 0 else n_keys
    gw = GQ_G * HEAD_DIM

    def call(bounded):
        return pl.pallas_call(
            functools.partial(_attn_kernel, n_tiles=n_keys // tk, tk=tk, bounded=bounded),
            grid=(b_, GQ_KV, t // tq),
            in_specs=[pl.BlockSpec((1, GQ_G * 128, tq), lambda b, kv, i: (b, kv, i)),
                      pl.BlockSpec((1, 1, n_keys, 128), lambda b, kv, i: (b, kv, 0, 0)),
                      pl.BlockSpec((1, V_ROWS, n_keys), lambda b, kv, i: (b, kv, 0))],
            out_specs=pl.BlockSpec((1, tq, gw), lambda b, kv, i: (b, i, kv)),
            out_shape=jax.ShapeDtypeStruct((b_, t, GQ_W), F32),
            scratch_shapes=[pltpu.VMEM((GQ_G, 1, tq), F32),
                            pltpu.VMEM((GQ_G, V_ROWS, tq), F32)],
            compiler_params=_cparams("parallel", "parallel", "arbitrary"),
            name="attention_bounded" if bounded else "attention_online",
        )(qt, k, vt)

    return lax.cond(logit_bound <= MAX_LOGIT_BOUND, lambda: call(True), lambda: call(False))


def _fourier_factors(t):
    bits = int(round(math.log2(t)))
    assert 1 << bits == t
    n1 = 1 << (bits // 2)
    return n1, t // n1


def _hi_lo(a):
    a = np.asarray(a, np.float64)
    hi = jnp.asarray(a, F32).astype(BF16)
    lo = (jnp.asarray(a, F32) - hi.astype(F32)).astype(BF16)
    return hi, lo


def _fourier_tables(t):
    n1, n2 = _fourier_factors(t)
    c = np.arange(FT_GC)
    ang = 2.0 * np.pi * np.outer(c, c) / FT_GC
    eye = np.eye(FT_GROUPS)
    w0 = np.concatenate([np.kron(eye, np.cos(ang)), -np.kron(eye, np.sin(ang))], axis=1)
    a1 = 2.0 * np.pi * np.outer(np.arange(n1), np.arange(n1)) / n1
    fr, fi = np.cos(a1), -np.sin(a1)
    m1 = np.block([[fr, -fi], [fi, fr]])
    tw = 2.0 * np.pi * np.outer(np.arange(n1), np.arange(n2)) / t
    a2 = 2.0 * np.pi * np.outer(np.arange(n2), np.arange(n2)) / n2
    m2 = np.concatenate([np.cos(a2), np.sin(a2)], axis=1) / math.sqrt(t * FT_GC)
    return dict(n1=n1, n2=n2, w0=_hi_lo(w0), m1=_hi_lo(m1), m2=_hi_lo(m2),
                twr=np.cos(tw).astype(np.float32), twi=(-np.sin(tw)).astype(np.float32))


def _ft_stage1_kernel(u_ref, w0h, w0l, m1h, m1l, twr_ref, twi_ref, yr_ref, yi_ref, *, n_tok):
    n1 = u_ref.shape[1]
    zr, zi = [], []
    for j in range(n_tok):
        uh, ul = _split2(u_ref[0, :, j * FT_W:(j + 1) * FT_W])
        z = _dot3(uh, ul, w0h[...], w0l[...])
        zr.append(z[:, :FT_W])
        zi.append(z[:, FT_W:])
    z = jnp.concatenate([jnp.concatenate(zr, axis=1), jnp.concatenate(zi, axis=1)], axis=0)
    zh, zl = _split2(z)
    y = _dot3(m1h[...], m1l[...], zh, zl)
    y_r, y_i = y[:n1], y[n1:]
    twr, twi = twr_ref[0], twi_ref[0]
    for j in range(n_tok):
        sl = slice(j * FT_W, (j + 1) * FT_W)
        cr, ci = twr[:, j:j + 1], twi[:, j:j + 1]
        yr_ref[0, :, j, :] = y_r[:, sl] * cr - y_i[:, sl] * ci
        yi_ref[0, :, j, :] = y_r[:, sl] * ci + y_i[:, sl] * cr


def _ft_stage2_kernel(yr_ref, yi_ref, m2h, m2l, o_ref, *, n_k1, n2):
    for j in range(n_k1):
        y = jnp.concatenate([yr_ref[0, j * n2:(j + 1) * n2, :], yi_ref[0, j * n2:(j + 1) * n2, :]], axis=0)
        yh, yl = _split2(y)
        o_ref[0, :, j, :] = _dot3(m2h[...], m2l[...], yh, yl)


def _fourier_call(u, tabs):
    b_, t, w = u.shape
    n1, n2 = tabs["n1"], tabs["n2"]
    n_tok = min(FT_TOKENS_PER_STEP, n2)
    twr = jnp.asarray(tabs["twr"]).reshape(n1, n2 // n_tok, n_tok).transpose(1, 0, 2)
    twi = jnp.asarray(tabs["twi"]).reshape(n1, n2 // n_tok, n_tok).transpose(1, 0, 2)
    grid_view = jax.ShapeDtypeStruct((b_, n1, n2, w), F32)
    tok_block = pl.BlockSpec((1, n1, n_tok, w), lambda b, j: (b, 0, j, 0))
    yr, yi = pl.pallas_call(
        functools.partial(_ft_stage1_kernel, n_tok=n_tok),
        grid=(b_, n2 // n_tok),
        in_specs=[pl.BlockSpec((1, n1, n_tok * w), lambda b, j: (b, 0, j)),
                  _resident(tabs["w0"][0].shape), _resident(tabs["w0"][1].shape),
                  _resident(tabs["m1"][0].shape), _resident(tabs["m1"][1].shape),
                  pl.BlockSpec((1, n1, n_tok), lambda b, j: (j, 0, 0)),
                  pl.BlockSpec((1, n1, n_tok), lambda b, j: (j, 0, 0))],
        out_specs=[tok_block] * 2,
        out_shape=[grid_view, grid_view],
        compiler_params=_cparams("parallel", "parallel"),
        name="fourier_stage1",
    )(u.reshape(b_, n1, n2 * w), *tabs["w0"], *tabs["m1"], twr, twi)
    n_k1 = min(FT_TOKENS_PER_STEP, n1)
    out = pl.pallas_call(
        functools.partial(_ft_stage2_kernel, n_k1=n_k1, n2=n2),
        grid=(b_, n1 // n_k1),
        in_specs=[pl.BlockSpec((1, n_k1 * n2, w), lambda b, j: (b, j, 0)),
                  pl.BlockSpec((1, n_k1 * n2, w), lambda b, j: (b, j, 0)),
                  _resident(tabs["m2"][0].shape), _resident(tabs["m2"][1].shape)],
        out_specs=pl.BlockSpec((1, n2, n_k1, w), lambda b, j: (b, 0, j, 0)),
        out_shape=jax.ShapeDtypeStruct((b_, n2, n1, w), F32),
        compiler_params=_cparams("parallel", "parallel"),
        name="fourier_stage2",
    )(yr.reshape(b_, t, w), yi.reshape(b_, t, w), *tabs["m2"])
    return out.reshape(b_, t, w)


def _merge_kernel(x_ref, mod_ref, hf_ref, hb_ref, mlo_ref, ft_ref, att_ref, of_ref, ob_ref, glr_ref,
                  wgate, bgate, wbr, wout, mlg, glg, lng, lnb, o_ref, *, alpha):
    x = x_ref[0]
    h = _modulate(x, mod_ref[0, 0:1, :], mod_ref[0, 1:2, :]).astype(BF16)
    ones = _group_ones(ML_W)
    hs = hf_ref[0] + hb_ref[0]
    hc = hs - _group_mean(hs, ones)
    hn = hc * lax.rsqrt(_group_mean(hc * hc, ones) + LN_EPS) * mlg[...]
    br_ml = hn * _sigmoid(mlo_ref[0])
    os_ = of_ref[0] + ob_ref[0]
    on = os_ * lax.rsqrt(_group_mean(os_ * os_, ones) + LN_EPS) * glg[...]
    br_gl = on * _silu(glr_ref[0])
    branches = (br_ml, ft_ref[0], att_ref[0], br_gl)
    mixed = None
    off = 0
    for j, br in enumerate(branches):
        wd = br.shape[1]
        gate = _sigmoid(jnp.dot(h, wgate[:, j * D_MODEL:(j + 1) * D_MODEL], preferred_element_type=F32)
                        + bgate[:, j * D_MODEL:(j + 1) * D_MODEL])
        u = gate * jnp.dot(br.astype(BF16), wbr[off:off + wd, :], preferred_element_type=F32)
        mixed = u if mixed is None else mixed + u
        off += wd
    y = jnp.dot(mixed.astype(BF16), wout[...], preferred_element_type=F32)
    o_ref[0] = _layernorm(alpha * x + mod_ref[0, 2:3, :] * y) * lng[...] + lnb[...]


def _merge_call(x, mod, hf, hb, ml, ftb, att, of, ob, gl, weights, alpha, tm):
    b_, t, d = x.shape
    tok = lambda w: pl.BlockSpec((1, tm, w), lambda b, i: (b, i, 0))
    last_quarter = pl.BlockSpec((1, tm, ML_W), lambda b, i: (b, i, 3))
    return pl.pallas_call(
        functools.partial(_merge_kernel, alpha=alpha),
        grid=(b_, t // tm),
        in_specs=[tok(d), pl.BlockSpec((1, N_ADA, d), lambda b, i: (b, 0, 0)),
                  tok(ML_W), tok(ML_W), last_quarter, tok(FT_W), tok(GQ_W), tok(GL_W), tok(GL_W),
                  last_quarter] + [_resident(w.shape) for w in weights],
        out_specs=tok(d),
        out_shape=jax.ShapeDtypeStruct((b_, t, d), F32),
        compiler_params=_cparams("parallel", "parallel"),
        name="merge",
    )(x, mod, hf, hb, ml, ftb, att, of, ob, gl, *weights)


def _ffn_kernel(x_ref, mod_ref, wg, wu, wd, lng, lnb, o_ref, *, alpha):
    x = x_ref[0]
    h = _modulate(x, mod_ref[0, 3:4, :], mod_ref[0, 4:5, :]).astype(BF16)
    a = jnp.dot(h, wg[...], preferred_element_type=F32)
    u = jnp.dot(h, wu[...], preferred_element_type=F32)
    f = jnp.dot((_silu(a) * u).astype(BF16), wd[...], preferred_element_type=F32)
    o_ref[0] = _layernorm(alpha * x + mod_ref[0, 5:6, :] * f) * lng[...] + lnb[...]


def _ffn_call(x, mod, weights, alpha, tm):
    b_, t, d = x.shape
    return pl.pallas_call(
        functools.partial(_ffn_kernel, alpha=alpha),
        grid=(b_, t // tm),
        in_specs=[pl.BlockSpec((1, tm, d), lambda b, i: (b, i, 0)),
                  pl.BlockSpec((1, N_ADA, d), lambda b, i: (b, 0, 0))]
                 + [_resident(w.shape) for w in weights],
        out_specs=pl.BlockSpec((1, tm, d), lambda b, i: (b, i, 0)),
        out_shape=jax.ShapeDtypeStruct((b_, t, d), F32),
        compiler_params=_cparams("parallel", "parallel"),
        name="ffn",
    )(x, mod, *weights)


def _router_kernel(x_ref, mod_ref, wrh, wrl, br_ref, h_out, dense_out, pos_out, post_out, cnt_out):
    tm = x_ref.shape[1]
    h = _modulate(x_ref[0], mod_ref[0, 3:4, :], mod_ref[0, 4:5, :])
    h_out[0] = h.astype(BF16)
    hh, hl = _split2(h)
    logits = _dot3(hh, hl, wrh[...], wrl[...])
    lane = lax.broadcasted_iota(jnp.int32, logits.shape, 1).astype(F32)
    sel = jnp.where(lane < N_EXPERTS, logits + br_ref[...], -jnp.inf)

    def pick(scores):
        mx = jnp.max(scores, axis=1, keepdims=True)
        idx = jnp.min(jnp.where(scores == mx, lane, 2.0 * SMALL_W), axis=1, keepdims=True)
        return lane == idx

    first = pick(sel)
    second = pick(jnp.where(first, -jnp.inf, sel))
    l1 = jnp.sum(jnp.where(first, logits, 0.0), axis=1, keepdims=True)
    l2 = jnp.sum(jnp.where(second, logits, 0.0), axis=1, keepdims=True)
    mx = jnp.maximum(l1, l2)
    e1, e2 = jnp.exp(l1 - mx), jnp.exp(l2 - mx)
    inv = 1.0 / (e1 + e2)
    dense_out[0] = jnp.where(first, e1 * inv, 0.0) + jnp.where(second, e2 * inv, 0.0)
    routed = first | second
    counts = jnp.dot(_tri(tm, upper=False), jnp.where(routed, 1.0, 0.0).astype(BF16),
                     preferred_element_type=F32)
    pos = jnp.where(routed, counts - 1.0, -1.0)
    pos_out[0] = pos
    post_out[0] = pos.T[0:8, :]
    cnt_out[0, 0] = jnp.broadcast_to(counts[tm - 1:tm, :], (8, SMALL_W))


def _router_call(x, mod, wrh, wrl, br, tm):
    b_, t, d = x.shape
    nt = t // tm
    tok = lambda w: pl.BlockSpec((1, tm, w), lambda b, i: (b, i, 0))
    return pl.pallas_call(
        _router_kernel,
        grid=(b_, nt),
        in_specs=[tok(d), pl.BlockSpec((1, N_ADA, d), lambda b, i: (b, 0, 0)),
                  _resident(wrh.shape), _resident(wrl.shape), _resident(br.shape)],
        out_specs=[tok(d), tok(SMALL_W), tok(SMALL_W),
                   pl.BlockSpec((1, 8, tm), lambda b, i: (b, 0, i)),
                   pl.BlockSpec((1, 1, 8, SMALL_W), lambda b, i: (b, i, 0, 0))],
        out_shape=[jax.ShapeDtypeStruct((b_, t, d), BF16), jax.ShapeDtypeStruct((b_, t, SMALL_W), F32),
                   jax.ShapeDtypeStruct((b_, t, SMALL_W), F32), jax.ShapeDtypeStruct((b_, 8, t), F32),
                   jax.ShapeDtypeStruct((b_, nt, 8, SMALL_W), F32)],
        compiler_params=_cparams("parallel", "parallel"),
        name="router",
    )(x, mod, wrh, wrl, br)


def _moe_kernel(cnt_ref, x_ref, mod_ref, h_ref, dense_ref, pos_ref, post_ref, wg, wu, wd, lng, lnb,
                o_ref, acc_scr, *, alpha, rows):
    b, i, e = pl.program_id(0), pl.program_id(1), pl.program_id(2)
    tm = x_ref.shape[1]

    @pl.when(e == 0)
    def _():
        acc_scr[...] = jnp.zeros(acc_scr.shape, F32)

    lane = lax.broadcasted_iota(jnp.int32, (tm, SMALL_W), 1)
    w_col = jnp.sum(jnp.where(lane == e, dense_ref[0], 0.0), axis=1, keepdims=True)
    pos_col = jnp.sum(jnp.where(lane == e, pos_ref[0], 0.0), axis=1, keepdims=True)
    pos_row = post_ref[0, pl.ds(e, 1), :]
    count = cnt_ref[(b * pl.num_programs(1) + i) * N_EXPERTS + e]

    def run_block(first_row, n):
        base = first_row.astype(F32)
        r_iota = lax.broadcasted_iota(jnp.int32, (n, 1), 0).astype(F32) + base
        c_iota = lax.broadcasted_iota(jnp.int32, (1, n), 1).astype(F32) + base
        gather = jnp.where(pos_row == r_iota, 1.0, 0.0).astype(BF16)
        xs = jnp.dot(gather, h_ref[0], preferred_element_type=F32).astype(BF16)
        a = jnp.dot(xs, wg[0], preferred_element_type=F32)
        u = jnp.dot(xs, wu[0], preferred_element_type=F32)
        f = jnp.dot((_silu(a) * u).astype(BF16), wd[0], preferred_element_type=F32)
        scatter = jnp.where(pos_col == c_iota, 1.0, 0.0).astype(BF16)
        acc_scr[...] += w_col * jnp.dot(scatter, f.astype(BF16), preferred_element_type=F32)

    n_full = count // rows
    rem = count - n_full * rows
    half = rows // 2

    def full_block(j, carry):
        run_block(j * rows, rows)
        return carry

    lax.fori_loop(0, n_full, full_block, 0)

    @pl.when(rem > half)
    def _():
        run_block(n_full * rows, rows)

    @pl.when((rem > 0) & (rem <= half))
    def _():
        run_block(n_full * rows, half)

    @pl.when(e == pl.num_programs(2) - 1)
    def _():
        o_ref[0] = (_layernorm(alpha * x_ref[0] + mod_ref[0, 5:6, :] * acc_scr[...]) * lng[...]
                    + lnb[...])


def _moe_call(x, mod, wrh, wrl, br, wg, wu, wd, lng, lnb, alpha, tm):
    b_, t, d = x.shape
    n_e, _, ff = wg.shape
    nt = t // tm
    h, dense, pos, post, cnt = _router_call(x, mod, wrh, wrl, br, tm)
    counts = cnt[:, :, 0, 0:n_e].astype(jnp.int32).reshape(-1)
    tok = lambda w: pl.BlockSpec((1, tm, w), lambda b, i, e, c: (b, i, 0))
    grid_spec = pltpu.PrefetchScalarGridSpec(
        num_scalar_prefetch=1,
        grid=(b_, nt, n_e),
        in_specs=[tok(d), pl.BlockSpec((1, N_ADA, d), lambda b, i, e, c: (b, 0, 0)),
                  tok(d), tok(SMALL_W), tok(SMALL_W),
                  pl.BlockSpec((1, 8, tm), lambda b, i, e, c: (b, 0, i)),
                  pl.BlockSpec((1, d, ff), lambda b, i, e, c: (e, 0, 0)),
                  pl.BlockSpec((1, d, ff), lambda b, i, e, c: (e, 0, 0)),
                  pl.BlockSpec((1, ff, d), lambda b, i, e, c: (e, 0, 0)),
                  _resident(lng.shape), _resident(lnb.shape)],
        out_specs=tok(d),
        scratch_shapes=[pltpu.VMEM((tm, d), F32)])
    return pl.pallas_call(
        functools.partial(_moe_kernel, alpha=alpha, rows=min(MOE_ROWS, tm)),
        grid_spec=grid_spec,
        out_shape=jax.ShapeDtypeStruct((b_, t, d), F32),
        compiler_params=_cparams("parallel", "parallel", "arbitrary"),
        name="moe",
    )(counts, x, mod, h, dense, pos, post, wg, wu, wd, lng, lnb)


def _rope_tables(t):
    rows = t // GRID_W
    row = jnp.repeat(jnp.arange(rows, dtype=F32), GRID_W)
    col = jnp.tile(jnp.arange(GRID_W, dtype=F32), rows)
    inv = jnp.power(ROPE_BASE, -jnp.arange(ROPE_PAIRS, dtype=F32) / ROPE_PAIRS)
    ar, ac = row[:, None] * inv, col[:, None] * inv
    cos = jnp.concatenate([jnp.cos(ar), jnp.cos(ar), jnp.cos(ac), jnp.cos(ac)], axis=1)
    sin = jnp.concatenate([-jnp.sin(ar), jnp.sin(ar), -jnp.sin(ac), jnp.sin(ac)], axis=1)
    return jnp.tile(cos, (1, 2)), jnp.tile(sin, (1, 2))


def _token_tile(t, pref):
    return pref if t % pref == 0 else t


def kernel(x, c, ctx, c_ctx, w_ada, b_ada, w_in, ml_gate_b, ml_norm_g, gq_qnorm_g, gq_knorm_g, gl_w2, gl_b2,
           gl_norm_g, w_branch, w_gate, b_gate, w_out, ln1_g, ln1_b, ln2_g, ln2_b, ffd_wg, ffd_wu, ffd_wd,
           moe_wr, moe_br, moe_wg, moe_wu, moe_wd):
    b_, s_len, d = x.shape
    n_ctx = ctx.shape[1]
    depth = w_in.shape[0]
    alpha = (2.0 * depth) ** 0.25
    bf = lambda a: a.astype(BF16)
    row = lambda a: a.reshape(1, -1).astype(F32)

    cos_l, sin_l = _rope_tables(s_len)
    cos_c, sin_c = jnp.ones((n_ctx, 128), F32), jnp.zeros((n_ctx, 128), F32)
    ft_l, ft_c = _fourier_tables(s_len), _fourier_tables(n_ctx)
    tm_l, tm_c = _token_tile(s_len, 512), _token_tile(n_ctx, 512)

    cc = jnp.zeros((8, d), F32).at[0:b_].set(c).at[b_].set(c_ctx)
    xc = ctx
    for l in range(depth):
        last = l == depth - 1
        ada = _ada_call(cc, bf(w_ada[l]), row(b_ada[l])).reshape(8, N_ADA, d)
        mod_l = ada[0:b_]
        mod_c = jnp.broadcast_to(ada[b_][None], (b_, N_ADA, d))

        w = w_in[l]
        o_gate, o_ft, o_gq, o_gl, o_af = 4 * ML_W, 4 * ML_W + 16, 4 * ML_W + 16 + FT_W, \
            4 * ML_W + 16 + FT_W + GQ_W + 2 * GQ_KW, 4 * ML_W + 16 + FT_W + GQ_W + 2 * GQ_KW + 4 * GL_W
        ml_scale = jnp.concatenate([jnp.ones((ML_W,)), jnp.full((ML_W,), HEAD_DIM ** -0.5), jnp.ones((2 * ML_W,))])
        gl_scale = jnp.concatenate([jnp.full((GL_W,), HEAD_DIM ** -0.5), jnp.ones((3 * GL_W,))])
        w_small = jnp.concatenate([w[:, o_gate:o_gate + 16], w[:, o_af:o_af + 2 * GL_RANK],
                                   jnp.zeros((d, SMALL_W - 16 - 2 * GL_RANK), F32)], axis=1)
        ws = [bf(w[:, 0:4 * ML_W] * ml_scale), bf(w[:, o_ft:o_ft + FT_W]), bf(w[:, o_gq:o_gl]),
              bf(w[:, o_gl:o_af] * gl_scale), bf(w_small)]
        gate_bias = jnp.zeros((1, SMALL_W), F32).at[0, 0:16].set(ml_gate_b[l].reshape(-1))
        w2e = jnp.zeros((SMALL_W, 2 * GL_W), F32)
        w2e = w2e.at[16:16 + GL_RANK, 0:GL_W].set(gl_w2[l, 0]).at[16 + GL_RANK:16 + 2 * GL_RANK, GL_W:].set(gl_w2[l, 1])
        merge_w = [bf(w_gate[l]), row(b_gate[l]), bf(w_branch[l]), bf(w_out[l]), row(ml_norm_g[l]),
                   row(gl_norm_g[l]), row(ln1_g[l]), row(ln1_b[l])]
        gq_g = jnp.tile(row(gq_qnorm_g[l]), (1, 2))
        gk_g = jnp.tile(row(gq_knorm_g[l]), (1, 2))
        logit_bound = (1.02 * LOG2_E * HEAD_DIM ** 0.5) * jnp.max(jnp.abs(gq_qnorm_g[l])) * jnp.max(jnp.abs(gq_knorm_g[l]))
        score_bias = jnp.zeros((1, HEAD_DIM), F32).at[0, 0].set(-logit_bound)

        def mixers(xs, mod, tm, cos, sin, states):
            ml, ftu, gq, gl, small = _inproj_call(xs, mod, ws, tm)
            hf, hb, ml_s, ml_m = _mlstm_call(ml, small, gate_bias, states[0], states[1])
            of, ob, gl_s = _gla_call(gl, small, bf(w2e), gl_b2[l].astype(F32), states[2])
            qt, k, vt = _qkprep_call(gq, cos, sin, gq_g, gk_g, score_bias, tm)
            return dict(ml=ml, ftu=ftu, gl=gl, hf=hf, hb=hb, of=of, ob=ob, qt=qt, k=k, vt=vt), (ml_s, ml_m, gl_s)

        zero_states = (jnp.zeros((b_, 2 * ML_HEADS, 2 * HEAD_DIM, 2 * HEAD_DIM), F32),
                       jnp.zeros((b_, 2 * ML_HEADS, 1, SMALL_W), F32),
                       jnp.zeros((b_, 2, GL_W, GL_W), F32))
        pc, ctx_states = mixers(xc, mod_c, tm_c, cos_c, sin_c, zero_states)
        pl_, _ = mixers(x, mod_l, tm_l, cos_l, sin_l, ctx_states)

        att_l = _attn_call(pl_["qt"], jnp.concatenate([pc["k"], pl_["k"]], axis=2),
                           jnp.concatenate([pc["vt"], pl_["vt"]], axis=2), logit_bound)
        x_mid = _merge_call(x, mod_l, pl_["hf"], pl_["hb"], pl_["ml"], _fourier_call(pl_["ftu"], ft_l), att_l,
                            pl_["of"], pl_["ob"], pl_["gl"], merge_w, alpha, tm_l)
        if not last:
            att_c = _attn_call(pc["qt"], pc["k"], pc["vt"], logit_bound)
            xc = _merge_call(xc, mod_c, pc["hf"], pc["hb"], pc["ml"], _fourier_call(pc["ftu"], ft_c), att_c,
                             pc["of"], pc["ob"], pc["gl"], merge_w, alpha, tm_c)

        j = l // 2
        if l % 2 == 0:
            ffn_w = [bf(ffd_wg[j]), bf(ffd_wu[j]), bf(ffd_wd[j]), row(ln2_g[l]), row(ln2_b[l])]
            x = _ffn_call(x_mid, mod_l, ffn_w, alpha, tm_l)
            if not last:
                xc = _ffn_call(xc, mod_c, ffn_w, alpha, tm_c)
        else:
            wr = jnp.zeros((d, SMALL_W), F32).at[:, 0:N_EXPERTS].set(moe_wr[j])
            wrh = bf(wr)
            wrl = bf(wr - wrh.astype(F32))
            brp = jnp.zeros((1, SMALL_W), F32).at[0, 0:N_EXPERTS].set(moe_br[j])
            moe_args = (wrh, wrl, brp, bf(moe_wg[j]), bf(moe_wu[j]), bf(moe_wd[j]), row(ln2_g[l]), row(ln2_b[l]))
            x = _moe_call(x_mid, mod_l, *moe_args, alpha, _token_tile(s_len, 1024))
            if not last:
                xc = _moe_call(xc, mod_c, *moe_args, alpha, tm_c)
    return x
```

```python
import functools
import math

import jax
import jax.numpy as jnp
import numpy as np
from jax import lax
from jax.experimental import pallas as pl
from jax.experimental.pallas import tpu as pltpu

F32 = jnp.float32
BF16 = jnp.bfloat16

D_MODEL = 1024
GRID_W = 64
HEAD_DIM = 64
ML_HEADS = 4
ML_W = ML_HEADS * HEAD_DIM
FT_GROUPS = 4
FT_GC = 64
FT_W = FT_GROUPS * FT_GC
GQ_KV = 2
GQ_G = 4
GQ_W = GQ_KV * GQ_G * HEAD_DIM
GQ_KW = GQ_KV * HEAD_DIM
ROPE_PAIRS = HEAD_DIM // 4
ROPE_BASE = 10000.0
GL_HEADS = 4
GL_W = GL_HEADS * HEAD_DIM
GL_RANK = 16
GL_TAU = 16.0
N_EXPERTS = 8
N_ADA = 6
LN_EPS = 1e-6
SMALL_W = 128

ML_CHUNK = 256
GL_CHUNK = 128
GL_SUB = 16
GL_EXP_CLAMP = 80.0
ATT_TQ = 1024
ATT_TK = 3328
ATT_ROWS = 256
LOG2_E = 1.4426950408889634
MAX_LOGIT_BOUND = 60.0
NEG_BIG = -1e30
MOE_ROWS = 256
FT_TOKENS_PER_STEP = 8
V_ROWS = 128

VMEM_LIMIT = 56 * 1024 * 1024


def _cparams(*sem):
    return pltpu.CompilerParams(dimension_semantics=sem, vmem_limit_bytes=VMEM_LIMIT)


def _resident(shape):
    nd = len(shape)
    return pl.BlockSpec(shape, lambda *_: (0,) * nd, pipeline_mode=pl.Buffered(1))


def _bdot(a, b):
    return jnp.dot(a.astype(BF16), b.astype(BF16), preferred_element_type=F32)


def _split2(x):
    hi = x.astype(BF16)
    lo = (x - hi.astype(F32)).astype(BF16)
    return hi, lo


def _split3(x):
    a = x.astype(BF16)
    r = x - a.astype(F32)
    b = r.astype(BF16)
    c = (r - b.astype(F32)).astype(BF16)
    return a, b, c


def _dot_exact_rhs(x, m_bf16):
    a, b, c = _split3(x)
    d = functools.partial(jnp.dot, preferred_element_type=F32)
    return d(a, m_bf16) + d(b, m_bf16) + d(c, m_bf16)


def _dot_exact_lhs(m_bf16, x):
    a, b, c = _split3(x)
    d = functools.partial(jnp.dot, preferred_element_type=F32)
    return d(m_bf16, a) + d(m_bf16, b) + d(m_bf16, c)


def _dot3(a_hi, a_lo, b_hi, b_lo):
    d = functools.partial(jnp.dot, preferred_element_type=F32)
    return d(a_hi, b_hi) + d(a_hi, b_lo) + d(a_lo, b_hi)


def _sigmoid(x):
    return 1.0 / (1.0 + jnp.exp(-x))


def _silu(x):
    return x * _sigmoid(x)


def _log_sigmoid(x):
    return jnp.minimum(x, 0.0) - jnp.log(1.0 + jnp.exp(-jnp.abs(x)))


def _layernorm(x):
    mu = jnp.mean(x, axis=-1, keepdims=True)
    xc = x - mu
    var = jnp.mean(xc * xc, axis=-1, keepdims=True)
    return xc * lax.rsqrt(var + LN_EPS)


def _modulate(x, shift, scale):
    return _layernorm(x) * (1.0 + scale) + shift


def _group_ones(width):
    r = lax.broadcasted_iota(jnp.int32, (width, width), 0) >> 6
    c = lax.broadcasted_iota(jnp.int32, (width, width), 1) >> 6
    return jnp.where(r == c, 1.0, 0.0).astype(BF16)


def _group_mean(x, ones):
    return _dot_exact_rhs(x, ones) * (1.0 / HEAD_DIM)


def _tri(n, upper):
    r = lax.broadcasted_iota(jnp.int32, (n, n), 0)
    c = lax.broadcasted_iota(jnp.int32, (n, n), 1)
    keep = (c >= r) if upper else (c <= r)
    return jnp.where(keep, 1.0, 0.0).astype(BF16)


def _ada_kernel(c_ref, w_ref, b_ref, o_ref):
    o_ref[...] = _bdot(_silu(c_ref[...]), w_ref[...]) + b_ref[...]


def _ada_call(cc, w, b):
    rows, d = cc.shape
    n = w.shape[1]
    tn = 1024
    return pl.pallas_call(
        _ada_kernel,
        grid=(n // tn,),
        in_specs=[pl.BlockSpec((rows, d), lambda j: (0, 0)),
                  pl.BlockSpec((d, tn), lambda j: (0, j)),
                  pl.BlockSpec((1, tn), lambda j: (0, j))],
        out_specs=pl.BlockSpec((rows, tn), lambda j: (0, j)),
        out_shape=jax.ShapeDtypeStruct((rows, n), F32),
        compiler_params=_cparams("parallel"),
        name="ada",
    )(cc, w, b)


def _inproj_kernel(x_ref, mod_ref, w_ml, w_ft, w_gq, w_gl, w_sm, o_ml, o_ft, o_gq, o_gl, o_sm):
    h = _modulate(x_ref[0], mod_ref[0, 0:1, :], mod_ref[0, 1:2, :]).astype(BF16)
    for w, o in ((w_ml, o_ml), (w_ft, o_ft), (w_gq, o_gq), (w_gl, o_gl), (w_sm, o_sm)):
        o[0] = jnp.dot(h, w[...], preferred_element_type=F32)


def _inproj_call(x, mod, ws, tm):
    b_, t, d = x.shape
    widths = [w.shape[1] for w in ws]
    return pl.pallas_call(
        _inproj_kernel,
        grid=(b_, t // tm),
        in_specs=[pl.BlockSpec((1, tm, d), lambda b, i: (b, i, 0)),
                  pl.BlockSpec((1, N_ADA, d), lambda b, i: (b, 0, 0))]
                 + [_resident(w.shape) for w in ws],
        out_specs=[pl.BlockSpec((1, tm, n), lambda b, i: (b, i, 0)) for n in widths],
        out_shape=[jax.ShapeDtypeStruct((b_, t, n), F32) for n in widths],
        compiler_params=_cparams("parallel", "parallel"),
        name="inproj",
    )(x, mod, *ws)


def _mlstm_kernel(qkv_f, sm_f, qkv_b, sm_b, bias_ref, s0_ref, m0_ref,
                  hf_ref, hb_ref, st_ref, mt_ref, s_scr, m_scr, *, chunk):
    i = pl.program_id(1)
    n_l = chunk

    @pl.when(i == 0)
    def _():
        s_scr[...] = s0_ref[0]
        m_scr[...] = m0_ref[0]

    row = lax.broadcasted_iota(jnp.int32, (n_l, n_l), 0)
    col = lax.broadcasted_iota(jnp.int32, (n_l, n_l), 1)
    lane = lax.broadcasted_iota(jnp.int32, (n_l, 128), 1)
    sub = lax.broadcasted_iota(jnp.int32, (HEAD_DIM, n_l), 0)
    ones_row = jnp.where(sub == 0, 1.0, 0.0)
    for d, (qkv_ref, sm_ref, h_ref) in enumerate(((qkv_f, sm_f, hf_ref), (qkv_b, sm_b, hb_ref))):
        rev = d == 1
        blk = qkv_ref[0]
        pre = sm_ref[0] + bias_ref[...]
        bcum = _dot_exact_lhs(_tri(n_l, upper=rev), _log_sigmoid(pre))
        pre_t = pre.T
        b_t = bcum.T
        q_t = blk[:, 0:ML_W].T.astype(BF16)
        v_t = blk[:, 2 * ML_W:3 * ML_W].T
        mask = (row >= col) if rev else (row <= col)
        last = 0 if rev else n_l - 1
        outs = []
        for h in range(ML_HEADS):
            ci = 8 * d + h
            cf = 8 * d + 4 + h
            idx = 4 * d + h
            pair = h // 2
            own = (lane >= HEAD_DIM) if h % 2 else (lane < HEAD_DIM)
            k_own = jnp.where(own, blk[:, ML_W + pair * 128:ML_W + (pair + 1) * 128], 0.0).astype(BF16)
            q_pair = q_t[pair * 128:(pair + 1) * 128, :]
            v_h = v_t[h * HEAD_DIM:(h + 1) * HEAD_DIM, :]
            c_col = pre[:, ci:ci + 1] - bcum[:, cf:cf + 1]
            b_row = b_t[cf:cf + 1, :]
            i_row = pre_t[ci:ci + 1, :]
            state = s_scr[idx]
            m_prev = m_scr[idx][:, 0:1]

            dmat = jnp.where(mask, b_row + c_col, -jnp.inf)
            inter = b_row + m_prev
            m_t = jnp.maximum(inter, jnp.max(dmat, axis=0, keepdims=True))
            w_intra = jnp.exp(dmat - m_t) * jnp.dot(k_own, q_pair, preferred_element_type=F32)
            w_inter = jnp.exp(inter - m_t)
            sq = jnp.dot(state.astype(BF16), q_pair, preferred_element_type=F32)
            num = (jnp.dot(v_h.astype(BF16), w_intra.astype(BF16), preferred_element_type=F32)
                   + w_inter * sq[0:HEAD_DIM])
            den = jnp.sum(w_intra, axis=0, keepdims=True) + w_inter * sq[HEAD_DIM:HEAD_DIM + 1]
            outs.append(num / jnp.maximum(jnp.abs(den), jnp.exp(-m_t)))

            b_last = b_row[:, last:last + 1]
            g_row = b_last - b_row + i_row
            m_new = jnp.maximum(b_last + m_prev, jnp.max(g_row, axis=1, keepdims=True))
            ws = jnp.exp(g_row - m_new)
            wc = jnp.exp(b_last + m_prev - m_new)
            v_ext = jnp.concatenate([v_h, ones_row], axis=0)
            s_scr[idx] = wc * state + jnp.dot((v_ext * ws).astype(BF16), k_own, preferred_element_type=F32)
            m_scr[idx] = jnp.broadcast_to(m_new, (1, SMALL_W))
        h_ref[0] = jnp.concatenate(outs, axis=0).T

    @pl.when(i == pl.num_programs(1) - 1)
    def _():
        st_ref[0] = s_scr[...]
        mt_ref[0] = m_scr[...]


def _mlstm_call(ml, small, bias, s0, m0):
    b_, t, _ = ml.shape
    chunk = min(ML_CHUNK, t)
    n = t // chunk
    fwd = lambda b, i: (b, i, 0)
    bwd = lambda b, i: (b, n - 1 - i, 0)
    state_spec = pl.BlockSpec((1, 2 * ML_HEADS, 2 * HEAD_DIM, 2 * HEAD_DIM), lambda b, i: (b, 0, 0, 0))
    m_spec = pl.BlockSpec((1, 2 * ML_HEADS, 1, SMALL_W), lambda b, i: (b, 0, 0, 0))
    return pl.pallas_call(
        functools.partial(_mlstm_kernel, chunk=chunk),
        grid=(b_, n),
        in_specs=[pl.BlockSpec((1, chunk, 3 * ML_W), fwd), pl.BlockSpec((1, chunk, SMALL_W), fwd),
                  pl.BlockSpec((1, chunk, 3 * ML_W), bwd), pl.BlockSpec((1, chunk, SMALL_W), bwd),
                  pl.BlockSpec((1, SMALL_W), lambda b, i: (0, 0)), state_spec, m_spec],
        out_specs=[pl.BlockSpec((1, chunk, ML_W), fwd), pl.BlockSpec((1, chunk, ML_W), bwd),
                   state_spec, m_spec],
        out_shape=[jax.ShapeDtypeStruct((b_, t, ML_W), F32), jax.ShapeDtypeStruct((b_, t, ML_W), F32),
                   jax.ShapeDtypeStruct(s0.shape, F32), jax.ShapeDtypeStruct(m0.shape, F32)],
        scratch_shapes=[pltpu.VMEM((2 * ML_HEADS, 2 * HEAD_DIM, 2 * HEAD_DIM), F32),
                        pltpu.VMEM((2 * ML_HEADS, 1, SMALL_W), F32)],
        compiler_params=_cparams("parallel", "arbitrary"),
        name="mlstm",
    )(ml, small, ml, small, bias, s0, m0)


def _gla_kernel(qkv_f, sm_f, qkv_b, sm_b, w2_ref, b2_ref, s0_ref,
                of_ref, ob_ref, st_ref, s_scr, *, chunk):
    i = pl.program_id(1)
    n_l = chunk
    n_sub = n_l // GL_SUB
    width = GL_W

    @pl.when(i == 0)
    def _():
        s_scr[...] = s0_ref[0]

    lane = lax.broadcasted_iota(jnp.int32, (GL_SUB, width), 1) >> 6
    head_masks = [lane == h for h in range(GL_HEADS)]
    chunk_lane = lax.broadcasted_iota(jnp.int32, (n_l, width), 1) >> 6
    chunk_masks = [chunk_lane == h for h in range(GL_HEADS)]
    bd_mask = ((lax.broadcasted_iota(jnp.int32, (width, width), 0) >> 6)
               == (lax.broadcasted_iota(jnp.int32, (width, width), 1) >> 6))
    states = [s_scr[0], s_scr[1]]
    new_states, new_outs = [], []
    for d, (qkv_ref, sm_ref, o_ref) in enumerate(((qkv_f, sm_f, of_ref), (qkv_b, sm_b, ob_ref))):
        rev = d == 1
        blk = qkv_ref[0]
        q = blk[:, 0:width]
        k = blk[:, width:2 * width]
        v = blk[:, 2 * width:3 * width]
        a = _bdot(sm_ref[0], w2_ref[...])[:, d * width:(d + 1) * width] + b2_ref[d:d + 1, :]
        la = _log_sigmoid(a) * (1.0 / GL_TAU)
        g = _dot_exact_lhs(_tri(n_l, upper=rev), la)
        last = 0 if rev else n_l - 1
        g_end = g[last:last + 1, :]
        state = states[d]
        o_inter = lax.dot_general((q * jnp.exp(g)).astype(BF16), state.astype(BF16),
                                  (((1,), (1,)), ((), ())), preferred_element_type=F32)
        v_bf = v.astype(BF16)
        a_blocks = []
        t_idx = lax.broadcasted_iota(jnp.int32, (GL_HEADS * GL_SUB, n_l), 0) & (GL_SUB - 1)
        s_idx = lax.broadcasted_iota(jnp.int32, (GL_HEADS * GL_SUB, n_l), 1)
        for s in range(n_sub):
            lo = s * GL_SUB
            hi = lo + GL_SUB
            r = g[hi - 1:hi, :] if rev else g[lo:lo + 1, :]
            qt = q[lo:hi] * jnp.exp(g[lo:hi] - r)
            kt = k * jnp.exp(jnp.minimum(r - g, GL_EXP_CLAMP))
            qstack = jnp.concatenate([jnp.where(hm, qt, 0.0) for hm in head_masks], axis=0)
            amat = lax.dot_general(qstack.astype(BF16), kt.astype(BF16),
                                   (((1,), (1,)), ((), ())), preferred_element_type=F32)
            keep = (s_idx >= lo + t_idx) if rev else (s_idx <= lo + t_idx)
            a_blocks.append(jnp.where(keep, amat, 0.0).astype(BF16))
        a_all = jnp.concatenate([a_blocks[s][h * GL_SUB:(h + 1) * GL_SUB]
                                 for h in range(GL_HEADS) for s in range(n_sub)], axis=0)
        ov = jnp.dot(a_all, v_bf, preferred_element_type=F32)
        o_intra = jnp.where(chunk_masks[0], ov[0:n_l], 0.0)
        for h in range(1, GL_HEADS):
            o_intra = o_intra + jnp.where(chunk_masks[h], ov[h * n_l:(h + 1) * n_l], 0.0)
        new_outs.append(o_inter + o_intra)

        kg = k * jnp.exp(g_end - g)
        upd = jnp.dot(v.T.astype(BF16), kg.astype(BF16), preferred_element_type=F32)
        new_states.append(jnp.exp(g_end) * state + jnp.where(bd_mask, upd, 0.0))
    of_ref[0] = new_outs[0]
    ob_ref[0] = new_outs[1]
    s_scr[0] = new_states[0]
    s_scr[1] = new_states[1]

    @pl.when(i == pl.num_programs(1) - 1)
    def _():
        st_ref[0] = s_scr[...]


def _gla_call(gl, small, w2e, b2, s0):
    b_, t, _ = gl.shape
    chunk = min(GL_CHUNK, t)
    n = t // chunk
    fwd = lambda b, i: (b, i, 0)
    bwd = lambda b, i: (b, n - 1 - i, 0)
    state_spec = pl.BlockSpec((1, 2, GL_W, GL_W), lambda b, i: (b, 0, 0, 0))
    return pl.pallas_call(
        functools.partial(_gla_kernel, chunk=chunk),
        grid=(b_, n),
        in_specs=[pl.BlockSpec((1, chunk, 3 * GL_W), fwd), pl.BlockSpec((1, chunk, SMALL_W), fwd),
                  pl.BlockSpec((1, chunk, 3 * GL_W), bwd), pl.BlockSpec((1, chunk, SMALL_W), bwd),
                  _resident(w2e.shape), _resident(b2.shape), state_spec],
        out_specs=[pl.BlockSpec((1, chunk, GL_W), fwd), pl.BlockSpec((1, chunk, GL_W), bwd), state_spec],
        out_shape=[jax.ShapeDtypeStruct((b_, t, GL_W), F32), jax.ShapeDtypeStruct((b_, t, GL_W), F32),
                   jax.ShapeDtypeStruct(s0.shape, F32)],
        scratch_shapes=[pltpu.VMEM((2, GL_W, GL_W), F32)],
        compiler_params=_cparams("parallel", "arbitrary"),
        name="gla",
    )(gl, small, gl, small, w2e, b2, s0)


def _qkprep_kernel(gq_ref, cos_ref, sin_ref, gq_g, gk_g, sb_ref, qt_out, k_out, vt_out):
    x = gq_ref[0]
    tm = x.shape[0]

    def norm_rope(z, g, reps):
        width = z.shape[1]
        msq = _group_mean(z * z, _group_ones(width))
        zn = z * lax.rsqrt(msq + LN_EPS) * jnp.tile(g, (1, reps))
        lane = lax.broadcasted_iota(jnp.int32, zn.shape, 1)
        partner = jnp.where((lane & 31) < ROPE_PAIRS,
                            pltpu.roll(zn, width - ROPE_PAIRS, axis=1),
                            pltpu.roll(zn, ROPE_PAIRS, axis=1))
        return zn * jnp.tile(cos_ref[...], (1, reps)) + partner * jnp.tile(sin_ref[...], (1, reps))

    lane = lax.broadcasted_iota(jnp.int32, (tm, 128), 1)

    def pad_heads(z, extra):
        out = []
        for p in range(z.shape[1] // 128):
            pair = z[:, p * 128:(p + 1) * 128]
            for base in (pair, pltpu.roll(pair, HEAD_DIM, axis=1)):
                out.append(jnp.where(lane < HEAD_DIM, base, jnp.where(lane == HEAD_DIM, extra, 0.0)))
        return out

    q = norm_rope(x[:, 0:GQ_W], gq_g[...], GQ_W // 128) * (LOG2_E * HEAD_DIM ** -0.5)
    qt_out[0] = jnp.concatenate(pad_heads(q, 1.0), axis=1).T.astype(BF16)
    k = norm_rope(x[:, GQ_W:GQ_W + GQ_KW], gk_g[...], GQ_KW // 128)
    for j, kj in enumerate(pad_heads(k, sb_ref[0:1, 0:1])):
        k_out[0, j] = kj.astype(BF16)
    v = x[:, GQ_W + GQ_KW:GQ_W + 2 * GQ_KW]
    v_t = v.T
    sub = lax.broadcasted_iota(jnp.int32, (V_ROWS - HEAD_DIM, tm), 0)
    ones_rows = jnp.where(sub == 0, 1.0, 0.0)
    vt_out[0] = jnp.concatenate([v_t[0:HEAD_DIM], ones_rows, v_t[HEAD_DIM:2 * HEAD_DIM], ones_rows],
                                axis=0).astype(BF16)


def _qkprep_call(gq, cos, sin, gq_g, gk_g, score_bias, tm):
    b_, t, w = gq.shape
    n_q = GQ_KV * GQ_G
    return pl.pallas_call(
        _qkprep_kernel,
        grid=(b_, t // tm),
        in_specs=[pl.BlockSpec((1, tm, w), lambda b, i: (b, i, 0)),
                  pl.BlockSpec((tm, 128), lambda b, i: (i, 0)),
                  pl.BlockSpec((tm, 128), lambda b, i: (i, 0)),
                  _resident(gq_g.shape), _resident(gk_g.shape), _resident(score_bias.shape)],
        out_specs=[pl.BlockSpec((1, n_q * 128, tm), lambda b, i: (b, 0, i)),
                   pl.BlockSpec((1, GQ_KV, tm, 128), lambda b, i: (b, 0, i, 0)),
                   pl.BlockSpec((1, GQ_KV * V_ROWS, tm), lambda b, i: (b, 0, i))],
        out_shape=[jax.ShapeDtypeStruct((b_, n_q * 128, t), BF16),
                   jax.ShapeDtypeStruct((b_, GQ_KV, t, 128), BF16),
                   jax.ShapeDtypeStruct((b_, GQ_KV * V_ROWS, t), BF16)],
        compiler_params=_cparams("parallel", "parallel"),
        name="qkprep",
    )(gq, cos, sin, gq_g, gk_g, score_bias)


def _attn_kernel(qt_ref, k_ref, vt_ref, o_ref, m_scr, acc_scr, *, n_tiles, tk, bounded):
    acc_scr[...] = jnp.zeros(acc_scr.shape, F32)
    if not bounded:
        m_scr[...] = jnp.full(m_scr.shape, NEG_BIG, F32)

    def body(j, carry):
        start = pl.multiple_of(j * tk, tk)
        k_tile = k_ref[0, 0, pl.ds(start, tk), :]
        vt_tile = vt_ref[0, :, pl.ds(start, tk)]
        for g in range(GQ_G):
            st = jnp.dot(k_tile, qt_ref[0, g * 128:(g + 1) * 128, :], preferred_element_type=F32)
            if bounded:
                acc_scr[g] += jnp.dot(vt_tile, jnp.exp2(st).astype(BF16), preferred_element_type=F32)
            else:
                m_prev = m_scr[g]
                m_new = jnp.maximum(m_prev, jnp.max(st, axis=0, keepdims=True))
                p = jnp.exp2(st - m_new).astype(BF16)
                acc_scr[g] = (jnp.exp2(m_prev - m_new) * acc_scr[g]
                              + jnp.dot(vt_tile, p, preferred_element_type=F32))
                m_scr[g] = m_new
        return carry

    lax.fori_loop(0, n_tiles, body, 0)
    outs = []
    for g in range(GQ_G):
        acc = acc_scr[g]
        outs.append(acc[0:HEAD_DIM] / acc[HEAD_DIM:HEAD_DIM + 1])
    o_ref[0] = jnp.concatenate(outs, axis=0).T


def _attn_call(qt, k, vt, logit_bound):
    b_, _, t = qt.shape
    n_keys = k.shape[2]
    tq = min(ATT_TQ, t)
    tk = ATT_TK if n_keys % ATT_TK == 0 else n_keys
    gw = GQ_G * HEAD_DIM

    def call(bounded):
        return pl.pallas_call(
            functools.partial(_attn_kernel, n_tiles=n_keys // tk, tk=tk, bounded=bounded),
            grid=(b_, GQ_KV, t // tq),
            in_specs=[pl.BlockSpec((1, GQ_G * 128, tq), lambda b, kv, i: (b, kv, i)),
                      pl.BlockSpec((1, 1, n_keys, 128), lambda b, kv, i: (b, kv, 0, 0)),
                      pl.BlockSpec((1, V_ROWS, n_keys), lambda b, kv, i: (b, kv, 0))],
            out_specs=pl.BlockSpec((1, tq, gw), lambda b, kv, i: (b, i, kv)),
            out_shape=jax.ShapeDtypeStruct((b_, t, GQ_W), F32),
            scratch_shapes=[pltpu.VMEM((GQ_G, 1, tq), F32),
                            pltpu.VMEM((GQ_G, V_ROWS, tq), F32)],
            compiler_params=_cparams("parallel", "parallel", "arbitrary"),
            name="attention_bounded" if bounded else "attention_online",
        )(qt, k, vt)

    return lax.cond(logit_bound <= MAX_LOGIT_BOUND, lambda: call(True), lambda: call(False))


def _fourier_factors(t):
    bits = int(round(math.log2(t)))
    assert 1 << bits == t
    n1 = 1 << (bits // 2)
    return n1, t // n1


def _hi_lo(a):
    a = np.asarray(a, np.float64)
    hi = jnp.asarray(a, F32).astype(BF16)
    lo = (jnp.asarray(a, F32) - hi.astype(F32)).astype(BF16)
    return hi, lo


def _fourier_tables(t):
    n1, n2 = _fourier_factors(t)
    c = np.arange(FT_GC)
    ang = 2.0 * np.pi * np.outer(c, c) / FT_GC
    eye = np.eye(FT_GROUPS)
    w0 = np.concatenate([np.kron(eye, np.cos(ang)), -np.kron(eye, np.sin(ang))], axis=1)
    a1 = 2.0 * np.pi * np.outer(np.arange(n1), np.arange(n1)) / n1
    fr, fi = np.cos(a1), -np.sin(a1)
    m1 = np.block([[fr, -fi], [fi, fr]])
    tw = 2.0 * np.pi * np.outer(np.arange(n1), np.arange(n2)) / t
    a2 = 2.0 * np.pi * np.outer(np.arange(n2), np.arange(n2)) / n2
    m2 = np.concatenate([np.cos(a2), np.sin(a2)], axis=1) / math.sqrt(t * FT_GC)
    return dict(n1=n1, n2=n2, w0=_hi_lo(w0), m1=_hi_lo(m1), m2=_hi_lo(m2),
                twr=np.cos(tw).astype(np.float32), twi=(-np.sin(tw)).astype(np.float32))


def _ft_stage1_kernel(u_ref, w0h, w0l, m1h, m1l, twr_ref, twi_ref, yr_ref, yi_ref, *, n_tok):
    n1 = u_ref.shape[1]
    zr, zi = [], []
    for j in range(n_tok):
        uh, ul = _split2(u_ref[0, :, j * FT_W:(j + 1) * FT_W])
        z = _dot3(uh, ul, w0h[...], w0l[...])
        zr.append(z[:, :FT_W])
        zi.append(z[:, FT_W:])
    z = jnp.concatenate([jnp.concatenate(zr, axis=1), jnp.concatenate(zi, axis=1)], axis=0)
    zh, zl = _split2(z)
    y = _dot3(m1h[...], m1l[...], zh, zl)
    y_r, y_i = y[:n1], y[n1:]
    twr, twi = twr_ref[0], twi_ref[0]
    for j in range(n_tok):
        sl = slice(j * FT_W, (j + 1) * FT_W)
        cr, ci = twr[:, j:j + 1], twi[:, j:j + 1]
        yr_ref[0, :, j, :] = y_r[:, sl] * cr - y_i[:, sl] * ci
        yi_ref[0, :, j, :] = y_r[:, sl] * ci + y_i[:, sl] * cr


def _ft_stage2_kernel(yr_ref, yi_ref, m2h, m2l, o_ref, *, n_k1, n2):
    for j in range(n_k1):
        y = jnp.concatenate([yr_ref[0, j * n2:(j + 1) * n2, :], yi_ref[0, j * n2:(j + 1) * n2, :]], axis=0)
        yh, yl = _split2(y)
        o_ref[0, :, j, :] = _dot3(m2h[...], m2l[...], yh, yl)


def _fourier_call(u, tabs):
    b_, t, w = u.shape
    n1, n2 = tabs["n1"], tabs["n2"]
    n_tok = min(FT_TOKENS_PER_STEP, n2)
    twr = jnp.asarray(tabs["twr"]).reshape(n1, n2 // n_tok, n_tok).transpose(1, 0, 2)
    twi = jnp.asarray(tabs["twi"]).reshape(n1, n2 // n_tok, n_tok).transpose(1, 0, 2)
    grid_view = jax.ShapeDtypeStruct((b_, n1, n2, w), F32)
    tok_block = pl.BlockSpec((1, n1, n_tok, w), lambda b, j: (b, 0, j, 0))
    yr, yi = pl.pallas_call(
        functools.partial(_ft_stage1_kernel, n_tok=n_tok),
        grid=(b_, n2 // n_tok),
        in_specs=[pl.BlockSpec((1, n1, n_tok * w), lambda b, j: (b, 0, j)),
                  _resident(tabs["w0"][0].shape), _resident(tabs["w0"][1].shape),
                  _resident(tabs["m1"][0].shape), _resident(tabs["m1"][1].shape),
                  pl.BlockSpec((1, n1, n_tok), lambda b, j: (j, 0, 0)),
                  pl.BlockSpec((1, n1, n_tok), lambda b, j: (j, 0, 0))],
        out_specs=[tok_block] * 2,
        out_shape=[grid_view, grid_view],
        compiler_params=_cparams("parallel", "parallel"),
        name="fourier_stage1",
    )(u.reshape(b_, n1, n2 * w), *tabs["w0"], *tabs["m1"], twr, twi)
    n_k1 = min(FT_TOKENS_PER_STEP, n1)
    out = pl.pallas_call(
        functools.partial(_ft_stage2_kernel, n_k1=n_k1, n2=n2),
        grid=(b_, n1 // n_k1),
        in_specs=[pl.BlockSpec((1, n_k1 * n2, w), lambda b, j: (b, j, 0)),
                  pl.BlockSpec((1, n_k1 * n2, w), lambda b, j: (b, j, 0)),
                  _resident(tabs["m2"][0].shape), _resident(tabs["m2"][1].shape)],
        out_specs=pl.BlockSpec((1, n2, n_k1, w), lambda b, j: (b, 0, j, 0)),
        out_shape=jax.ShapeDtypeStruct((b_, n2, n1, w), F32),
        compiler_params=_cparams("parallel", "parallel"),
        name="fourier_stage2",
    )(yr.reshape(b_, t, w), yi.reshape(b_, t, w), *tabs["m2"])
    return out.reshape(b_, t, w)


def _merge_kernel(x_ref, mod_ref, hf_ref, hb_ref, mlo_ref, ft_ref, att_ref, of_ref, ob_ref, glr_ref,
                  wgate, bgate, wbr, wout, mlg, glg, lng, lnb, o_ref, *, alpha):
    x = x_ref[0]
    h = _modulate(x, mod_ref[0, 0:1, :], mod_ref[0, 1:2, :]).astype(BF16)
    ones = _group_ones(ML_W)
    hs = hf_ref[0] + hb_ref[0]
    hc = hs - _group_mean(hs, ones)
    hn = hc * lax.rsqrt(_group_mean(hc * hc, ones) + LN_EPS) * mlg[...]
    br_ml = hn * _sigmoid(mlo_ref[0])
    os_ = of_ref[0] + ob_ref[0]
    on = os_ * lax.rsqrt(_group_mean(os_ * os_, ones) + LN_EPS) * glg[...]
    br_gl = on * _silu(glr_ref[0])
    branches = (br_ml, ft_ref[0], att_ref[0], br_gl)
    mixed = None
    off = 0
    for j, br in enumerate(branches):
        wd = br.shape[1]
        gate = _sigmoid(jnp.dot(h, wgate[:, j * D_MODEL:(j + 1) * D_MODEL], preferred_element_type=F32)
                        + bgate[:, j * D_MODEL:(j + 1) * D_MODEL])
        u = gate * jnp.dot(br.astype(BF16), wbr[off:off + wd, :], preferred_element_type=F32)
        mixed = u if mixed is None else mixed + u
        off += wd
    y = jnp.dot(mixed.astype(BF16), wout[...], preferred_element_type=F32)
    o_ref[0] = _layernorm(alpha * x + mod_ref[0, 2:3, :] * y) * lng[...] + lnb[...]


def _merge_call(x, mod, hf, hb, ml, ftb, att, of, ob, gl, weights, alpha, tm):
    b_, t, d = x.shape
    tok = lambda w: pl.BlockSpec((1, tm, w), lambda b, i: (b, i, 0))
    last_quarter = pl.BlockSpec((1, tm, ML_W), lambda b, i: (b, i, 3))
    return pl.pallas_call(
        functools.partial(_merge_kernel, alpha=alpha),
        grid=(b_, t // tm),
        in_specs=[tok(d), pl.BlockSpec((1, N_ADA, d), lambda b, i: (b, 0, 0)),
                  tok(ML_W), tok(ML_W), last_quarter, tok(FT_W), tok(GQ_W), tok(GL_W), tok(GL_W),
                  last_quarter] + [_resident(w.shape) for w in weights],
        out_specs=tok(d),
        out_shape=jax.ShapeDtypeStruct((b_, t, d), F32),
        compiler_params=_cparams("parallel", "parallel"),
        name="merge",
    )(x, mod, hf, hb, ml, ftb, att, of, ob, gl, *weights)


def _ffn_kernel(x_ref, mod_ref, wg, wu, wd, lng, lnb, o_ref, *, alpha):
    x = x_ref[0]
    h = _modulate(x, mod_ref[0, 3:4, :], mod_ref[0, 4:5, :]).astype(BF16)
    a = jnp.dot(h, wg[...], preferred_element_type=F32)
    u = jnp.dot(h, wu[...], preferred_element_type=F32)
    f = jnp.dot((_silu(a) * u).astype(BF16), wd[...], preferred_element_type=F32)
    o_ref[0] = _layernorm(alpha * x + mod_ref[0, 5:6, :] * f) * lng[...] + lnb[...]


def _ffn_call(x, mod, weights, alpha, tm):
    b_, t, d = x.shape
    return pl.pallas_call(
        functools.partial(_ffn_kernel, alpha=alpha),
        grid=(b_, t // tm),
        in_specs=[pl.BlockSpec((1, tm, d), lambda b, i: (b, i, 0)),
                  pl.BlockSpec((1, N_ADA, d), lambda b, i: (b, 0, 0))]
                 + [_resident(w.shape) for w in weights],
        out_specs=pl.BlockSpec((1, tm, d), lambda b, i: (b, i, 0)),
        out_shape=jax.ShapeDtypeStruct((b_, t, d), F32),
        compiler_params=_cparams("parallel", "parallel"),
        name="ffn",
    )(x, mod, *weights)


def _router_kernel(x_ref, mod_ref, wrh, wrl, br_ref, h_out, dense_out, pos_out, post_out, cnt_out):
    tm = x_ref.shape[1]
    h = _modulate(x_ref[0], mod_ref[0, 3:4, :], mod_ref[0, 4:5, :])
    h_out[0] = h.astype(BF16)
    hh, hl = _split2(h)
    logits = _dot3(hh, hl, wrh[...], wrl[...])
    lane = lax.broadcasted_iota(jnp.int32, logits.shape, 1).astype(F32)
    sel = jnp.where(lane < N_EXPERTS, logits + br_ref[...], -jnp.inf)

    def pick(scores):
        mx = jnp.max(scores, axis=1, keepdims=True)
        idx = jnp.min(jnp.where(scores == mx, lane, 2.0 * SMALL_W), axis=1, keepdims=True)
        return lane == idx

    first = pick(sel)
    second = pick(jnp.where(first, -jnp.inf, sel))
    l1 = jnp.sum(jnp.where(first, logits, 0.0), axis=1, keepdims=True)
    l2 = jnp.sum(jnp.where(second, logits, 0.0), axis=1, keepdims=True)
    mx = jnp.maximum(l1, l2)
    e1, e2 = jnp.exp(l1 - mx), jnp.exp(l2 - mx)
    inv = 1.0 / (e1 + e2)
    dense_out[0] = jnp.where(first, e1 * inv, 0.0) + jnp.where(second, e2 * inv, 0.0)
    routed = first | second
    counts = jnp.dot(_tri(tm, upper=False), jnp.where(routed, 1.0, 0.0).astype(BF16),
                     preferred_element_type=F32)
    pos = jnp.where(routed, counts - 1.0, -1.0)
    pos_out[0] = pos
    post_out[0] = pos.T[0:8, :]
    cnt_out[0, 0] = jnp.broadcast_to(counts[tm - 1:tm, :], (8, SMALL_W))


def _router_call(x, mod, wrh, wrl, br, tm):
    b_, t, d = x.shape
    nt = t // tm
    tok = lambda w: pl.BlockSpec((1, tm, w), lambda b, i: (b, i, 0))
    return pl.pallas_call(
        _router_kernel,
        grid=(b_, nt),
        in_specs=[tok(d), pl.BlockSpec((1, N_ADA, d), lambda b, i: (b, 0, 0)),
                  _resident(wrh.shape), _resident(wrl.shape), _resident(br.shape)],
        out_specs=[tok(d), tok(SMALL_W), tok(SMALL_W),
                   pl.BlockSpec((1, 8, tm), lambda b, i: (b, 0, i)),
                   pl.BlockSpec((1, 1, 8, SMALL_W), lambda b, i: (b, i, 0, 0))],
        out_shape=[jax.ShapeDtypeStruct((b_, t, d), BF16), jax.ShapeDtypeStruct((b_, t, SMALL_W), F32),
                   jax.ShapeDtypeStruct((b_, t, SMALL_W), F32), jax.ShapeDtypeStruct((b_, 8, t), F32),
                   jax.ShapeDtypeStruct((b_, nt, 8, SMALL_W), F32)],
        compiler_params=_cparams("parallel", "parallel"),
        name="router",
    )(x, mod, wrh, wrl, br)


def _moe_kernel(cnt_ref, x_ref, mod_ref, h_ref, dense_ref, pos_ref, post_ref, wg, wu, wd, lng, lnb,
                o_ref, acc_scr, *, alpha, rows):
    b, i, e = pl.program_id(0), pl.program_id(1), pl.program_id(2)
    tm = x_ref.shape[1]

    @pl.when(e == 0)
    def _():
        acc_scr[...] = jnp.zeros(acc_scr.shape, F32)

    lane = lax.broadcasted_iota(jnp.int32, (tm, SMALL_W), 1)
    w_col = jnp.sum(jnp.where(lane == e, dense_ref[0], 0.0), axis=1, keepdims=True)
    pos_col = jnp.sum(jnp.where(lane == e, pos_ref[0], 0.0), axis=1, keepdims=True)
    pos_row = post_ref[0, pl.ds(e, 1), :]
    count = cnt_ref[(b * pl.num_programs(1) + i) * N_EXPERTS + e]

    def run_block(first_row, n):
        base = first_row.astype(F32)
        r_iota = lax.broadcasted_iota(jnp.int32, (n, 1), 0).astype(F32) + base
        c_iota = lax.broadcasted_iota(jnp.int32, (1, n), 1).astype(F32) + base
        gather = jnp.where(pos_row == r_iota, 1.0, 0.0).astype(BF16)
        xs = jnp.dot(gather, h_ref[0], preferred_element_type=F32).astype(BF16)
        a = jnp.dot(xs, wg[0], preferred_element_type=F32)
        u = jnp.dot(xs, wu[0], preferred_element_type=F32)
        f = jnp.dot((_silu(a) * u).astype(BF16), wd[0], preferred_element_type=F32)
        scatter = jnp.where(pos_col == c_iota, 1.0, 0.0).astype(BF16)
        acc_scr[...] += w_col * jnp.dot(scatter, f.astype(BF16), preferred_element_type=F32)

    n_full = count // rows
    rem = count - n_full * rows
    half = rows // 2

    def full_block(j, carry):
        run_block(j * rows, rows)
        return carry

    lax.fori_loop(0, n_full, full_block, 0)

    @pl.when(rem > half)
    def _():
        run_block(n_full * rows, rows)

    @pl.when((rem > 0) & (rem <= half))
    def _():
        run_block(n_full * rows, half)

    @pl.when(e == pl.num_programs(2) - 1)
    def _():
        o_ref[0] = (_layernorm(alpha * x_ref[0] + mod_ref[0, 5:6, :] * acc_scr[...]) * lng[...]
                    + lnb[...])


def _moe_call(x, mod, wrh, wrl, br, wg, wu, wd, lng, lnb, alpha, tm):
    b_, t, d = x.shape
    n_e, _, ff = wg.shape
    nt = t // tm
    h, dense, pos, post, cnt = _router_call(x, mod, wrh, wrl, br, tm)
    counts = cnt[:, :, 0, 0:n_e].astype(jnp.int32).reshape(-1)
    tok = lambda w: pl.BlockSpec((1, tm, w), lambda b, i, e, c: (b, i, 0))
    grid_spec = pltpu.PrefetchScalarGridSpec(
        num_scalar_prefetch=1,
        grid=(b_, nt, n_e),
        in_specs=[tok(d), pl.BlockSpec((1, N_ADA, d), lambda b, i, e, c: (b, 0, 0)),
                  tok(d), tok(SMALL_W), tok(SMALL_W),
                  pl.BlockSpec((1, 8, tm), lambda b, i, e, c: (b, 0, i)),
                  pl.BlockSpec((1, d, ff), lambda b, i, e, c: (e, 0, 0)),
                  pl.BlockSpec((1, d, ff), lambda b, i, e, c: (e, 0, 0)),
                  pl.BlockSpec((1, ff, d), lambda b, i, e, c: (e, 0, 0)),
                  _resident(lng.shape), _resident(lnb.shape)],
        out_specs=tok(d),
        scratch_shapes=[pltpu.VMEM((tm, d), F32)])
    return pl.pallas_call(
        functools.partial(_moe_kernel, alpha=alpha, rows=min(MOE_ROWS, tm)),
        grid_spec=grid_spec,
        out_shape=jax.ShapeDtypeStruct((b_, t, d), F32),
        compiler_params=_cparams("parallel", "parallel", "arbitrary"),
        name="moe",
    )(counts, x, mod, h, dense, pos, post, wg, wu, wd, lng, lnb)


def _rope_tables(t):
    rows = t // GRID_W
    row = jnp.repeat(jnp.arange(rows, dtype=F32), GRID_W)
    col = jnp.tile(jnp.arange(GRID_W, dtype=F32), rows)
    inv = jnp.power(ROPE_BASE, -jnp.arange(ROPE_PAIRS, dtype=F32) / ROPE_PAIRS)
    ar, ac = row[:, None] * inv, col[:, None] * inv
    cos = jnp.concatenate([jnp.cos(ar), jnp.cos(ar), jnp.cos(ac), jnp.cos(ac)], axis=1)
    sin = jnp.concatenate([-jnp.sin(ar), jnp.sin(ar), -jnp.sin(ac), jnp.sin(ac)], axis=1)
    return jnp.tile(cos, (1, 2)), jnp.tile(sin, (1, 2))


def _token_tile(t, pref):
    return pref if t % pref == 0 else t


def kernel(x, c, ctx, c_ctx, w_ada, b_ada, w_in, ml_gate_b, ml_norm_g, gq_qnorm_g, gq_knorm_g, gl_w2, gl_b2,
           gl_norm_g, w_branch, w_gate, b_gate, w_out, ln1_g, ln1_b, ln2_g, ln2_b, ffd_wg, ffd_wu, ffd_wd,
           moe_wr, moe_br, moe_wg, moe_wu, moe_wd):
    b_, s_len, d = x.shape
    n_ctx = ctx.shape[1]
    depth = w_in.shape[0]
    alpha = (2.0 * depth) ** 0.25
    bf = lambda a: a.astype(BF16)
    row = lambda a: a.reshape(1, -1).astype(F32)

    cos_l, sin_l = _rope_tables(s_len)
    cos_c, sin_c = jnp.ones((n_ctx, 128), F32), jnp.zeros((n_ctx, 128), F32)
    ft_l, ft_c = _fourier_tables(s_len), _fourier_tables(n_ctx)
    tm_l, tm_c = _token_tile(s_len, 512), _token_tile(n_ctx, 512)

    cc = jnp.zeros((8, d), F32).at[0:b_].set(c).at[b_].set(c_ctx)
    xc = ctx
    for l in range(depth):
        last = l == depth - 1
        ada = _ada_call(cc, bf(w_ada[l]), row(b_ada[l])).reshape(8, N_ADA, d)
        mod_l = ada[0:b_]
        mod_c = jnp.broadcast_to(ada[b_][None], (b_, N_ADA, d))

        w = w_in[l]
        o_gate, o_ft, o_gq, o_gl, o_af = 4 * ML_W, 4 * ML_W + 16, 4 * ML_W + 16 + FT_W, \
            4 * ML_W + 16 + FT_W + GQ_W + 2 * GQ_KW, 4 * ML_W + 16 + FT_W + GQ_W + 2 * GQ_KW + 4 * GL_W
        ml_scale = jnp.concatenate([jnp.ones((ML_W,)), jnp.full((ML_W,), HEAD_DIM ** -0.5), jnp.ones((2 * ML_W,))])
        gl_scale = jnp.concatenate([jnp.full((GL_W,), HEAD_DIM ** -0.5), jnp.ones((3 * GL_W,))])
        w_small = jnp.concatenate([w[:, o_gate:o_gate + 16], w[:, o_af:o_af + 2 * GL_RANK],
                                   jnp.zeros((d, SMALL_W - 16 - 2 * GL_RANK), F32)], axis=1)
        ws = [bf(w[:, 0:4 * ML_W] * ml_scale), bf(w[:, o_ft:o_ft + FT_W]), bf(w[:, o_gq:o_gl]),
              bf(w[:, o_gl:o_af] * gl_scale), bf(w_small)]
        gate_bias = jnp.zeros((1, SMALL_W), F32).at[0, 0:16].set(ml_gate_b[l].reshape(-1))
        w2e = jnp.zeros((SMALL_W, 2 * GL_W), F32)
        w2e = w2e.at[16:16 + GL_RANK, 0:GL_W].set(gl_w2[l, 0]).at[16 + GL_RANK:16 + 2 * GL_RANK, GL_W:].set(gl_w2[l, 1])
        merge_w = [bf(w_gate[l]), row(b_gate[l]), bf(w_branch[l]), bf(w_out[l]), row(ml_norm_g[l]),
                   row(gl_norm_g[l]), row(ln1_g[l]), row(ln1_b[l])]
        gq_g = jnp.tile(row(gq_qnorm_g[l]), (1, 2))
        gk_g = jnp.tile(row(gq_knorm_g[l]), (1, 2))
        logit_bound = (1.02 * LOG2_E * HEAD_DIM ** 0.5) * jnp.max(jnp.abs(gq_qnorm_g[l])) * jnp.max(jnp.abs(gq_knorm_g[l]))
        score_bias = jnp.zeros((1, HEAD_DIM), F32).at[0, 0].set(-logit_bound)

        def mixers(xs, mod, tm, cos, sin, states):
            ml, ftu, gq, gl, small = _inproj_call(xs, mod, ws, tm)
            hf, hb, ml_s, ml_m = _mlstm_call(ml, small, gate_bias, states[0], states[1])
            of, ob, gl_s = _gla_call(gl, small, bf(w2e), gl_b2[l].astype(F32), states[2])
            qt, k, vt = _qkprep_call(gq, cos, sin, gq_g, gk_g, score_bias, tm)
            return dict(ml=ml, ftu=ftu, gl=gl, hf=hf, hb=hb, of=of, ob=ob, qt=qt, k=k, vt=vt), (ml_s, ml_m, gl_s)

        zero_states = (jnp.zeros((b_, 2 * ML_HEADS, 2 * HEAD_DIM, 2 * HEAD_DIM), F32),
                       jnp.zeros((b_, 2 * ML_HEADS, 1, SMALL_W), F32),
                       jnp.zeros((b_, 2, GL_W, GL_W), F32))
        pc, ctx_states = mixers(xc, mod_c, tm_c, cos_c, sin_c, zero_states)
        pl_, _ = mixers(x, mod_l, tm_l, cos_l, sin_l, ctx_states)

        att_l = _attn_call(pl_["qt"], jnp.concatenate([pc["k"], pl_["k"]], axis=2),
                           jnp.concatenate([pc["vt"], pl_["vt"]], axis=2), logit_bound)
        x_mid = _merge_call(x, mod_l, pl_["hf"], pl_["hb"], pl_["ml"], _fourier_call(pl_["ftu"], ft_l), att_l,
                            pl_["of"], pl_["ob"], pl_["gl"], merge_w, alpha, tm_l)
        if not last:
            att_c = _attn_call(pc["qt"], pc["k"], pc["vt"], logit_bound)
            xc = _merge_call(xc, mod_c, pc["hf"], pc["hb"], pc["ml"], _fourier_call(pc["ftu"], ft_c), att_c,
                             pc["of"], pc["ob"], pc["gl"], merge_w, alpha, tm_c)

        j = l // 2
        if l % 2 == 0:
            ffn_w = [bf(ffd_wg[j]), bf(ffd_wu[j]), bf(ffd_wd[j]), row(ln2_g[l]), row(ln2_b[l])]
            x = _ffn_call(x_mid, mod_l, ffn_w, alpha, tm_l)
            if not last:
                xc = _ffn_call(xc, mod_c, ffn_w, alpha, tm_c)
        else:
            wr = jnp.zeros((d, SMALL_W), F32).at[:, 0:N_EXPERTS].set(moe_wr[j])
            wrh = bf(wr)
            wrl = bf(wr - wrh.astype(F32))
            brp = jnp.zeros((1, SMALL_W), F32).at[0, 0:N_EXPERTS].set(moe_br[j])
            moe_args = (wrh, wrl, brp, bf(moe_wg[j]), bf(moe_wu[j]), bf(moe_wd[j]), row(ln2_g[l]), row(ln2_b[l]))
            x = _moe_call(x_mid, mod_l, *moe_args, alpha, _token_tile(s_len, 1024))
            if not last:
                xc = _moe_call(xc, mod_c, *moe_args, alpha, tm_c)
    return x
```

```python
import functools
import math

import jax
import jax.numpy as jnp
import numpy as np
from jax import lax
from jax.experimental import pallas as pl
from jax.experimental.pallas import tpu as pltpu

F32 = jnp.float32
BF16 = jnp.bfloat16

D_MODEL = 1024
GRID_W = 64
HEAD_DIM = 64
ML_HEADS = 4
ML_W = ML_HEADS * HEAD_DIM
FT_GROUPS = 4
FT_GC = 64
FT_W = FT_GROUPS * FT_GC
GQ_KV = 2
GQ_G = 4
GQ_W = GQ_KV * GQ_G * HEAD_DIM
GQ_KW = GQ_KV * HEAD_DIM
ROPE_PAIRS = HEAD_DIM // 4
ROPE_BASE = 10000.0
GL_HEADS = 4
GL_W = GL_HEADS * HEAD_DIM
GL_RANK = 16
GL_TAU = 16.0
N_EXPERTS = 8
N_ADA = 6
LN_EPS = 1e-6
SMALL_W = 128

ML_CHUNK = 256
GL_CHUNK = 128
GL_SUB = 16
GL_EXP_CLAMP = 80.0
ATT_TQ = 512
ATT_TK = 8320
ATT_ROWS = 256
LOG2_E = 1.4426950408889634
LN_2 = 0.6931471805599453
MAX_LOGIT_BOUND = 60.0
NEG_BIG = -1e30
MOE_ROWS = 256
FT_TOKENS_PER_STEP = 8
V_ROWS = 128

VMEM_LIMIT = 56 * 1024 * 1024


def _cparams(*sem):
    return pltpu.CompilerParams(dimension_semantics=sem, vmem_limit_bytes=VMEM_LIMIT)


def _resident(shape):
    nd = len(shape)
    return pl.BlockSpec(shape, lambda *_: (0,) * nd, pipeline_mode=pl.Buffered(1))


def _bdot(a, b):
    return jnp.dot(a.astype(BF16), b.astype(BF16), preferred_element_type=F32)


def _split2(x):
    hi = x.astype(BF16)
    lo = (x - hi.astype(F32)).astype(BF16)
    return hi, lo


def _split3(x):
    a = x.astype(BF16)
    r = x - a.astype(F32)
    b = r.astype(BF16)
    c = (r - b.astype(F32)).astype(BF16)
    return a, b, c


def _dot_exact_rhs(x, m_bf16):
    a, b, c = _split3(x)
    d = functools.partial(jnp.dot, preferred_element_type=F32)
    return d(a, m_bf16) + d(b, m_bf16) + d(c, m_bf16)


def _dot_exact_lhs(m_bf16, x):
    a, b, c = _split3(x)
    d = functools.partial(jnp.dot, preferred_element_type=F32)
    return d(m_bf16, a) + d(m_bf16, b) + d(m_bf16, c)


def _dot3(a_hi, a_lo, b_hi, b_lo):
    d = functools.partial(jnp.dot, preferred_element_type=F32)
    return d(a_hi, b_hi) + d(a_hi, b_lo) + d(a_lo, b_hi)


def _sigmoid(x):
    return 1.0 / (1.0 + jnp.exp(-x))


def _silu(x):
    return x * _sigmoid(x)


def _log_sigmoid(x):
    return jnp.minimum(x, 0.0) - LN_2 * jnp.log2(1.0 + jnp.exp2(-LOG2_E * jnp.abs(x)))


def _layernorm(x):
    mu = jnp.mean(x, axis=-1, keepdims=True)
    xc = x - mu
    var = jnp.mean(xc * xc, axis=-1, keepdims=True)
    return xc * lax.rsqrt(var + LN_EPS)


def _modulate(x, shift, scale):
    return _layernorm(x) * (1.0 + scale) + shift


def _group_ones(width):
    r = lax.broadcasted_iota(jnp.int32, (width, width), 0) >> 6
    c = lax.broadcasted_iota(jnp.int32, (width, width), 1) >> 6
    return jnp.where(r == c, 1.0, 0.0).astype(BF16)


def _group_mean(x, ones):
    return _dot_exact_rhs(x, ones) * (1.0 / HEAD_DIM)


def _tri(n, upper):
    r = lax.broadcasted_iota(jnp.int32, (n, n), 0)
    c = lax.broadcasted_iota(jnp.int32, (n, n), 1)
    keep = (c >= r) if upper else (c <= r)
    return jnp.where(keep, 1.0, 0.0).astype(BF16)


def _ada_kernel(c_ref, w_ref, b_ref, o_ref):
    o_ref[...] = _bdot(_silu(c_ref[...]), w_ref[...]) + b_ref[...]


def _ada_call(cc, w, b):
    rows, d = cc.shape
    n = w.shape[1]
    tn = 1024
    return pl.pallas_call(
        _ada_kernel,
        grid=(n // tn,),
        in_specs=[pl.BlockSpec((rows, d), lambda j: (0, 0)),
                  pl.BlockSpec((d, tn), lambda j: (0, j)),
                  pl.BlockSpec((1, tn), lambda j: (0, j))],
        out_specs=pl.BlockSpec((rows, tn), lambda j: (0, j)),
        out_shape=jax.ShapeDtypeStruct((rows, n), F32),
        compiler_params=_cparams("parallel"),
        name="ada",
    )(cc, w, b)


def _inproj_kernel(x_ref, mod_ref, w_ml, w_ft, w_gq, w_gl, w_sm, o_ml, o_ft, o_gq, o_gl, o_sm):
    h = _modulate(x_ref[0], mod_ref[0, 0:1, :], mod_ref[0, 1:2, :]).astype(BF16)
    for w, o in ((w_ml, o_ml), (w_ft, o_ft), (w_gq, o_gq), (w_gl, o_gl), (w_sm, o_sm)):
        o[0] = jnp.dot(h, w[...], preferred_element_type=F32)


def _inproj_call(x, mod, ws, tm):
    b_, t, d = x.shape
    widths = [w.shape[1] for w in ws]
    return pl.pallas_call(
        _inproj_kernel,
        grid=(b_, t // tm),
        in_specs=[pl.BlockSpec((1, tm, d), lambda b, i: (b, i, 0)),
                  pl.BlockSpec((1, N_ADA, d), lambda b, i: (b, 0, 0))]
                 + [_resident(w.shape) for w in ws],
        out_specs=[pl.BlockSpec((1, tm, n), lambda b, i: (b, i, 0)) for n in widths],
        out_shape=[jax.ShapeDtypeStruct((b_, t, n), F32) for n in widths],
        compiler_params=_cparams("parallel", "parallel"),
        name="inproj",
    )(x, mod, *ws)


def _mlstm_kernel(qkv_f, sm_f, qkv_b, sm_b, bias_ref, s0_ref, m0_ref,
                  hf_ref, hb_ref, st_ref, mt_ref, s_scr, m_scr, *, chunk):
    i = pl.program_id(1)
    n_l = chunk

    @pl.when(i == 0)
    def _():
        s_scr[...] = s0_ref[0]
        m_scr[...] = m0_ref[0]

    row = lax.broadcasted_iota(jnp.int32, (n_l, n_l), 0)
    col = lax.broadcasted_iota(jnp.int32, (n_l, n_l), 1)
    lane = lax.broadcasted_iota(jnp.int32, (n_l, 128), 1)
    sub = lax.broadcasted_iota(jnp.int32, (HEAD_DIM, n_l), 0)
    ones_row = jnp.where(sub == 0, 1.0, 0.0)
    for d, (qkv_ref, sm_ref, h_ref) in enumerate(((qkv_f, sm_f, hf_ref), (qkv_b, sm_b, hb_ref))):
        rev = d == 1
        blk = qkv_ref[0]
        pre = sm_ref[0] + bias_ref[...]
        bcum = _dot_exact_lhs(_tri(n_l, upper=rev), _log_sigmoid(pre))
        pre_t = pre.T
        b_t = bcum.T
        q_t = blk[:, 0:ML_W].T.astype(BF16)
        v_t = blk[:, 2 * ML_W:3 * ML_W].T
        mask = (row >= col) if rev else (row <= col)
        last = 0 if rev else n_l - 1
        outs = []
        for h in range(ML_HEADS):
            ci = 8 * d + h
            cf = 8 * d + 4 + h
            idx = 4 * d + h
            pair = h // 2
            own = (lane >= HEAD_DIM) if h % 2 else (lane < HEAD_DIM)
            k_own = jnp.where(own, blk[:, ML_W + pair * 128:ML_W + (pair + 1) * 128], 0.0).astype(BF16)
            q_pair = q_t[pair * 128:(pair + 1) * 128, :]
            v_h = v_t[h * HEAD_DIM:(h + 1) * HEAD_DIM, :]
            c_col = pre[:, ci:ci + 1] - bcum[:, cf:cf + 1]
            b_row = b_t[cf:cf + 1, :]
            i_row = pre_t[ci:ci + 1, :]
            state = s_scr[idx]
            m_prev = m_scr[idx][:, 0:1]

            dmat = jnp.where(mask, b_row + c_col, -jnp.inf)
            inter = b_row + m_prev
            m_t = jnp.maximum(inter, jnp.max(dmat, axis=0, keepdims=True))
            w_intra = jnp.exp(dmat - m_t) * jnp.dot(k_own, q_pair, preferred_element_type=F32)
            w_inter = jnp.exp(inter - m_t)
            sq = jnp.dot(state.astype(BF16), q_pair, preferred_element_type=F32)
            num = (jnp.dot(v_h.astype(BF16), w_intra.astype(BF16), preferred_element_type=F32)
                   + w_inter * sq[0:HEAD_DIM])
            den = jnp.sum(w_intra, axis=0, keepdims=True) + w_inter * sq[HEAD_DIM:HEAD_DIM + 1]
            outs.append(num / jnp.maximum(jnp.abs(den), jnp.exp(-m_t)))

            b_last = b_row[:, last:last + 1]
            g_row = b_last - b_row + i_row
            m_new = jnp.maximum(b_last + m_prev, jnp.max(g_row, axis=1, keepdims=True))
            ws = jnp.exp(g_row - m_new)
            wc = jnp.exp(b_last + m_prev - m_new)
            v_ext = jnp.concatenate([v_h, ones_row], axis=0)
            s_scr[idx] = wc * state + jnp.dot((v_ext * ws).astype(BF16), k_own, preferred_element_type=F32)
            m_scr[idx] = jnp.broadcast_to(m_new, (1, SMALL_W))
        h_ref[0] = jnp.concatenate(outs, axis=0).T

    @pl.when(i == pl.num_programs(1) - 1)
    def _():
        st_ref[0] = s_scr[...]
        mt_ref[0] = m_scr[...]


def _mlstm_call(ml, small, bias, s0, m0):
    b_, t, _ = ml.shape
    chunk = min(ML_CHUNK, t)
    n = t // chunk
    fwd = lambda b, i: (b, i, 0)
    bwd = lambda b, i: (b, n - 1 - i, 0)
    state_spec = pl.BlockSpec((1, 2 * ML_HEADS, 2 * HEAD_DIM, 2 * HEAD_DIM), lambda b, i: (b, 0, 0, 0))
    m_spec = pl.BlockSpec((1, 2 * ML_HEADS, 1, SMALL_W), lambda b, i: (b, 0, 0, 0))
    return pl.pallas_call(
        functools.partial(_mlstm_kernel, chunk=chunk),
        grid=(b_, n),
        in_specs=[pl.BlockSpec((1, chunk, 3 * ML_W), fwd), pl.BlockSpec((1, chunk, SMALL_W), fwd),
                  pl.BlockSpec((1, chunk, 3 * ML_W), bwd), pl.BlockSpec((1, chunk, SMALL_W), bwd),
                  pl.BlockSpec((1, SMALL_W), lambda b, i: (0, 0)), state_spec, m_spec],
        out_specs=[pl.BlockSpec((1, chunk, ML_W), fwd), pl.BlockSpec((1, chunk, ML_W), bwd),
                   state_spec, m_spec],
        out_shape=[jax.ShapeDtypeStruct((b_, t, ML_W), F32), jax.ShapeDtypeStruct((b_, t, ML_W), F32),
                   jax.ShapeDtypeStruct(s0.shape, F32), jax.ShapeDtypeStruct(m0.shape, F32)],
        scratch_shapes=[pltpu.VMEM((2 * ML_HEADS, 2 * HEAD_DIM, 2 * HEAD_DIM), F32),
                        pltpu.VMEM((2 * ML_HEADS, 1, SMALL_W), F32)],
        compiler_params=_cparams("parallel", "arbitrary"),
        name="mlstm",
    )(ml, small, ml, small, bias, s0, m0)


def _gla_kernel(qkv_f, sm_f, qkv_b, sm_b, w2_ref, b2_ref, s0_ref,
                of_ref, ob_ref, st_ref, s_scr, *, chunk):
    i = pl.program_id(1)
    n_l = chunk
    n_sub = n_l // GL_SUB
    width = GL_W

    @pl.when(i == 0)
    def _():
        s_scr[...] = s0_ref[0]

    lane = lax.broadcasted_iota(jnp.int32, (GL_SUB, width), 1) >> 6
    head_masks = [lane == h for h in range(GL_HEADS)]
    chunk_lane = lax.broadcasted_iota(jnp.int32, (n_l, width), 1) >> 6
    chunk_masks = [chunk_lane == h for h in range(GL_HEADS)]
    bd_mask = ((lax.broadcasted_iota(jnp.int32, (width, width), 0) >> 6)
               == (lax.broadcasted_iota(jnp.int32, (width, width), 1) >> 6))
    states = [s_scr[0], s_scr[1]]
    new_states, new_outs = [], []
    for d, (qkv_ref, sm_ref, o_ref) in enumerate(((qkv_f, sm_f, of_ref), (qkv_b, sm_b, ob_ref))):
        rev = d == 1
        blk = qkv_ref[0]
        q = blk[:, 0:width]
        k = blk[:, width:2 * width]
        v = blk[:, 2 * width:3 * width]
        a = _bdot(sm_ref[0], w2_ref[...])[:, d * width:(d + 1) * width] + b2_ref[d:d + 1, :]
        la = _log_sigmoid(a) * (1.0 / GL_TAU)
        g = _dot_exact_lhs(_tri(n_l, upper=rev), la)
        last = 0 if rev else n_l - 1
        g_end = g[last:last + 1, :]
        state = states[d]
        o_inter = lax.dot_general((q * jnp.exp(g)).astype(BF16), state.astype(BF16),
                                  (((1,), (1,)), ((), ())), preferred_element_type=F32)
        v_bf = v.astype(BF16)
        a_blocks = []
        t_idx = lax.broadcasted_iota(jnp.int32, (GL_HEADS * GL_SUB, n_l), 0) & (GL_SUB - 1)
        s_idx = lax.broadcasted_iota(jnp.int32, (GL_HEADS * GL_SUB, n_l), 1)
        for s in range(n_sub):
            lo = s * GL_SUB
            hi = lo + GL_SUB
            r = g[hi - 1:hi, :] if rev else g[lo:lo + 1, :]
            qt = q[lo:hi] * jnp.exp(g[lo:hi] - r)
            kt = k * jnp.exp(jnp.minimum(r - g, GL_EXP_CLAMP))
            qstack = jnp.concatenate([jnp.where(hm, qt, 0.0) for hm in head_masks], axis=0)
            amat = lax.dot_general(qstack.astype(BF16), kt.astype(BF16),
                                   (((1,), (1,)), ((), ())), preferred_element_type=F32)
            keep = (s_idx >= lo + t_idx) if rev else (s_idx <= lo + t_idx)
            a_blocks.append(jnp.where(keep, amat, 0.0).astype(BF16))
        a_all = jnp.concatenate([a_blocks[s][h * GL_SUB:(h + 1) * GL_SUB]
                                 for h in range(GL_HEADS) for s in range(n_sub)], axis=0)
        ov = jnp.dot(a_all, v_bf, preferred_element_type=F32)
        o_intra = jnp.where(chunk_masks[0], ov[0:n_l], 0.0)
        for h in range(1, GL_HEADS):
            o_intra = o_intra + jnp.where(chunk_masks[h], ov[h * n_l:(h + 1) * n_l], 0.0)
        new_outs.append(o_inter + o_intra)

        kg = k * jnp.exp(g_end - g)
        upd = jnp.dot(v.T.astype(BF16), kg.astype(BF16), preferred_element_type=F32)
        new_states.append(jnp.exp(g_end) * state + jnp.where(bd_mask, upd, 0.0))
    of_ref[0] = new_outs[0]
    ob_ref[0] = new_outs[1]
    s_scr[0] = new_states[0]
    s_scr[1] = new_states[1]

    @pl.when(i == pl.num_programs(1) - 1)
    def _():
        st_ref[0] = s_scr[...]


def _gla_call(gl, small, w2e, b2, s0):
    b_, t, _ = gl.shape
    chunk = min(GL_CHUNK, t)
    n = t // chunk
    fwd = lambda b, i: (b, i, 0)
    bwd = lambda b, i: (b, n - 1 - i, 0)
    state_spec = pl.BlockSpec((1, 2, GL_W, GL_W), lambda b, i: (b, 0, 0, 0))
    return pl.pallas_call(
        functools.partial(_gla_kernel, chunk=chunk),
        grid=(b_, n),
        in_specs=[pl.BlockSpec((1, chunk, 3 * GL_W), fwd), pl.BlockSpec((1, chunk, SMALL_W), fwd),
                  pl.BlockSpec((1, chunk, 3 * GL_W), bwd), pl.BlockSpec((1, chunk, SMALL_W), bwd),
                  _resident(w2e.shape), _resident(b2.shape), state_spec],
        out_specs=[pl.BlockSpec((1, chunk, GL_W), fwd), pl.BlockSpec((1, chunk, GL_W), bwd), state_spec],
        out_shape=[jax.ShapeDtypeStruct((b_, t, GL_W), F32), jax.ShapeDtypeStruct((b_, t, GL_W), F32),
                   jax.ShapeDtypeStruct(s0.shape, F32)],
        scratch_shapes=[pltpu.VMEM((2, GL_W, GL_W), F32)],
        compiler_params=_cparams("parallel", "arbitrary"),
        name="gla",
    )(gl, small, gl, small, w2e, b2, s0)


def _qkprep_kernel(gq_ref, cos_ref, sin_ref, gq_g, gk_g, sb_ref, qt_out, k_out, vt_out):
    x = gq_ref[0]
    tm = x.shape[0]

    def norm_rope(z, g, reps):
        width = z.shape[1]
        msq = _group_mean(z * z, _group_ones(width))
        zn = z * lax.rsqrt(msq + LN_EPS) * jnp.tile(g, (1, reps))
        lane = lax.broadcasted_iota(jnp.int32, zn.shape, 1)
        partner = jnp.where((lane & 31) < ROPE_PAIRS,
                            pltpu.roll(zn, width - ROPE_PAIRS, axis=1),
                            pltpu.roll(zn, ROPE_PAIRS, axis=1))
        return zn * jnp.tile(cos_ref[...], (1, reps)) + partner * jnp.tile(sin_ref[...], (1, reps))

    lane = lax.broadcasted_iota(jnp.int32, (tm, 128), 1)

    def pad_heads(z, extra):
        out = []
        for p in range(z.shape[1] // 128):
            pair = z[:, p * 128:(p + 1) * 128]
            for base in (pair, pltpu.roll(pair, HEAD_DIM, axis=1)):
                out.append(jnp.where(lane < HEAD_DIM, base, jnp.where(lane == HEAD_DIM, extra, 0.0)))
        return out

    q = norm_rope(x[:, 0:GQ_W], gq_g[...], GQ_W // 128) * (LOG2_E * HEAD_DIM ** -0.5)
    qt_out[0] = jnp.concatenate(pad_heads(q, 1.0), axis=1).T.astype(BF16)
    k = norm_rope(x[:, GQ_W:GQ_W + GQ_KW], gk_g[...], GQ_KW // 128)
    for j, kj in enumerate(pad_heads(k, sb_ref[0:1, 0:1])):
        k_out[0, j] = kj.astype(BF16)
    v = x[:, GQ_W + GQ_KW:GQ_W + 2 * GQ_KW]
    v_t = v.T
    sub = lax.broadcasted_iota(jnp.int32, (V_ROWS - HEAD_DIM, tm), 0)
    ones_rows = jnp.where(sub == 0, 1.0, 0.0)
    vt_out[0] = jnp.concatenate([v_t[0:HEAD_DIM], ones_rows, v_t[HEAD_DIM:2 * HEAD_DIM], ones_rows],
                                axis=0).astype(BF16)


def _qkprep_call(gq, cos, sin, gq_g, gk_g, score_bias, tm):
    b_, t, w = gq.shape
    n_q = GQ_KV * GQ_G
    return pl.pallas_call(
        _qkprep_kernel,
        grid=(b_, t // tm),
        in_specs=[pl.BlockSpec((1, tm, w), lambda b, i: (b, i, 0)),
                  pl.BlockSpec((tm, 128), lambda b, i: (i, 0)),
                  pl.BlockSpec((tm, 128), lambda b, i: (i, 0)),
                  _resident(gq_g.shape), _resident(gk_g.shape), _resident(score_bias.shape)],
        out_specs=[pl.BlockSpec((1, n_q * 128, tm), lambda b, i: (b, 0, i)),
                   pl.BlockSpec((1, GQ_KV, tm, 128), lambda b, i: (b, 0, i, 0)),
                   pl.BlockSpec((1, GQ_KV * V_ROWS, tm), lambda b, i: (b, 0, i))],
        out_shape=[jax.ShapeDtypeStruct((b_, n_q * 128, t), BF16),
                   jax.ShapeDtypeStruct((b_, GQ_KV, t, 128), BF16),
                   jax.ShapeDtypeStruct((b_, GQ_KV * V_ROWS, t), BF16)],
        compiler_params=_cparams("parallel", "parallel"),
        name="qkprep",
    )(gq, cos, sin, gq_g, gk_g, score_bias)


def _attn_kernel(qt_ref, k_ref, vt_ref, o_ref, m_scr, acc_scr, *, n_tiles, tk, bounded):
    acc_scr[...] = jnp.zeros(acc_scr.shape, F32)
    if not bounded:
        m_scr[...] = jnp.full(m_scr.shape, NEG_BIG, F32)

    def body(j, carry):
        start = pl.multiple_of(j * tk, tk)
        k_tile = k_ref[0, 0, pl.ds(start, tk), :]
        vt_tile = vt_ref[0, :, pl.ds(start, tk)]
        for g in range(GQ_G):
            st = jnp.dot(k_tile, qt_ref[0, g * 128:(g + 1) * 128, :], preferred_element_type=F32)
            if bounded:
                acc_scr[g] += jnp.dot(vt_tile, jnp.exp2(st).astype(BF16), preferred_element_type=F32)
            else:
                m_prev = m_scr[g]
                m_new = jnp.maximum(m_prev, jnp.max(st, axis=0, keepdims=True))
                p = jnp.exp2(st - m_new).astype(BF16)
                acc_scr[g] = (jnp.exp2(m_prev - m_new) * acc_scr[g]
                              + jnp.dot(vt_tile, p, preferred_element_type=F32))
                m_scr[g] = m_new
        return carry

    lax.fori_loop(0, n_tiles, body, 0)
    outs = []
    for g in range(GQ_G):
        acc = acc_scr[g]
        outs.append(acc[0:HEAD_DIM] / acc[HEAD_DIM:HEAD_DIM + 1])
    o_ref[0] = jnp.concatenate(outs, axis=0).T


def _attn_call(qt, k, vt, logit_bound):
    b_, _, t = qt.shape
    n_keys = k.shape[2]
    tq = min(ATT_TQ, t)
    tk = ATT_TK if n_keys % ATT_TK == 0 else n_keys
    gw = GQ_G * HEAD_DIM

    def call(bounded):
        return pl.pallas_call(
            functools.partial(_attn_kernel, n_tiles=n_keys // tk, tk=tk, bounded=bounded),
            grid=(b_, GQ_KV, t // tq),
            in_specs=[pl.BlockSpec((1, GQ_G * 128, tq), lambda b, kv, i: (b, kv, i)),
                      pl.BlockSpec((1, 1, n_keys, 128), lambda b, kv, i: (b, kv, 0, 0)),
                      pl.BlockSpec((1, V_ROWS, n_keys), lambda b, kv, i: (b, kv, 0))],
            out_specs=pl.BlockSpec((1, tq, gw), lambda b, kv, i: (b, i, kv)),
            out_shape=jax.ShapeDtypeStruct((b_, t, GQ_W), F32),
            scratch_shapes=[pltpu.VMEM((GQ_G, 1, tq), F32),
                            pltpu.VMEM((GQ_G, V_ROWS, tq), F32)],
            compiler_params=_cparams("parallel", "parallel", "arbitrary"),
            name="attention_bounded" if bounded else "attention_online",
        )(qt, k, vt)

    return lax.cond(logit_bound <= MAX_LOGIT_BOUND, lambda: call(True), lambda: call(False))


def _fourier_factors(t):
    bits = int(round(math.log2(t)))
    assert 1 << bits == t
    n1 = 1 << (bits // 2)
    return n1, t // n1


def _hi_lo(a):
    a = np.asarray(a, np.float64)
    hi = jnp.asarray(a, F32).astype(BF16)
    lo = (jnp.asarray(a, F32) - hi.astype(F32)).astype(BF16)
    return hi, lo


def _fourier_tables(t):
    n1, n2 = _fourier_factors(t)
    c = np.arange(FT_GC)
    ang = 2.0 * np.pi * np.outer(c, c) / FT_GC
    eye = np.eye(FT_GROUPS)
    w0 = np.concatenate([np.kron(eye, np.cos(ang)), -np.kron(eye, np.sin(ang))], axis=1)
    a1 = 2.0 * np.pi * np.outer(np.arange(n1), np.arange(n1)) / n1
    fr, fi = np.cos(a1), -np.sin(a1)
    m1 = np.block([[fr, -fi], [fi, fr]])
    tw = 2.0 * np.pi * np.outer(np.arange(n1), np.arange(n2)) / t
    a2 = 2.0 * np.pi * np.outer(np.arange(n2), np.arange(n2)) / n2
    m2 = np.concatenate([np.cos(a2), np.sin(a2)], axis=1) / math.sqrt(t * FT_GC)
    return dict(n1=n1, n2=n2, w0=_hi_lo(w0), m1=_hi_lo(m1), m2=_hi_lo(m2),
                twr=np.cos(tw).astype(np.float32), twi=(-np.sin(tw)).astype(np.float32))


def _ft_stage1_kernel(u_ref, w0h, w0l, m1h, m1l, twr_ref, twi_ref, yr_ref, yi_ref, *, n_tok):
    n1 = u_ref.shape[1]
    zr, zi = [], []
    for j in range(n_tok):
        uh, ul = _split2(u_ref[0, :, j * FT_W:(j + 1) * FT_W])
        z = _dot3(uh, ul, w0h[...], w0l[...])
        zr.append(z[:, :FT_W])
        zi.append(z[:, FT_W:])
    z = jnp.concatenate([jnp.concatenate(zr, axis=1), jnp.concatenate(zi, axis=1)], axis=0)
    zh, zl = _split2(z)
    y = _dot3(m1h[...], m1l[...], zh, zl)
    y_r, y_i = y[:n1], y[n1:]
    twr, twi = twr_ref[0], twi_ref[0]
    for j in range(n_tok):
        sl = slice(j * FT_W, (j + 1) * FT_W)
        cr, ci = twr[:, j:j + 1], twi[:, j:j + 1]
        yr_ref[0, :, j, :] = y_r[:, sl] * cr - y_i[:, sl] * ci
        yi_ref[0, :, j, :] = y_r[:, sl] * ci + y_i[:, sl] * cr


def _ft_stage2_kernel(yr_ref, yi_ref, m2h, m2l, o_ref, *, n_k1, n2):
    for j in range(n_k1):
        y = jnp.concatenate([yr_ref[0, j * n2:(j + 1) * n2, :], yi_ref[0, j * n2:(j + 1) * n2, :]], axis=0)
        yh, yl = _split2(y)
        o_ref[0, :, j, :] = _dot3(m2h[...], m2l[...], yh, yl)


def _fourier_call(u, tabs):
    b_, t, w = u.shape
    n1, n2 = tabs["n1"], tabs["n2"]
    n_tok = min(FT_TOKENS_PER_STEP, n2)
    twr = jnp.asarray(tabs["twr"]).reshape(n1, n2 // n_tok, n_tok).transpose(1, 0, 2)
    twi = jnp.asarray(tabs["twi"]).reshape(n1, n2 // n_tok, n_tok).transpose(1, 0, 2)
    grid_view = jax.ShapeDtypeStruct((b_, n1, n2, w), F32)
    tok_block = pl.BlockSpec((1, n1, n_tok, w), lambda b, j: (b, 0, j, 0))
    yr, yi = pl.pallas_call(
        functools.partial(_ft_stage1_kernel, n_tok=n_tok),
        grid=(b_, n2 // n_tok),
        in_specs=[pl.BlockSpec((1, n1, n_tok * w), lambda b, j: (b, 0, j)),
                  _resident(tabs["w0"][0].shape), _resident(tabs["w0"][1].shape),
                  _resident(tabs["m1"][0].shape), _resident(tabs["m1"][1].shape),
                  pl.BlockSpec((1, n1, n_tok), lambda b, j: (j, 0, 0)),
                  pl.BlockSpec((1, n1, n_tok), lambda b, j: (j, 0, 0))],
        out_specs=[tok_block] * 2,
        out_shape=[grid_view, grid_view],
        compiler_params=_cparams("parallel", "parallel"),
        name="fourier_stage1",
    )(u.reshape(b_, n1, n2 * w), *tabs["w0"], *tabs["m1"], twr, twi)
    n_k1 = min(FT_TOKENS_PER_STEP, n1)
    out = pl.pallas_call(
        functools.partial(_ft_stage2_kernel, n_k1=n_k1, n2=n2),
        grid=(b_, n1 // n_k1),
        in_specs=[pl.BlockSpec((1, n_k1 * n2, w), lambda b, j: (b, j, 0)),
                  pl.BlockSpec((1, n_k1 * n2, w), lambda b, j: (b, j, 0)),
                  _resident(tabs["m2"][0].shape), _resident(tabs["m2"][1].shape)],
        out_specs=pl.BlockSpec((1, n2, n_k1, w), lambda b, j: (b, 0, j, 0)),
        out_shape=jax.ShapeDtypeStruct((b_, n2, n1, w), F32),
        compiler_params=_cparams("parallel", "parallel"),
        name="fourier_stage2",
    )(yr.reshape(b_, t, w), yi.reshape(b_, t, w), *tabs["m2"])
    return out.reshape(b_, t, w)


def _merge_kernel(x_ref, mod_ref, hf_ref, hb_ref, mlo_ref, ft_ref, att_ref, of_ref, ob_ref, glr_ref,
                  wgate, bgate, wbr, wout, mlg, glg, lng, lnb, o_ref, *, alpha):
    x = x_ref[0]
    h = _modulate(x, mod_ref[0, 0:1, :], mod_ref[0, 1:2, :]).astype(BF16)
    ones = _group_ones(ML_W)
    hs = hf_ref[0] + hb_ref[0]
    hc = hs - _group_mean(hs, ones)
    hn = hc * lax.rsqrt(_group_mean(hc * hc, ones) + LN_EPS) * mlg[...]
    br_ml = hn * _sigmoid(mlo_ref[0])
    os_ = of_ref[0] + ob_ref[0]
    on = os_ * lax.rsqrt(_group_mean(os_ * os_, ones) + LN_EPS) * glg[...]
    br_gl = on * _silu(glr_ref[0])
    branches = (br_ml, ft_ref[0], att_ref[0], br_gl)
    mixed = None
    off = 0
    for j, br in enumerate(branches):
        wd = br.shape[1]
        gate = _sigmoid(jnp.dot(h, wgate[:, j * D_MODEL:(j + 1) * D_MODEL], preferred_element_type=F32)
                        + bgate[:, j * D_MODEL:(j + 1) * D_MODEL])
        u = gate * jnp.dot(br.astype(BF16), wbr[off:off + wd, :], preferred_element_type=F32)
        mixed = u if mixed is None else mixed + u
        off += wd
    y = jnp.dot(mixed.astype(BF16), wout[...], preferred_element_type=F32)
    o_ref[0] = _layernorm(alpha * x + mod_ref[0, 2:3, :] * y) * lng[...] + lnb[...]


def _merge_call(x, mod, hf, hb, ml, ftb, att, of, ob, gl, weights, alpha, tm):
    b_, t, d = x.shape
    tok = lambda w: pl.BlockSpec((1, tm, w), lambda b, i: (b, i, 0))
    last_quarter = pl.BlockSpec((1, tm, ML_W), lambda b, i: (b, i, 3))
    return pl.pallas_call(
        functools.partial(_merge_kernel, alpha=alpha),
        grid=(b_, t // tm),
        in_specs=[tok(d), pl.BlockSpec((1, N_ADA, d), lambda b, i: (b, 0, 0)),
                  tok(ML_W), tok(ML_W), last_quarter, tok(FT_W), tok(GQ_W), tok(GL_W), tok(GL_W),
                  last_quarter] + [_resident(w.shape) for w in weights],
        out_specs=tok(d),
        out_shape=jax.ShapeDtypeStruct((b_, t, d), F32),
        compiler_params=_cparams("parallel", "parallel"),
        name="merge",
    )(x, mod, hf, hb, ml, ftb, att, of, ob, gl, *weights)


def _ffn_kernel(x_ref, mod_ref, wg, wu, wd, lng, lnb, o_ref, *, alpha):
    x = x_ref[0]
    h = _modulate(x, mod_ref[0, 3:4, :], mod_ref[0, 4:5, :]).astype(BF16)
    a = jnp.dot(h, wg[...], preferred_element_type=F32)
    u = jnp.dot(h, wu[...], preferred_element_type=F32)
    f = jnp.dot((_silu(a) * u).astype(BF16), wd[...], preferred_element_type=F32)
    o_ref[0] = _layernorm(alpha * x + mod_ref[0, 5:6, :] * f) * lng[...] + lnb[...]


def _ffn_call(x, mod, weights, alpha, tm):
    b_, t, d = x.shape
    return pl.pallas_call(
        functools.partial(_ffn_kernel, alpha=alpha),
        grid=(b_, t // tm),
        in_specs=[pl.BlockSpec((1, tm, d), lambda b, i: (b, i, 0)),
                  pl.BlockSpec((1, N_ADA, d), lambda b, i: (b, 0, 0))]
                 + [_resident(w.shape) for w in weights],
        out_specs=pl.BlockSpec((1, tm, d), lambda b, i: (b, i, 0)),
        out_shape=jax.ShapeDtypeStruct((b_, t, d), F32),
        compiler_params=_cparams("parallel", "parallel"),
        name="ffn",
    )(x, mod, *weights)


def _router_kernel(x_ref, mod_ref, wrh, wrl, br_ref, h_out, dense_out, pos_out, post_out, cnt_out):
    tm = x_ref.shape[1]
    h = _modulate(x_ref[0], mod_ref[0, 3:4, :], mod_ref[0, 4:5, :])
    h_out[0] = h.astype(BF16)
    hh, hl = _split2(h)
    logits = _dot3(hh, hl, wrh[...], wrl[...])
    lane = lax.broadcasted_iota(jnp.int32, logits.shape, 1).astype(F32)
    sel = jnp.where(lane < N_EXPERTS, logits + br_ref[...], -jnp.inf)

    def pick(scores):
        mx = jnp.max(scores, axis=1, keepdims=True)
        idx = jnp.min(jnp.where(scores == mx, lane, 2.0 * SMALL_W), axis=1, keepdims=True)
        return lane == idx

    first = pick(sel)
    second = pick(jnp.where(first, -jnp.inf, sel))
    l1 = jnp.sum(jnp.where(first, logits, 0.0), axis=1, keepdims=True)
    l2 = jnp.sum(jnp.where(second, logits, 0.0), axis=1, keepdims=True)
    mx = jnp.maximum(l1, l2)
    e1, e2 = jnp.exp(l1 - mx), jnp.exp(l2 - mx)
    inv = 1.0 / (e1 + e2)
    dense_out[0] = jnp.where(first, e1 * inv, 0.0) + jnp.where(second, e2 * inv, 0.0)
    routed = first | second
    counts = jnp.dot(_tri(tm, upper=False), jnp.where(routed, 1.0, 0.0).astype(BF16),
                     preferred_element_type=F32)
    pos = jnp.where(routed, counts - 1.0, -1.0)
    pos_out[0] = pos
    post_out[0] = pos.T[0:8, :]
    cnt_out[0, 0] = jnp.broadcast_to(counts[tm - 1:tm, :], (8, SMALL_W))


def _router_call(x, mod, wrh, wrl, br, tm):
    b_, t, d = x.shape
    nt = t // tm
    tok = lambda w: pl.BlockSpec((1, tm, w), lambda b, i: (b, i, 0))
    return pl.pallas_call(
        _router_kernel,
        grid=(b_, nt),
        in_specs=[tok(d), pl.BlockSpec((1, N_ADA, d), lambda b, i: (b, 0, 0)),
                  _resident(wrh.shape), _resident(wrl.shape), _resident(br.shape)],
        out_specs=[tok(d), tok(SMALL_W), tok(SMALL_W),
                   pl.BlockSpec((1, 8, tm), lambda b, i: (b, 0, i)),
                   pl.BlockSpec((1, 1, 8, SMALL_W), lambda b, i: (b, i, 0, 0))],
        out_shape=[jax.ShapeDtypeStruct((b_, t, d), BF16), jax.ShapeDtypeStruct((b_, t, SMALL_W), F32),
                   jax.ShapeDtypeStruct((b_, t, SMALL_W), F32), jax.ShapeDtypeStruct((b_, 8, t), F32),
                   jax.ShapeDtypeStruct((b_, nt, 8, SMALL_W), F32)],
        compiler_params=_cparams("parallel", "parallel"),
        name="router",
    )(x, mod, wrh, wrl, br)


def _moe_kernel(cnt_ref, x_ref, mod_ref, h_ref, dense_ref, pos_ref, post_ref, wg, wu, wd, lng, lnb,
                o_ref, acc_scr, *, alpha, rows):
    b, i, e = pl.program_id(0), pl.program_id(1), pl.program_id(2)
    tm = x_ref.shape[1]

    @pl.when(e == 0)
    def _():
        acc_scr[...] = jnp.zeros(acc_scr.shape, F32)

    lane = lax.broadcasted_iota(jnp.int32, (tm, SMALL_W), 1)
    w_col = jnp.sum(jnp.where(lane == e, dense_ref[0], 0.0), axis=1, keepdims=True)
    pos_col = jnp.sum(jnp.where(lane == e, pos_ref[0], 0.0), axis=1, keepdims=True)
    pos_row = post_ref[0, pl.ds(e, 1), :]
    count = cnt_ref[(b * pl.num_programs(1) + i) * N_EXPERTS + e]

    def run_block(first_row, n):
        base = first_row.astype(F32)
        r_iota = lax.broadcasted_iota(jnp.int32, (n, 1), 0).astype(F32) + base
        c_iota = lax.broadcasted_iota(jnp.int32, (1, n), 1).astype(F32) + base
        gather = jnp.where(pos_row == r_iota, 1.0, 0.0).astype(BF16)
        xs = jnp.dot(gather, h_ref[0], preferred_element_type=F32).astype(BF16)
        a = jnp.dot(xs, wg[0], preferred_element_type=F32)
        u = jnp.dot(xs, wu[0], preferred_element_type=F32)
        f = jnp.dot((_silu(a) * u).astype(BF16), wd[0], preferred_element_type=F32)
        scatter = jnp.where(pos_col == c_iota, 1.0, 0.0).astype(BF16)
        acc_scr[...] += w_col * jnp.dot(scatter, f.astype(BF16), preferred_element_type=F32)

    n_full = count // rows
    rem = count - n_full * rows
    half = rows // 2

    def full_block(j, carry):
        run_block(j * rows, rows)
        return carry

    lax.fori_loop(0, n_full, full_block, 0)

    @pl.when(rem > half)
    def _():
        run_block(n_full * rows, rows)

    @pl.when((rem > 0) & (rem <= half))
    def _():
        run_block(n_full * rows, half)

    @pl.when(e == pl.num_programs(2) - 1)
    def _():
        o_ref[0] = (_layernorm(alpha * x_ref[0] + mod_ref[0, 5:6, :] * acc_scr[...]) * lng[...]
                    + lnb[...])


def _moe_call(x, mod, wrh, wrl, br, wg, wu, wd, lng, lnb, alpha, tm):
    b_, t, d = x.shape
    n_e, _, ff = wg.shape
    nt = t // tm
    h, dense, pos, post, cnt = _router_call(x, mod, wrh, wrl, br, tm)
    counts = cnt[:, :, 0, 0:n_e].astype(jnp.int32).reshape(-1)
    tok = lambda w: pl.BlockSpec((1, tm, w), lambda b, i, e, c: (b, i, 0))
    grid_spec = pltpu.PrefetchScalarGridSpec(
        num_scalar_prefetch=1,
        grid=(b_, nt, n_e),
        in_specs=[tok(d), pl.BlockSpec((1, N_ADA, d), lambda b, i, e, c: (b, 0, 0)),
                  tok(d), tok(SMALL_W), tok(SMALL_W),
                  pl.BlockSpec((1, 8, tm), lambda b, i, e, c: (b, 0, i)),
                  pl.BlockSpec((1, d, ff), lambda b, i, e, c: (e, 0, 0)),
                  pl.BlockSpec((1, d, ff), lambda b, i, e, c: (e, 0, 0)),
                  pl.BlockSpec((1, ff, d), lambda b, i, e, c: (e, 0, 0)),
                  _resident(lng.shape), _resident(lnb.shape)],
        out_specs=tok(d),
        scratch_shapes=[pltpu.VMEM((tm, d), F32)])
    return pl.pallas_call(
        functools.partial(_moe_kernel, alpha=alpha, rows=min(MOE_ROWS, tm)),
        grid_spec=grid_spec,
        out_shape=jax.ShapeDtypeStruct((b_, t, d), F32),
        compiler_params=_cparams("parallel", "parallel", "arbitrary"),
        name="moe",
    )(counts, x, mod, h, dense, pos, post, wg, wu, wd, lng, lnb)


def _rope_tables(t):
    rows = t // GRID_W
    row = jnp.repeat(jnp.arange(rows, dtype=F32), GRID_W)
    col = jnp.tile(jnp.arange(GRID_W, dtype=F32), rows)
    inv = jnp.power(ROPE_BASE, -jnp.arange(ROPE_PAIRS, dtype=F32) / ROPE_PAIRS)
    ar, ac = row[:, None] * inv, col[:, None] * inv
    cos = jnp.concatenate([jnp.cos(ar), jnp.cos(ar), jnp.cos(ac), jnp.cos(ac)], axis=1)
    sin = jnp.concatenate([-jnp.sin(ar), jnp.sin(ar), -jnp.sin(ac), jnp.sin(ac)], axis=1)
    return jnp.tile(cos, (1, 2)), jnp.tile(sin, (1, 2))


def _token_tile(t, pref):
    return pref if t % pref == 0 else t


def kernel(x, c, ctx, c_ctx, w_ada, b_ada, w_in, ml_gate_b, ml_norm_g, gq_qnorm_g, gq_knorm_g, gl_w2, gl_b2,
           gl_norm_g, w_branch, w_gate, b_gate, w_out, ln1_g, ln1_b, ln2_g, ln2_b, ffd_wg, ffd_wu, ffd_wd,
           moe_wr, moe_br, moe_wg, moe_wu, moe_wd):
    b_, s_len, d = x.shape
    n_ctx = ctx.shape[1]
    depth = w_in.shape[0]
    alpha = (2.0 * depth) ** 0.25
    bf = lambda a: a.astype(BF16)
    row = lambda a: a.reshape(1, -1).astype(F32)

    cos_l, sin_l = _rope_tables(s_len)
    cos_c, sin_c = jnp.ones((n_ctx, 128), F32), jnp.zeros((n_ctx, 128), F32)
    ft_l, ft_c = _fourier_tables(s_len), _fourier_tables(n_ctx)
    tm_l, tm_c = _token_tile(s_len, 512), _token_tile(n_ctx, 512)

    cc = jnp.zeros((8, d), F32).at[0:b_].set(c).at[b_].set(c_ctx)
    xc = ctx
    for l in range(depth):
        last = l == depth - 1
        ada = _ada_call(cc, bf(w_ada[l]), row(b_ada[l])).reshape(8, N_ADA, d)
        mod_l = ada[0:b_]
        mod_c = jnp.broadcast_to(ada[b_][None], (b_, N_ADA, d))

        w = w_in[l]
        o_gate, o_ft, o_gq, o_gl, o_af = 4 * ML_W, 4 * ML_W + 16, 4 * ML_W + 16 + FT_W, \
            4 * ML_W + 16 + FT_W + GQ_W + 2 * GQ_KW, 4 * ML_W + 16 + FT_W + GQ_W + 2 * GQ_KW + 4 * GL_W
        ml_scale = jnp.concatenate([jnp.ones((ML_W,)), jnp.full((ML_W,), HEAD_DIM ** -0.5), jnp.ones((2 * ML_W,))])
        gl_scale = jnp.concatenate([jnp.full((GL_W,), HEAD_DIM ** -0.5), jnp.ones((3 * GL_W,))])
        w_small = jnp.concatenate([w[:, o_gate:o_gate + 16], w[:, o_af:o_af + 2 * GL_RANK],
                                   jnp.zeros((d, SMALL_W - 16 - 2 * GL_RANK), F32)], axis=1)
        ws = [bf(w[:, 0:4 * ML_W] * ml_scale), bf(w[:, o_ft:o_ft + FT_W]), bf(w[:, o_gq:o_gl]),
              bf(w[:, o_gl:o_af] * gl_scale), bf(w_small)]
        gate_bias = jnp.zeros((1, SMALL_W), F32).at[0, 0:16].set(ml_gate_b[l].reshape(-1))
        w2e = jnp.zeros((SMALL_W, 2 * GL_W), F32)
        w2e = w2e.at[16:16 + GL_RANK, 0:GL_W].set(gl_w2[l, 0]).at[16 + GL_RANK:16 + 2 * GL_RANK, GL_W:].set(gl_w2[l, 1])
        merge_w = [bf(w_gate[l]), row(b_gate[l]), bf(w_branch[l]), bf(w_out[l]), row(ml_norm_g[l]),
                   row(gl_norm_g[l]), row(ln1_g[l]), row(ln1_b[l])]
        gq_g = jnp.tile(row(gq_qnorm_g[l]), (1, 2))
        gk_g = jnp.tile(row(gq_knorm_g[l]), (1, 2))
        logit_bound = (1.02 * LOG2_E * HEAD_DIM ** 0.5) * jnp.max(jnp.abs(gq_qnorm_g[l])) * jnp.max(jnp.abs(gq_knorm_g[l]))
        score_bias = jnp.zeros((1, HEAD_DIM), F32).at[0, 0].set(-logit_bound)

        def mixers(xs, mod, tm, cos, sin, states):
            ml, ftu, gq, gl, small = _inproj_call(xs, mod, ws, tm)
            hf, hb, ml_s, ml_m = _mlstm_call(ml, small, gate_bias, states[0], states[1])
            of, ob, gl_s = _gla_call(gl, small, bf(w2e), gl_b2[l].astype(F32), states[2])
            qt, k, vt = _qkprep_call(gq, cos, sin, gq_g, gk_g, score_bias, tm)
            return dict(ml=ml, ftu=ftu, gl=gl, hf=hf, hb=hb, of=of, ob=ob, qt=qt, k=k, vt=vt), (ml_s, ml_m, gl_s)

        zero_states = (jnp.zeros((b_, 2 * ML_HEADS, 2 * HEAD_DIM, 2 * HEAD_DIM), F32),
                       jnp.zeros((b_, 2 * ML_HEADS, 1, SMALL_W), F32),
                       jnp.zeros((b_, 2, GL_W, GL_W), F32))
        pc, ctx_states = mixers(xc, mod_c, tm_c, cos_c, sin_c, zero_states)
        pl_, _ = mixers(x, mod_l, tm_l, cos_l, sin_l, ctx_states)

        att_l = _attn_call(pl_["qt"], jnp.concatenate([pc["k"], pl_["k"]], axis=2),
                           jnp.concatenate([pc["vt"], pl_["vt"]], axis=2), logit_bound)
        x_mid = _merge_call(x, mod_l, pl_["hf"], pl_["hb"], pl_["ml"], _fourier_call(pl_["ftu"], ft_l), att_l,
                            pl_["of"], pl_["ob"], pl_["gl"], merge_w, alpha, tm_l)
        if not last:
            att_c = _attn_call(pc["qt"], pc["k"], pc["vt"], logit_bound)
            xc = _merge_call(xc, mod_c, pc["hf"], pc["hb"], pc["ml"], _fourier_call(pc["ftu"], ft_c), att_c,
                             pc["of"], pc["ob"], pc["gl"], merge_w, alpha, tm_c)

        j = l // 2
        if l % 2 == 0:
            ffn_w = [bf(ffd_wg[j]), bf(ffd_wu[j]), bf(ffd_wd[j]), row(ln2_g[l]), row(ln2_b[l])]
            x = _ffn_call(x_mid, mod_l, ffn_w, alpha, tm_l)
            if not last:
                xc = _ffn_call(xc, mod_c, ffn_w, alpha, tm_c)
        else:
            wr = jnp.zeros((d, SMALL_W), F32).at[:, 0:N_EXPERTS].set(moe_wr[j])
            wrh = bf(wr)
            wrl = bf(wr - wrh.astype(F32))
            brp = jnp.zeros((1, SMALL_W), F32).at[0, 0:N_EXPERTS].set(moe_br[j])
            moe_args = (wrh, wrl, brp, bf(moe_wg[j]), bf(moe_wu[j]), bf(moe_wd[j]), row(ln2_g[l]), row(ln2_b[l]))
            x = _moe_call(x_mid, mod_l, *moe_args, alpha, _token_tile(s_len, 1024))
            if not last:
                xc = _moe_call(xc, mod_c, *moe_args, alpha, tm_c)
    return x
```
